```python
import math
import jax, jax.numpy as jnp
from jax import lax
import numpy as np

D_MODEL = 1024
BATCH = 4
SEQ = 4096
DEPTH = 1
DEC_BATCH = 128
DEC_SEQ = 8
PAST_LEN = 2048
PAGE_SIZE = 128

A_HEADS = 8
A_KV_HEADS = 4
A_DH = 64
A_GROUP = A_HEADS // A_KV_HEADS
IDX_HEADS = 4
IDX_DH = 64
TOPK_MAX = 256
Q_BLOCK = 128
G_HEADS = 4
G_DK = 128
G_DV = 128
CONV_W = 4
CONV_CH = 2 * G_HEADS * G_DK + G_HEADS * G_DV
G_CHUNK = 64
MEM_TOKENS = 256
M_HEADS = 4
M_DH = 128
D_FF = -(-8 * D_MODEL // (3 * 256)) * 256
EPS = 1e-6

SPLIT_SIZES = (A_HEADS * A_DH, A_KV_HEADS * A_DH, A_KV_HEADS * A_DH,
               IDX_HEADS * IDX_DH, IDX_DH, IDX_HEADS,
               CONV_CH, G_HEADS * G_DV, G_HEADS, G_HEADS,
               M_HEADS * M_DH,
               3 * D_MODEL)
D_IN = sum(SPLIT_SIZES)

kernel_name = 'hybrid_dsa_gdn_memory_step'


def rms_norm(x, gain):
    xf = x.astype(jnp.float32)
    y = xf * lax.rsqrt(jnp.mean(xf * xf, axis=-1, keepdims=True) + EPS)
    return (y * gain.astype(jnp.float32)).astype(x.dtype)


def l2_norm(x):
    xf = x.astype(jnp.float32)
    return xf * lax.rsqrt(jnp.sum(xf * xf, axis=-1, keepdims=True) + EPS)


def split_projection(h):
    offs = np.cumsum(SPLIT_SIZES)[:-1].tolist()
    return jnp.split(h, offs, axis=-1)


def project_inputs(x, norm_mix, w_in, a_q_norm, a_k_norm, m_q_norm):
    B, T, _ = x.shape
    h = rms_norm(x, norm_mix) @ w_in
    aq, ak, av, iq, ik, iw, gqkv, gz, gb, ga, mq, gates = split_projection(h)
    aq = rms_norm(aq.reshape(B, T, A_HEADS, A_DH), a_q_norm)
    ak = rms_norm(ak.reshape(B, T, A_KV_HEADS, A_DH), a_k_norm)
    av = av.reshape(B, T, A_KV_HEADS, A_DH)
    iq = iq.reshape(B, T, IDX_HEADS, IDX_DH)
    iw = iw * IDX_HEADS ** -0.5
    mq = rms_norm(mq.reshape(B, T, M_HEADS, M_DH), m_q_norm)
    return aq, ak, av, iq, ik, iw, gqkv, gz, gb, ga, mq, gates


def indexer_topk(qi, wi, ki, q_pos, k_top):
    L = ki.shape[1]
    dots = jnp.einsum('bthd,bsd->bths', qi, ki).astype(jnp.float32) * IDX_DH ** -0.5
    score = jnp.einsum('bths,bth->bts', jax.nn.relu(dots), wi.astype(jnp.float32))
    admissible = jnp.arange(L)[None, :] <= q_pos[:, None]
    score = jnp.where(admissible[None], score, -jnp.inf)
    _, idx = lax.top_k(score, k_top)
    valid = idx <= q_pos[None, :, None]
    return idx, valid


def sparse_attend(q, k_sel, v_sel, valid):
    B, T = q.shape[:2]
    qg = q.reshape(B, T, A_KV_HEADS, A_GROUP, A_DH)
    s = jnp.einsum('btgrd,btkgd->btgrk', qg, k_sel).astype(jnp.float32) * A_DH ** -0.5
    s = jnp.where(valid[:, :, None, None, :], s, -jnp.inf)
    p = jax.nn.softmax(s, axis=-1).astype(v_sel.dtype)
    o = jnp.einsum('btgrk,btkgd->btgrd', p, v_sel)
    return o.reshape(B, T, A_HEADS * A_DH)


def prompt_sparse_attention(q, k, v, qi, wi, ki):
    B, S = q.shape[:2]
    k_top = min(TOPK_MAX, S // 4)
    nb = S // Q_BLOCK

    def blocks(a):
        return jnp.moveaxis(a.reshape((B, nb, Q_BLOCK) + a.shape[2:]), 1, 0)

    take = jax.vmap(lambda rows, idx: rows[idx])

    def one_block(args):
        q_b, qi_b, wi_b, start = args
        q_pos = start + jnp.arange(Q_BLOCK)
        idx, valid = indexer_topk(qi_b, wi_b, ki, q_pos, k_top)
        return sparse_attend(q_b, take(k, idx), take(v, idx), valid)

    out = lax.map(one_block, (blocks(q), blocks(qi), blocks(wi), jnp.arange(nb) * Q_BLOCK))
    return jnp.moveaxis(out, 0, 1).reshape(B, S, A_HEADS * A_DH)


def sample_sparse_attention(q, k_new, v_new, qi, wi, ki_new, pool_k, pool_v, pool_ki, page_table):
    DB, T = q.shape[:2]
    past = page_table.shape[1] * PAGE_SIZE
    k_top = min(TOPK_MAX, (past + T) // 4)
    ki_past = pool_ki[page_table].reshape(DB, past, IDX_DH)
    ki_all = jnp.concatenate([ki_past, ki_new.astype(ki_past.dtype)], axis=1)
    q_pos = past + jnp.arange(T)
    idx, valid = indexer_topk(qi, wi, ki_all, q_pos, k_top)
    b = jnp.arange(DB)[:, None, None]
    in_past = (idx < past)[..., None, None]
    pidx = jnp.minimum(idx, past - 1)
    phys = page_table[b, pidx // PAGE_SIZE]
    off = pidx % PAGE_SIZE
    nidx = jnp.clip(idx - past, 0, T - 1)
    k_sel = jnp.where(in_past, pool_k[phys, off], k_new[b, nidx].astype(pool_k.dtype))
    v_sel = jnp.where(in_past, pool_v[phys, off], v_new[b, nidx].astype(pool_v.dtype))
    return sparse_attend(q, k_sel, v_sel, valid)


def short_conv(u, buf, w):
    T = u.shape[1]
    ucat = jnp.concatenate([buf.astype(u.dtype), u], axis=1)
    out = sum(ucat[:, j:j + T] * w[j] for j in range(CONV_W))
    return jax.nn.silu(out), ucat[:, -(CONV_W - 1):]


def gated_delta_chunked(q, k, v, beta, g, s0):
    B, T, H, DK = q.shape
    DV = v.shape[-1]
    C = min(G_CHUNK, T)
    n = -(-T // C)
    pad = n * C - T

    def prep(a):
        a = jnp.pad(a, [(0, 0), (0, pad)] + [(0, 0)] * (a.ndim - 2))
        return jnp.moveaxis(a.reshape((B, n, C) + a.shape[2:]), 3, 2)

    q, k, v, beta, g = (prep(a) for a in (q, k, v, beta, g))
    q = q * DK ** -0.5
    gc = jnp.cumsum(g, axis=-1)
    i = jnp.arange(C)
    causal = i[:, None] >= i[None, :]
    strict = i[:, None] > i[None, :]
    gamma = jnp.exp(jnp.where(causal, gc[..., :, None] - gc[..., None, :], -jnp.inf))
    kb = k * beta[..., None]
    m = jnp.where(strict, jnp.einsum('bnhid,bnhjd->bnhij', kb, k) * gamma, 0.0)
    a = m + jnp.eye(C, dtype=m.dtype)
    rhs = jnp.concatenate([v * beta[..., None], kb * jnp.exp(gc)[..., None]], axis=-1)
    sol = lax.linalg.triangular_solve(a, rhs, left_side=True, lower=True, unit_diagonal=True)
    u, w = sol[..., :DV], sol[..., DV:]
    aqk = jnp.einsum('bnhid,bnhjd->bnhij', q, k) * gamma
    dq = q * jnp.exp(gc)[..., None]
    g_last = gc[..., -1]
    k_tail = k * jnp.exp(g_last[..., None] - gc)[..., None]

    def step(S, xs):
        u_c, w_c, aqk_c, dq_c, kt_c, gl_c = xs
        v_new = u_c - jnp.einsum('bhck,bhkv->bhcv', w_c, S)
        o = jnp.einsum('bhck,bhkv->bhcv', dq_c, S) + jnp.einsum('bhij,bhjv->bhiv', aqk_c, v_new)
        S = S * jnp.exp(gl_c)[..., None, None] + jnp.einsum('bhck,bhcv->bhkv', kt_c, v_new)
        return S, o

    xs = tuple(jnp.moveaxis(t, 1, 0) for t in (u, w, aqk, dq, k_tail, g_last))
    S, o = lax.scan(step, s0, xs)
    o = jnp.transpose(o, (1, 0, 3, 2, 4)).reshape(B, n * C, H, DV)[:, :T]
    return o, S


def gdn_branch(gqkv, gz, gb, ga, conv_buf, s0, g_conv, g_a_log, g_dt_bias, g_o_norm):
    B, T, _ = gqkv.shape
    u, new_buf = short_conv(gqkv, conv_buf, g_conv)
    q, k, v = jnp.split(u, [G_HEADS * G_DK, 2 * G_HEADS * G_DK], axis=-1)
    q = l2_norm(q.reshape(B, T, G_HEADS, G_DK))
    k = l2_norm(k.reshape(B, T, G_HEADS, G_DK))
    v = v.reshape(B, T, G_HEADS, G_DV).astype(jnp.float32)
    beta = jax.nn.sigmoid(gb.astype(jnp.float32))
    g = -jnp.exp(g_a_log.astype(jnp.float32)) * jax.nn.softplus(ga.astype(jnp.float32) + g_dt_bias.astype(jnp.float32))
    o, s_new = gated_delta_chunked(q, k, v, beta, g, s0.astype(jnp.float32))
    o = rms_norm(o, g_o_norm) * jax.nn.silu(gz.reshape(B, T, G_HEADS, G_DV).astype(jnp.float32))
    return o.reshape(B, T, G_HEADS * G_DV).astype(gqkv.dtype), new_buf, s_new.astype(gqkv.dtype)


def memory_kv(mem, mem_norm, w_mem_kv, m_k_norm):
    B, M, _ = mem.shape
    mk, mv = jnp.split(rms_norm(mem, mem_norm) @ w_mem_kv, 2, axis=-1)
    mk = rms_norm(mk.reshape(B, M, M_HEADS, M_DH), m_k_norm)
    return mk, mv.reshape(B, M, M_HEADS, M_DH)


def memory_attend(q, mk, mv):
    B, T = q.shape[:2]
    s = jnp.einsum('bthd,bmhd->bhtm', q, mk).astype(jnp.float32) * M_DH ** -0.5
    p = jax.nn.softmax(s, axis=-1).astype(mv.dtype)
    return jnp.einsum('bhtm,bmhd->bthd', p, mv).reshape(B, T, M_HEADS * M_DH)


def merge_and_ffn(x, a_out, g_out, m_out, gates, w_a_out, w_g_out, w_m_out, w_o, norm_ffn, w_ffn_in, w_ffn_out):
    ga_, gg_, gm_ = jnp.split(jax.nn.sigmoid(gates), 3, axis=-1)
    h = ga_ * (a_out @ w_a_out) + gg_ * (g_out @ w_g_out) + gm_ * (m_out @ w_m_out)
    x = x + h @ w_o
    gate, up = jnp.split(rms_norm(x, norm_ffn) @ w_ffn_in, 2, axis=-1)
    return x + (jax.nn.silu(gate) * up) @ w_ffn_out


def prompt_layer(x, mem, lw):
    (norm_mix, w_in, a_q_norm, a_k_norm, g_conv, g_a_log, g_dt_bias, g_o_norm, mem_norm, w_mem_kv,
     m_q_norm, m_k_norm, w_a_out, w_g_out, w_m_out, w_o, norm_ffn, w_ffn_in, w_ffn_out) = lw
    B = x.shape[0]
    aq, ak, av, iq, ik, iw, gqkv, gz, gb, ga, mq, gates = project_inputs(x, norm_mix, w_in, a_q_norm, a_k_norm, m_q_norm)
    a_out = prompt_sparse_attention(aq, ak, av, iq, iw, ik)
    buf0 = jnp.zeros((B, CONV_W - 1, CONV_CH), x.dtype)
    s0 = jnp.zeros((B, G_HEADS, G_DK, G_DV), jnp.float32)
    g_out, conv_buf, s_gdn = gdn_branch(gqkv, gz, gb, ga, buf0, s0, g_conv, g_a_log, g_dt_bias, g_o_norm)
    mk, mv = memory_kv(mem, mem_norm, w_mem_kv, m_k_norm)
    m_out = memory_attend(mq, mk, mv)
    y = merge_and_ffn(x, a_out, g_out, m_out, gates, w_a_out, w_g_out, w_m_out, w_o, norm_ffn, w_ffn_in, w_ffn_out)
    return y, (ak, av, ik, s_gdn, conv_buf, mk, mv)


def sample_layer(x, c_k, c_v, c_ik, page_table, s_gdn, s_conv, c_mk, c_mv, lw):
    (norm_mix, w_in, a_q_norm, a_k_norm, g_conv, g_a_log, g_dt_bias, g_o_norm, mem_norm, w_mem_kv,
     m_q_norm, m_k_norm, w_a_out, w_g_out, w_m_out, w_o, norm_ffn, w_ffn_in, w_ffn_out) = lw
    aq, ak, av, iq, ik, iw, gqkv, gz, gb, ga, mq, gates = project_inputs(x, norm_mix, w_in, a_q_norm, a_k_norm, m_q_norm)
    a_out = sample_sparse_attention(aq, ak, av, iq, iw, ik, c_k, c_v, c_ik, page_table)
    g_out, conv_buf, s_new = gdn_branch(gqkv, gz, gb, ga, s_conv, s_gdn, g_conv, g_a_log, g_dt_bias, g_o_norm)
    m_out = memory_attend(mq, c_mk, c_mv)
    y = merge_and_ffn(x, a_out, g_out, m_out, gates, w_a_out, w_g_out, w_m_out, w_o, norm_ffn, w_ffn_in, w_ffn_out)
    return y, (ak, av, ik, s_new, conv_buf)


def setup_inputs(seed: int = 0) -> dict:
    key = jax.random.key(seed)
    keys = jax.random.split(key, 48)
    counter = [0]

    def nxt():
        counter[0] += 1
        return keys[counter[0] - 1]

    def nrm(shape, scale=1.0):
        return jax.random.normal(nxt(), shape, jnp.float32) * scale

    def gain(shape):
        return 1.0 + 0.02 * jax.random.normal(nxt(), shape, jnp.float32)

    n_pages = PAST_LEN // PAGE_SIZE
    n_used = DEC_BATCH * n_pages
    n_phys = n_used + max(1, n_used // 4)
    page_table = jax.random.permutation(nxt(), n_phys)[:n_used].reshape(DEC_BATCH, n_pages).astype(jnp.int32)
    a_log = jnp.log(jax.random.uniform(nxt(), (DEPTH, G_HEADS), jnp.float32, 1.0, 16.0))
    dt = jnp.exp(jax.random.uniform(nxt(), (DEPTH, G_HEADS), jnp.float32, math.log(1e-3), math.log(1e-1)))
    dt_bias = dt + jnp.log(-jnp.expm1(-dt))
    d_br = A_HEADS * A_DH
    return {
        'x_prompt': nrm((BATCH, SEQ, D_MODEL)),
        'x_sample': nrm((DEC_BATCH, DEC_SEQ, D_MODEL)),
        'mem_prompt': nrm((BATCH, MEM_TOKENS, D_MODEL)),
        'cache_k': nrm((DEPTH, n_phys, PAGE_SIZE, A_KV_HEADS, A_DH)),
        'cache_v': nrm((DEPTH, n_phys, PAGE_SIZE, A_KV_HEADS, A_DH)),
        'cache_idx_k': nrm((DEPTH, n_phys, PAGE_SIZE, IDX_DH)),
        'page_table': page_table,
        'state_gdn': nrm((DEPTH, DEC_BATCH, G_HEADS, G_DK, G_DV), 0.1),
        'state_conv': nrm((DEPTH, DEC_BATCH, CONV_W - 1, CONV_CH)),
        'cache_mem_k': nrm((DEPTH, DEC_BATCH, MEM_TOKENS, M_HEADS, M_DH)),
        'cache_mem_v': nrm((DEPTH, DEC_BATCH, MEM_TOKENS, M_HEADS, M_DH)),
        'norm_mix': gain((DEPTH, D_MODEL)),
        'w_in': nrm((DEPTH, D_MODEL, D_IN), D_MODEL ** -0.5),
        'a_q_norm': gain((DEPTH, A_DH)),
        'a_k_norm': gain((DEPTH, A_DH)),
        'g_conv': nrm((DEPTH, CONV_W, CONV_CH), CONV_W ** -0.5),
        'g_a_log': a_log,
        'g_dt_bias': dt_bias,
        'g_o_norm': gain((DEPTH, G_DV)),
        'mem_norm': gain((DEPTH, D_MODEL)),
        'w_mem_kv': nrm((DEPTH, D_MODEL, 2 * M_HEADS * M_DH), D_MODEL ** -0.5),
        'm_q_norm': gain((DEPTH, M_DH)),
        'm_k_norm': gain((DEPTH, M_DH)),
        'w_a_out': nrm((DEPTH, d_br, D_MODEL), d_br ** -0.5),
        'w_g_out': nrm((DEPTH, G_HEADS * G_DV, D_MODEL), (G_HEADS * G_DV) ** -0.5),
        'w_m_out': nrm((DEPTH, M_HEADS * M_DH, D_MODEL), (M_HEADS * M_DH) ** -0.5),
        'w_o': nrm((DEPTH, D_MODEL, D_MODEL), D_MODEL ** -0.5),
        'norm_ffn': gain((DEPTH, D_MODEL)),
        'w_ffn_in': nrm((DEPTH, D_MODEL, 2 * D_FF), D_MODEL ** -0.5),
        'w_ffn_out': nrm((DEPTH, D_FF, D_MODEL), D_FF ** -0.5),
    }


def reference(x_prompt, x_sample, mem_prompt, cache_k, cache_v, cache_idx_k, page_table, state_gdn, state_conv,
              cache_mem_k, cache_mem_v, norm_mix, w_in, a_q_norm, a_k_norm, g_conv, g_a_log, g_dt_bias, g_o_norm,
              mem_norm, w_mem_kv, m_q_norm, m_k_norm, w_a_out, w_g_out, w_m_out, w_o, norm_ffn, w_ffn_in, w_ffn_out):
    yp, ys = x_prompt, x_sample
    p_states, s_states = [], []
    for l in range(DEPTH):
        lw = (norm_mix[l], w_in[l], a_q_norm[l], a_k_norm[l], g_conv[l], g_a_log[l], g_dt_bias[l], g_o_norm[l],
              mem_norm[l], w_mem_kv[l], m_q_norm[l], m_k_norm[l], w_a_out[l], w_g_out[l], w_m_out[l], w_o[l],
              norm_ffn[l], w_ffn_in[l], w_ffn_out[l])
        yp, ps = prompt_layer(yp, mem_prompt, lw)
        ys, ss = sample_layer(ys, cache_k[l], cache_v[l], cache_idx_k[l], page_table, state_gdn[l], state_conv[l],
                              cache_mem_k[l], cache_mem_v[l], lw)
        p_states.append(ps)
        s_states.append(ss)
    p_k, p_v, p_idx_k, p_gdn, p_conv, p_mem_k, p_mem_v = [jnp.stack(t) for t in zip(*p_states)]
    s_k, s_v, s_idx_k, s_gdn, s_conv = [jnp.stack(t) for t in zip(*s_states)]
    return (yp, ys, p_k, p_v, p_idx_k, p_gdn, p_conv, p_mem_k, p_mem_v, s_k, s_v, s_idx_k, s_gdn, s_conv)
```

```python
import functools

import numpy as np
import jax
import jax.numpy as jnp
from jax import lax
from jax.experimental import pallas as pl
from jax.experimental.pallas import tpu as pltpu

F32 = jnp.float32
BF16 = jnp.bfloat16
I32 = jnp.int32

A_HEADS = 8
A_KV_HEADS = 4
A_DH = 64
IDX_HEADS = 4
IDX_DH = 64
TOPK_MAX = 256
G_HEADS = 4
G_DK = 128
G_DV = 128
CONV_W = 4
G_CHUNK = 64
M_HEADS = 4
M_DH = 128
EPS = 1e-6

LANES = 128
VMEM_LIMIT = 56 * 1024 * 1024
INT_MIN = np.int32(-2 ** 31)

MISC_IK = 0
MISC_IW = IDX_DH
MISC_GB = MISC_IW + IDX_HEADS
MISC_GA = MISC_GB + G_HEADS

NN = (((1,), (0,)), ((), ()))
NT = (((1,), (1,)), ((), ()))
TN = (((0,), (0,)), ((), ()))


def _dg(a, b, dn=NN):
    return lax.dot_general(a, b, dn, preferred_element_type=F32)


def _split2(x):
    hi = x.astype(BF16)
    lo = (x - hi.astype(F32)).astype(BF16)
    return hi, lo


def _dot3(a, b, dn=NN):
    ah, al = _split2(a)
    bh, bl = _split2(b)
    return _dg(ah, bh, dn) + (_dg(ah, bl, dn) + _dg(al, bh, dn))


def _sigmoid(x):
    return 1.0 / (1.0 + jnp.exp(-x))


def _silu(x):
    return x * _sigmoid(x)


def _softplus(x):
    return jnp.maximum(x, 0.0) + jnp.log1p(jnp.exp(-jnp.abs(x)))


def _rms_rows(x, gain):
    ms = jnp.mean(x * x, axis=-1, keepdims=True)
    return x * lax.rsqrt(ms + EPS) * gain


def _headnorm_pairs(h, gain128):
    rows, width = h.shape
    lo_mask = lax.broadcasted_iota(I32, (rows, LANES), 1) < (LANES // 2)
    outs = []
    for c in range(width // LANES):
        s = h[:, c * LANES:(c + 1) * LANES]
        sq = s * s
        lo = jnp.sum(jnp.where(lo_mask, sq, 0.0), axis=-1, keepdims=True)
        hi = jnp.sum(jnp.where(lo_mask, 0.0, sq), axis=-1, keepdims=True)
        ms = jnp.where(lo_mask, lo, hi) * (2.0 / LANES)
        outs.append(s * lax.rsqrt(ms + EPS) * gain128)
    return outs


def _headnorm_full(h, gain128):
    outs = []
    for c in range(h.shape[1] // LANES):
        outs.append(_rms_rows(h[:, c * LANES:(c + 1) * LANES], gain128))
    return outs


def _const_spec(shape):
    nd = len(shape)
    return pl.BlockSpec(shape, lambda *_: (0,) * nd, pipeline_mode=pl.Buffered(1))


def _params(sem):
    return pltpu.CompilerParams(dimension_semantics=sem, vmem_limit_bytes=VMEM_LIMIT)


_P_AQ = (0, 512)
_P_AK = (512, 768)
_P_AV = (768, 1024)
_P_IQ = (1024, 1280)
_P_MISC = (1280, 1408)
_P_GQKV = (1408, 2944)
_P_GZ = (2944, 3456)
_P_MQ = (3456, 3968)
_P_GATES = (3968, 7040)
_P_WIDTH = 7040


def _pack_w_in(w_in):
    d = w_in.shape[0]
    sizes = (A_HEADS * A_DH, A_KV_HEADS * A_DH, A_KV_HEADS * A_DH, IDX_HEADS * IDX_DH, IDX_DH, IDX_HEADS,
             2 * G_HEADS * G_DK + G_HEADS * G_DV, G_HEADS * G_DV, G_HEADS, G_HEADS, M_HEADS * M_DH)
    offs = np.cumsum((0,) + sizes)
    aq, ak, av, iq, ik, iw, gqkv, gz, gb, ga, mq = (w_in[:, offs[i]:offs[i + 1]] for i in range(len(sizes)))
    gates = w_in[:, offs[-1]:]
    pad = jnp.zeros((d, LANES - (IDX_DH + IDX_HEADS + 2 * G_HEADS)), w_in.dtype)
    misc = jnp.concatenate([ik, iw, gb, ga, pad], axis=1)
    packed = jnp.concatenate([aq, ak, av, iq, misc, gqkv, gz, mq, gates], axis=1)
    assert packed.shape[1] == _P_WIDTH
    return packed.astype(BF16)


def _inproj_kernel(x_ref, gain_ref, w_ref, aqg_ref, akg_ref, mqg_ref, misc_scale_ref,
                   aq_ref, ak_ref, av_ref, iq_ref, misc_ref, gqkv_ref, gz_ref, mq_ref, gates_ref):
    xn = _rms_rows(x_ref[...], gain_ref[...]).astype(BF16)

    def proj(rng):
        return jnp.dot(xn, w_ref[:, rng[0]:rng[1]], preferred_element_type=F32)

    for c, y in enumerate(_headnorm_pairs(proj(_P_AQ), aqg_ref[...])):
        aq_ref[:, c * LANES:(c + 1) * LANES] = y
    for c, y in enumerate(_headnorm_pairs(proj(_P_AK), akg_ref[...])):
        ak_ref[:, c * LANES:(c + 1) * LANES] = y
    av_ref[...] = proj(_P_AV)
    iq_ref[...] = proj(_P_IQ)
    misc_ref[...] = proj(_P_MISC) * misc_scale_ref[...]
    gqkv_ref[...] = proj(_P_GQKV)
    gz_ref[...] = proj(_P_GZ)
    for c, y in enumerate(_headnorm_full(proj(_P_MQ), mqg_ref[...])):
        mq_ref[:, c * LANES:(c + 1) * LANES] = y
    gates_ref[...] = _sigmoid(proj(_P_GATES))


def _inproj(x2d, norm_mix, w_packed, a_q_norm, a_k_norm, m_q_norm, tm):
    n, d = x2d.shape
    widths = [r[1] - r[0] for r in (_P_AQ, _P_AK, _P_AV, _P_IQ, _P_MISC, _P_GQKV, _P_GZ, _P_MQ, _P_GATES)]
    lane = np.arange(LANES)
    misc_scale = np.where((lane >= MISC_IW) & (lane < MISC_GB), IDX_HEADS ** -0.5, 1.0).astype(np.float32)[None]
    return pl.pallas_call(
        _inproj_kernel,
        grid=(n // tm,),
        in_specs=[pl.BlockSpec((tm, d), lambda i: (i, 0)),
                  _const_spec((1, d)), _const_spec((d, _P_WIDTH)),
                  _const_spec((1, LANES)), _const_spec((1, LANES)), _const_spec((1, LANES)),
                  _const_spec((1, LANES))],
        out_specs=[pl.BlockSpec((tm, w), lambda i: (i, 0)) for w in widths],
        out_shape=[jax.ShapeDtypeStruct((n, w), F32) for w in widths],
        compiler_params=_params(("parallel",)),
        name="inproj",
    )(x2d, norm_mix.reshape(1, d), w_packed,
      jnp.tile(a_q_norm.reshape(1, A_DH), (1, 2)), jnp.tile(a_k_norm.reshape(1, A_DH), (1, 2)),
      m_q_norm.reshape(1, M_DH), jnp.asarray(misc_scale))


KEY_NEG_INF = np.int32(-2 ** 31 + 0x7FFFFF)


def _canon_zero(score):
    return jnp.where(score == 0.0, 0.0, score)


def _key_to_float(key):
    key = jnp.maximum(key, KEY_NEG_INF)
    return lax.bitcast_convert_type(key ^ ((key >> 31) & np.int32(0x7FFFFFFF)), F32)


def _radix_select(count_fn, rows, k_top, idx_bits):
    thr = jnp.full((rows, 1), INT_MIN, I32)
    for bit in range(31, -1, -1):
        cand = thr ^ np.int32(-2 ** 31 if bit == 31 else 2 ** bit)
        cand_f = _key_to_float(cand)
        cnt = count_fn(lambda s, c, cand_f=cand_f: s >= cand_f)
        thr = jnp.where(cnt >= k_top, cand, thr)
    thr_f = _key_to_float(thr)
    need = k_top - count_fn(lambda s, c: s > thr_f)
    cut = jnp.zeros((rows, 1), I32)
    for bit in range(idx_bits - 1, -1, -1):
        cand = cut | np.int32(2 ** bit)
        cnt = count_fn(lambda s, c, cand=cand: (s == thr_f) & (c < cand))
        cut = jnp.where(cnt < need, cand, cut)
    return thr_f, cut


def _prompt_attn_kernel(aq_ref, iq_ref, mq_ref, k_ref, v_ref, mk_ref, out_ref,
                        key_ref, m_ref, l_ref, acc_ref, *, tq, k_top, idx_bits):
    j = pl.program_id(1)
    nk = j + 1
    ck = tq
    half = 128
    row_ids = j * tq + lax.broadcasted_iota(I32, (tq, ck), 0)
    lane_ids = lax.broadcasted_iota(I32, (tq, ck), 1)

    iqb = (iq_ref[...] * (IDX_DH ** -0.5)).astype(BF16)
    iw = mq_ref[:, MISC_IW:MISC_IW + IDX_HEADS]

    def idx_body(c, carry):
        start = pl.multiple_of(c * ck, ck)
        kic = mk_ref[pl.ds(start, ck), MISC_IK:MISC_IK + IDX_DH].astype(BF16)
        acc = jnp.zeros((tq, ck), F32)
        for h in range(IDX_HEADS):
            d = _dg(iqb[:, h * IDX_DH:(h + 1) * IDX_DH], kic, NT)
            acc = acc + jnp.maximum(d, 0.0) * iw[:, h:h + 1]
        score = jnp.where(c * ck + lane_ids <= row_ids, acc, -jnp.inf)
        key_ref[c] = _canon_zero(score)
        return carry

    lax.fori_loop(0, nk, idx_body, 0)

    thr_parts, cut_parts = [], []
    for r in range(tq // half):
        col_l = lax.broadcasted_iota(I32, (half, LANES), 1)

        def count_fn(pred, r=r, col_l=col_l):
            def body(c, cnt):
                kc = key_ref[c, r * half:(r + 1) * half, :]
                for i in range(ck // LANES):
                    m = pred(kc[:, i * LANES:(i + 1) * LANES], c * ck + i * LANES + col_l)
                    cnt = cnt + jnp.where(m, 1.0, 0.0)
                return cnt
            cnt = lax.fori_loop(0, nk, body, jnp.zeros((half, LANES), F32))
            return jnp.sum(cnt, axis=-1, keepdims=True)

        t, x = _radix_select(count_fn, half, k_top, idx_bits)
        thr_parts.append(t)
        cut_parts.append(x)
    thr = jnp.concatenate(thr_parts, axis=0)
    cut = jnp.concatenate(cut_parts, axis=0)

    qb = (aq_ref[...] * (A_DH ** -0.5)).astype(BF16)
    m_ref[...] = jnp.full(m_ref.shape, -1e30, F32)
    l_ref[...] = jnp.zeros(l_ref.shape, F32)
    acc_ref[...] = jnp.zeros(acc_ref.shape, F32)
    group = A_HEADS // A_KV_HEADS

    def att_body(c, carry):
        start = pl.multiple_of(c * ck, ck)
        kc = k_ref[pl.ds(start, ck), :].astype(BF16)
        vc = v_ref[pl.ds(start, ck), :].astype(BF16)
        keyc = key_ref[c]
        cols = c * ck + lane_ids
        sel = ((keyc > thr) | ((keyc == thr) & (cols <= cut))) & (cols <= row_ids)
        for h in range(A_HEADS):
            g = h // group
            s = _dg(qb[:, h * A_DH:(h + 1) * A_DH], kc[:, g * A_DH:(g + 1) * A_DH], NT)
            s = jnp.where(sel, s, -jnp.inf)
            m_old = m_ref[h]
            m_new = jnp.maximum(m_old, jnp.max(s, axis=-1, keepdims=True))
            alpha = jnp.exp(m_old - m_new)
            p = jnp.exp(s - m_new)
            l_ref[h] = alpha * l_ref[h] + jnp.sum(p, axis=-1, keepdims=True)
            pv = _dg(p.astype(BF16), vc[:, g * A_DH:(g + 1) * A_DH])
            acc_ref[h] = alpha * acc_ref[h] + pv
            m_ref[h] = m_new
        return carry

    lax.fori_loop(0, nk, att_body, 0)
    for h in range(A_HEADS):
        out_ref[:, h * A_DH:(h + 1) * A_DH] = acc_ref[h] / l_ref[h]


def _prompt_attention(aq, ak, av, iq, misc, tq):
    b, s, _ = aq.shape
    k_top = min(TOPK_MAX, s // 4)
    idx_bits = max(1, int(np.ceil(np.log2(s))))
    kern = functools.partial(_prompt_attn_kernel, tq=tq, k_top=k_top, idx_bits=idx_bits)
    qspec = lambda w: pl.BlockSpec((None, tq, w), lambda bi, j: (bi, j, 0))
    kspec = lambda w: pl.BlockSpec((None, s, w), lambda bi, j: (bi, 0, 0))
    return pl.pallas_call(
        kern,
        grid=(b, s // tq),
        in_specs=[qspec(A_HEADS * A_DH), qspec(IDX_HEADS * IDX_DH), qspec(LANES),
                  kspec(A_KV_HEADS * A_DH), kspec(A_KV_HEADS * A_DH), kspec(LANES)],
        out_specs=qspec(A_HEADS * A_DH),
        out_shape=jax.ShapeDtypeStruct((b, s, A_HEADS * A_DH), F32),
        scratch_shapes=[pltpu.VMEM((s // tq, tq, tq), F32),
                        pltpu.VMEM((A_HEADS, tq, 1), F32), pltpu.VMEM((A_HEADS, tq, 1), F32),
                        pltpu.VMEM((A_HEADS, tq, A_DH), F32)],
        compiler_params=_params(("arbitrary", "arbitrary")),
        name="prompt_attn",
    )(aq, iq, misc, ak, av, misc)


def _sample_attn_kernel(pt_ref, aq_ref, iq_ref, misc_ref, kn_ref, vn_ref, *refs, n_pages, page, k_top, idx_bits):
    ki_refs = refs[:n_pages]
    k_refs = refs[n_pages:2 * n_pages]
    v_refs = refs[2 * n_pages:3 * n_pages]
    out_ref = refs[3 * n_pages]
    t = aq_ref.shape[0]
    past = n_pages * page
    assert page == LANES and t <= LANES
    row_ids = lax.broadcasted_iota(I32, (t, LANES), 0)
    lane_ids = lax.broadcasted_iota(I32, (t, LANES), 1)

    iqb = (iq_ref[...] * (IDX_DH ** -0.5)).astype(BF16)
    iw = misc_ref[:, MISC_IW:MISC_IW + IDX_HEADS]
    pad_rows = jnp.zeros((LANES - t, LANES), F32)

    def idx_scores(kic):
        acc = jnp.zeros((t, LANES), F32)
        for h in range(IDX_HEADS):
            d = _dg(iqb[:, h * IDX_DH:(h + 1) * IDX_DH], kic, NT)
            acc = acc + jnp.maximum(d, 0.0) * iw[:, h:h + 1]
        return acc

    keys = [_canon_zero(idx_scores(ki_refs[p][...].astype(BF16))) for p in range(n_pages)]
    new_rows = jnp.concatenate([misc_ref[...], pad_rows], axis=0)
    new_ok = lane_ids <= row_ids
    keys.append(jnp.where(new_ok, _canon_zero(idx_scores(new_rows[:, MISC_IK:MISC_IK + IDX_DH].astype(BF16))), -jnp.inf))

    def count_fn(pred):
        cnt = jnp.zeros((t, LANES), F32)
        for p, kc in enumerate(keys):
            cnt = cnt + jnp.where(pred(kc, p * LANES + lane_ids), 1.0, 0.0)
        return jnp.sum(cnt, axis=-1, keepdims=True)

    thr, cut = _radix_select(count_fn, t, k_top, idx_bits)

    qb = (aq_ref[...] * (A_DH ** -0.5)).astype(BF16)
    group = A_HEADS // A_KV_HEADS
    pad_kv = jnp.zeros((LANES - t, A_KV_HEADS * A_DH), F32)
    scores = [[] for _ in range(A_HEADS)]
    vals = []
    for p in range(n_pages + 1):
        if p < n_pages:
            kc = k_refs[p][...].astype(BF16)
            vc = v_refs[p][...].astype(BF16)
        else:
            kc = jnp.concatenate([kn_ref[...], pad_kv], axis=0).astype(BF16)
            vc = jnp.concatenate([vn_ref[...], pad_kv], axis=0).astype(BF16)
        vals.append(vc)
        cols = p * LANES + lane_ids
        sel = (keys[p] > thr) | ((keys[p] == thr) & (cols <= cut))
        if p == n_pages:
            sel = sel & new_ok
        for h in range(A_HEADS):
            g = h // group
            s = _dg(qb[:, h * A_DH:(h + 1) * A_DH], kc[:, g * A_DH:(g + 1) * A_DH], NT)
            scores[h].append(jnp.where(sel, s, -jnp.inf))
    for h in range(A_HEADS):
        g = h // group
        mx = scores[h][0]
        for s in scores[h][1:]:
            mx = jnp.maximum(mx, s)
        mx = jnp.max(mx, axis=-1, keepdims=True)
        den = jnp.zeros((t, LANES), F32)
        acc = jnp.zeros((t, A_DH), F32)
        for p, s in enumerate(scores[h]):
            e = jnp.exp(s - mx)
            den = den + e
            acc = acc + _dg(e.astype(BF16), vals[p][:, g * A_DH:(g + 1) * A_DH])
        out_ref[:, h * A_DH:(h + 1) * A_DH] = acc / jnp.sum(den, axis=-1, keepdims=True)


def _sample_attention(aq, ak, av, iq, misc, pool_k, pool_v, pool_ki, page_table):
    db, t, _ = aq.shape
    n_pages = page_table.shape[1]
    page = pool_ki.shape[1]
    total = n_pages * page + t
    k_top = min(TOPK_MAX, total // 4)
    idx_bits = max(1, int(np.ceil(np.log2((n_pages + 1) * LANES))))
    kern = functools.partial(_sample_attn_kernel, n_pages=n_pages, page=page, k_top=k_top, idx_bits=idx_bits)
    qspec = lambda w: pl.BlockSpec((None, t, w), lambda bi, pt: (bi, 0, 0))
    pspec = lambda w, p: pl.BlockSpec((None, page, w), lambda bi, pt: (pt[bi, p], 0, 0))
    kvw = A_KV_HEADS * A_DH
    in_specs = [qspec(A_HEADS * A_DH), qspec(IDX_HEADS * IDX_DH), qspec(LANES), qspec(kvw), qspec(kvw)]
    in_specs += [pspec(IDX_DH, p) for p in range(n_pages)]
    in_specs += [pspec(kvw, p) for p in range(n_pages)]
    in_specs += [pspec(kvw, p) for p in range(n_pages)]
    return pl.pallas_call(
        kern,
        grid_spec=pltpu.PrefetchScalarGridSpec(
            num_scalar_prefetch=1, grid=(db,), in_specs=in_specs, out_specs=qspec(A_HEADS * A_DH)),
        out_shape=jax.ShapeDtypeStruct((db, t, A_HEADS * A_DH), F32),
        compiler_params=_params(("arbitrary",)),
        name="sample_attn",
    )(page_table, aq, iq, misc, ak, av, *([pool_ki] * n_pages), *([pool_k] * n_pages), *([pool_v] * n_pages))


def _gdn_kernel(u_ref, gz_ref, misc_ref, conv0_ref, s0_ref, convw_ref, alog_ref, dtb_ref, onorm_ref,
                out_ref, s_out_ref, conv_out_ref, state_ref, ubuf_ref, *, chunk):
    c = pl.program_id(1)
    n_c = pl.num_programs(1)
    hist = CONV_W - 1
    base = 8

    @pl.when(c == 0)
    def _():
        state_ref[...] = s0_ref[...]
        ubuf_ref[base - hist:base, :] = conv0_ref[...]

    u = u_ref[...]
    ubuf_ref[base:base + chunk, :] = u
    conv = 0.0
    for jt in range(CONV_W):
        conv = conv + ubuf_ref[base - hist + jt:base - hist + jt + chunk, :] * convw_ref[jt:jt + 1, :]
    act = _silu(conv)
    tail = ubuf_ref[base + chunk - hist:base + chunk, :]
    ubuf_ref[base - hist:base, :] = tail

    @pl.when(c == n_c - 1)
    def _():
        conv_out_ref[...] = tail

    misc = misc_ref[...]
    lane = lax.broadcasted_iota(I32, misc.shape, 1)
    beta_all = _sigmoid(misc)
    g_all = -jnp.exp(alog_ref[...]) * _softplus(misc + dtb_ref[...])
    g_all = jnp.where((lane >= MISC_GA) & (lane < MISC_GA + G_HEADS), g_all, 0.0)

    ii = lax.broadcasted_iota(I32, (chunk, chunk), 0)
    jj = lax.broadcasted_iota(I32, (chunk, chunk), 1)
    lower = jnp.where(ii >= jj, 1.0, 0.0).astype(BF16)
    upper = jnp.where(ii <= jj, 1.0, 0.0).astype(BF16)
    g1 = g_all.astype(BF16)
    r1 = g_all - g1.astype(F32)
    g2 = r1.astype(BF16)
    g3 = (r1 - g2.astype(F32)).astype(BF16)
    gc_col = _dg(lower, g1) + _dg(lower, g2) + _dg(lower, g3)
    gc_row = _dg(g1, upper, TN) + _dg(g2, upper, TN) + _dg(g3, upper, TN)

    qk = G_HEADS * G_DK
    n_dbl = int(np.log2(chunk))
    assert 2 ** n_dbl == chunk
    for h in range(G_HEADS):
        q = act[:, h * G_DK:(h + 1) * G_DK]
        k = act[:, qk + h * G_DK:qk + (h + 1) * G_DK]
        v = act[:, 2 * qk + h * G_DV:2 * qk + (h + 1) * G_DV]
        q = q * lax.rsqrt(jnp.sum(q * q, axis=-1, keepdims=True) + EPS) * (G_DK ** -0.5)
        k = k * lax.rsqrt(jnp.sum(k * k, axis=-1, keepdims=True) + EPS)
        beta = beta_all[:, MISC_GB + h:MISC_GB + h + 1]
        gcol = gc_col[:, MISC_GA + h:MISC_GA + h + 1]
        grow = gc_row[MISC_GA + h:MISC_GA + h + 1, :]
        gamma = jnp.exp(jnp.where(ii >= jj, gcol - grow, -jnp.inf))
        kb = k * beta
        neg_m = jnp.where(ii > jj, -(_dot3(kb, k, NT) * gamma), 0.0)
        eg = jnp.exp(gcol)
        sol = jnp.concatenate([v * beta, kb * eg], axis=-1)
        pw = neg_m
        for it in range(n_dbl):
            sol = sol + _dot3(pw, sol)
            if it + 1 < n_dbl:
                pw = _dot3(pw, pw)
        u_s = sol[:, :G_DV]
        w_s = sol[:, G_DV:]
        aqk = _dot3(q, k, NT) * gamma
        dq = q * eg
        g_last = gcol[chunk - 1:chunk, :]
        k_tail = k * jnp.exp(g_last - gcol)
        st = state_ref[h]
        v_new = u_s - _dot3(w_s, st)
        o = _dot3(dq, st) + _dot3(aqk, v_new)
        state_ref[h] = st * jnp.exp(g_last) + _dot3(k_tail, v_new, TN)
        o = _rms_rows(o, onorm_ref[...]) * _silu(gz_ref[:, h * G_DV:(h + 1) * G_DV])
        out_ref[:, h * G_DV:(h + 1) * G_DV] = o

    @pl.when(c == n_c - 1)
    def _():
        s_out_ref[...] = state_ref[...]


def _gdn(gqkv, gz, misc, conv0, s0, g_conv, g_a_log, g_dt_bias, g_o_norm):
    b, t, cch = gqkv.shape
    chunk = min(G_CHUNK, t)
    assert t % chunk == 0 and chunk % 8 == 0
    hist = CONV_W - 1
    alog = jnp.zeros((1, LANES), F32).at[0, MISC_GA:MISC_GA + G_HEADS].set(g_a_log.astype(F32))
    dtb = jnp.zeros((1, LANES), F32).at[0, MISC_GA:MISC_GA + G_HEADS].set(g_dt_bias.astype(F32))
    kern = functools.partial(_gdn_kernel, chunk=chunk)
    tspec = lambda w: pl.BlockSpec((None, chunk, w), lambda bi, c: (bi, c, 0))
    sspec = pl.BlockSpec((None, G_HEADS, G_DK, G_DV), lambda bi, c: (bi, 0, 0, 0))
    cspec = pl.BlockSpec((None, hist, cch), lambda bi, c: (bi, 0, 0))
    return pl.pallas_call(
        kern,
        grid=(b, t // chunk),
        in_specs=[tspec(cch), tspec(G_HEADS * G_DV), tspec(LANES), cspec, sspec,
                  _const_spec((CONV_W, cch)), _const_spec((1, LANES)), _const_spec((1, LANES)),
                  _const_spec((1, G_DV))],
        out_specs=[tspec(G_HEADS * G_DV), sspec, cspec],
        out_shape=[jax.ShapeDtypeStruct((b, t, G_HEADS * G_DV), F32),
                   jax.ShapeDtypeStruct((b, G_HEADS, G_DK, G_DV), F32),
                   jax.ShapeDtypeStruct((b, hist, cch), F32)],
        scratch_shapes=[pltpu.VMEM((G_HEADS, G_DK, G_DV), F32), pltpu.VMEM((8 + chunk, cch), F32)],
        compiler_params=_params(("arbitrary", "arbitrary")),
        name="gdn",
    )(gqkv, gz, misc, conv0, s0, g_conv, alog, dtb, g_o_norm.reshape(1, G_DV))


def _mem_kv_kernel(mem_ref, gain_ref, w_ref, kg_ref, mk_ref, mv_ref):
    xn = _rms_rows(mem_ref[...], gain_ref[...]).astype(BF16)
    half = M_HEADS * M_DH
    kk = jnp.dot(xn, w_ref[:, :half], preferred_element_type=F32)
    for c, y in enumerate(_headnorm_full(kk, kg_ref[...])):
        mk_ref[:, c * LANES:(c + 1) * LANES] = y
    mv_ref[...] = jnp.dot(xn, w_ref[:, half:], preferred_element_type=F32)


def _mem_kv(mem2d, mem_norm, w_mem_kv, m_k_norm, tm):
    n, d = mem2d.shape
    half = M_HEADS * M_DH
    return pl.pallas_call(
        _mem_kv_kernel,
        grid=(n // tm,),
        in_specs=[pl.BlockSpec((tm, d), lambda i: (i, 0)), _const_spec((1, d)), _const_spec((d, 2 * half)),
                  _const_spec((1, M_DH))],
        out_specs=[pl.BlockSpec((tm, half), lambda i: (i, 0))] * 2,
        out_shape=[jax.ShapeDtypeStruct((n, half), F32)] * 2,
        compiler_params=_params(("parallel",)),
        name="mem_kv",
    )(mem2d, mem_norm.reshape(1, d), w_mem_kv.astype(BF16), m_k_norm.reshape(1, M_DH))


def _mem_attn_kernel(q_ref, k_ref, v_ref, out_ref):
    for h in range(M_HEADS):
        sl = slice(h * M_DH, (h + 1) * M_DH)
        s = _dg(q_ref[:, sl].astype(BF16), k_ref[:, sl].astype(BF16), NT) * (M_DH ** -0.5)
        e = jnp.exp(s - jnp.max(s, axis=-1, keepdims=True))
        p = e / jnp.sum(e, axis=-1, keepdims=True)
        out_ref[:, sl] = _dg(p.astype(BF16), v_ref[:, sl].astype(BF16))


def _mem_attention(mq, mk, mv, tm):
    b, t, w = mq.shape
    m = mk.shape[1]
    return pl.pallas_call(
        _mem_attn_kernel,
        grid=(b, t // tm),
        in_specs=[pl.BlockSpec((None, tm, w), lambda bi, i: (bi, i, 0)),
                  pl.BlockSpec((None, m, w), lambda bi, i: (bi, 0, 0)),
                  pl.BlockSpec((None, m, w), lambda bi, i: (bi, 0, 0))],
        out_specs=pl.BlockSpec((None, tm, w), lambda bi, i: (bi, i, 0)),
        out_shape=jax.ShapeDtypeStruct((b, t, w), F32),
        compiler_params=_params(("parallel", "parallel")),
        name="mem_attn",
    )(mq, mk, mv)


def _merge_ffn_kernel(x_ref, a_ref, g_ref, m_ref, gates_ref, wa_ref, wg_ref, wm_ref, wo_ref, nf_ref,
                      win_ref, wout_ref, y_ref, *, d_ff, ff_chunk):
    d = x_ref.shape[1]
    gates = gates_ref[...]
    h = (gates[:, :d] * jnp.dot(a_ref[...].astype(BF16), wa_ref[...], preferred_element_type=F32)
         + gates[:, d:2 * d] * jnp.dot(g_ref[...].astype(BF16), wg_ref[...], preferred_element_type=F32)
         + gates[:, 2 * d:] * jnp.dot(m_ref[...].astype(BF16), wm_ref[...], preferred_element_type=F32))
    x1 = x_ref[...] + jnp.dot(h.astype(BF16), wo_ref[...], preferred_element_type=F32)
    xn = _rms_rows(x1, nf_ref[...]).astype(BF16)
    acc = jnp.zeros_like(x1)
    for c in range(d_ff // ff_chunk):
        lo = c * ff_chunk
        gate = jnp.dot(xn, win_ref[:, lo:lo + ff_chunk], preferred_element_type=F32)
        up = jnp.dot(xn, win_ref[:, d_ff + lo:d_ff + lo + ff_chunk], preferred_element_type=F32)
        acc = acc + jnp.dot((_silu(gate) * up).astype(BF16), wout_ref[lo:lo + ff_chunk, :],
                            preferred_element_type=F32)
    y_ref[...] = x1 + acc


def _merge_ffn(x2d, a_out, g_out, m_out, gates, w_a, w_g, w_m, w_o, norm_ffn, w_in, w_out, tm):
    n, d = x2d.shape
    d_ff = w_out.shape[0]
    ff_chunk = 2 * LANES
    assert d_ff % ff_chunk == 0
    kern = functools.partial(_merge_ffn_kernel, d_ff=d_ff, ff_chunk=ff_chunk)
    row = lambda w: pl.BlockSpec((tm, w), lambda i: (i, 0))
    return pl.pallas_call(
        kern,
        grid=(n // tm,),
        in_specs=[row(d), row(a_out.shape[1]), row(g_out.shape[1]), row(m_out.shape[1]), row(3 * d),
                  _const_spec(w_a.shape), _const_spec(w_g.shape), _const_spec(w_m.shape), _const_spec(w_o.shape),
                  _const_spec((1, d)), _const_spec(w_in.shape), _const_spec(w_out.shape)],
        out_specs=row(d),
        out_shape=jax.ShapeDtypeStruct((n, d), F32),
        compiler_params=_params(("parallel",)),
        name="merge_ffn",
    )(x2d, a_out, g_out, m_out, gates, w_a, w_g, w_m, w_o, norm_ffn.reshape(1, d), w_in, w_out)


def _tile(n, pref):
    t = min(n, pref)
    assert n % t == 0
    return t


def kernel(x_prompt, x_sample, mem_prompt, cache_k, cache_v, cache_idx_k, page_table, state_gdn, state_conv,
           cache_mem_k, cache_mem_v, norm_mix, w_in, a_q_norm, a_k_norm, g_conv, g_a_log, g_dt_bias, g_o_norm,
           mem_norm, w_mem_kv, m_q_norm, m_k_norm, w_a_out, w_g_out, w_m_out, w_o, norm_ffn, w_ffn_in, w_ffn_out):
    depth = w_in.shape[0]
    b, s, d = x_prompt.shape
    db, t, _ = x_sample.shape
    n_mem = mem_prompt.shape[1]
    kvw = A_KV_HEADS * A_DH
    cch = g_conv.shape[2]
    yp, ys = x_prompt, x_sample
    p_states, s_states = [], []
    for l in range(depth):
        w_packed = _pack_w_in(w_in[l])
        bf = lambda w: w.astype(BF16)
        proj_w = (norm_mix[l], w_packed, a_q_norm[l], a_k_norm[l], m_q_norm[l])
        ffn_w = (bf(w_a_out[l]), bf(w_g_out[l]), bf(w_m_out[l]), bf(w_o[l]), norm_ffn[l], bf(w_ffn_in[l]),
                 bf(w_ffn_out[l]))
        gdn_w = (g_conv[l], g_a_log[l], g_dt_bias[l], g_o_norm[l])

        x2 = yp.reshape(b * s, d)
        aq, ak, av, iq, misc, gqkv, gz, mq, gates = _inproj(x2, *proj_w, tm=_tile(b * s, 256))
        r3 = lambda a: a.reshape(b, s, a.shape[-1])
        a_out = _prompt_attention(r3(aq), r3(ak), r3(av), r3(iq), r3(misc), tq=_tile(s, 256))
        g_out, p_gdn, p_conv = _gdn(r3(gqkv), r3(gz), r3(misc),
                                    jnp.zeros((b, CONV_W - 1, cch), F32),
                                    jnp.zeros((b, G_HEADS, G_DK, G_DV), F32), *gdn_w)
        mk, mv = _mem_kv(mem_prompt.reshape(b * n_mem, d), mem_norm[l], w_mem_kv[l], m_k_norm[l],
                         tm=_tile(b * n_mem, 256))
        m_out = _mem_attention(r3(mq), mk.reshape(b, n_mem, -1), mv.reshape(b, n_mem, -1), tm=_tile(s, 512))
        y2 = _merge_ffn(x2, a_out.reshape(b * s, -1), g_out.reshape(b * s, -1), m_out.reshape(b * s, -1), gates,
                        *ffn_w, tm=_tile(b * s, 256))
        yp = y2.reshape(b, s, d)
        p_states.append((ak.reshape(b, s, A_KV_HEADS, A_DH), av.reshape(b, s, A_KV_HEADS, A_DH),
                         misc[:, MISC_IK:MISC_IK + IDX_DH].reshape(b, s, IDX_DH), p_gdn, p_conv,
                         mk.reshape(b, n_mem, M_HEADS, M_DH), mv.reshape(b, n_mem, M_HEADS, M_DH)))

        xs2 = ys.reshape(db * t, d)
        aq, ak, av, iq, misc, gqkv, gz, mq, gates = _inproj(xs2, *proj_w, tm=_tile(db * t, 256))
        r3 = lambda a: a.reshape(db, t, a.shape[-1])
        n_phys, page = cache_idx_k.shape[1], cache_idx_k.shape[2]
        a_out = _sample_attention(r3(aq), r3(ak), r3(av), r3(iq), r3(misc),
                                  cache_k[l].reshape(n_phys, page, kvw), cache_v[l].reshape(n_phys, page, kvw),
                                  cache_idx_k[l], page_table)
        g_out, s_gdn, s_conv = _gdn(r3(gqkv), r3(gz), r3(misc), state_conv[l], state_gdn[l], *gdn_w)
        m_out = _mem_attention(r3(mq), cache_mem_k[l].reshape(db, n_mem, -1), cache_mem_v[l].reshape(db, n_mem, -1),
                               tm=t)
        y2 = _merge_ffn(xs2, a_out.reshape(db * t, -1), g_out.reshape(db * t, -1), m_out.reshape(db * t, -1), gates,
                        *ffn_w, tm=_tile(db * t, 256))
        ys = y2.reshape(db, t, d)
        s_states.append((ak.reshape(db, t, A_KV_HEADS, A_DH), av.reshape(db, t, A_KV_HEADS, A_DH),
                         misc[:, MISC_IK:MISC_IK + IDX_DH].reshape(db, t, IDX_DH), s_gdn, s_conv))

    p_k, p_v, p_idx_k, p_gdn, p_conv, p_mem_k, p_mem_v = [jnp.stack(z) for z in zip(*p_states)]
    s_k, s_v, s_idx_k, s_gdn, s_conv = [jnp.stack(z) for z in zip(*s_states)]
    return (yp, ys, p_k, p_v, p_idx_k, p_gdn, p_conv, p_mem_k, p_mem_v, s_k, s_v, s_idx_k, s_gdn, s_conv)
```

```python
import functools

import numpy as np
import jax
import jax.numpy as jnp
from jax import lax
from jax.experimental import pallas as pl
from jax.experimental.pallas import tpu as pltpu

F32 = jnp.float32
BF16 = jnp.bfloat16
I32 = jnp.int32

A_HEADS = 8
A_KV_HEADS = 4
A_DH = 64
IDX_HEADS = 4
IDX_DH = 64
TOPK_MAX = 256
G_HEADS = 4
G_DK = 128
G_DV = 128
CONV_W = 4
G_CHUNK = 64
M_HEADS = 4
M_DH = 128
EPS = 1e-6

LANES = 128
VMEM_LIMIT = 56 * 1024 * 1024
INT_MIN = np.int32(-2 ** 31)

MISC_IK = 0
MISC_IW = IDX_DH
MISC_GB = MISC_IW + IDX_HEADS
MISC_GA = MISC_GB + G_HEADS

NN = (((1,), (0,)), ((), ()))
NT = (((1,), (1,)), ((), ()))
TN = (((0,), (0,)), ((), ()))


def _dg(a, b, dn=NN):
    return lax.dot_general(a, b, dn, preferred_element_type=F32)


def _split2(x):
    hi = x.astype(BF16)
    lo = (x - hi.astype(F32)).astype(BF16)
    return hi, lo


def _dot3(a, b, dn=NN):
    ah, al = _split2(a)
    bh, bl = _split2(b)
    return _dg(ah, bh, dn) + (_dg(ah, bl, dn) + _dg(al, bh, dn))


def _sigmoid(x):
    return 1.0 / (1.0 + jnp.exp(-x))


def _silu(x):
    return x * _sigmoid(x)


def _softplus(x):
    return jnp.maximum(x, 0.0) + jnp.log1p(jnp.exp(-jnp.abs(x)))


def _rms_rows(x, gain):
    ms = jnp.mean(x * x, axis=-1, keepdims=True)
    return x * lax.rsqrt(ms + EPS) * gain


def _headnorm_pairs(h, gain128):
    rows, width = h.shape
    lo_mask = lax.broadcasted_iota(I32, (rows, LANES), 1) < (LANES // 2)
    outs = []
    for c in range(width // LANES):
        s = h[:, c * LANES:(c + 1) * LANES]
        sq = s * s
        lo = jnp.sum(jnp.where(lo_mask, sq, 0.0), axis=-1, keepdims=True)
        hi = jnp.sum(jnp.where(lo_mask, 0.0, sq), axis=-1, keepdims=True)
        ms = jnp.where(lo_mask, lo, hi) * (2.0 / LANES)
        outs.append(s * lax.rsqrt(ms + EPS) * gain128)
    return outs


def _headnorm_full(h, gain128):
    outs = []
    for c in range(h.shape[1] // LANES):
        outs.append(_rms_rows(h[:, c * LANES:(c + 1) * LANES], gain128))
    return outs


def _const_spec(shape):
    nd = len(shape)
    return pl.BlockSpec(shape, lambda *_: (0,) * nd, pipeline_mode=pl.Buffered(1))


def _params(sem):
    return pltpu.CompilerParams(dimension_semantics=sem, vmem_limit_bytes=VMEM_LIMIT)


_P_AQ = (0, 512)
_P_AK = (512, 768)
_P_AV = (768, 1024)
_P_IQ = (1024, 1280)
_P_MISC = (1280, 1408)
_P_GQKV = (1408, 2944)
_P_GZ = (2944, 3456)
_P_MQ = (3456, 3968)
_P_GATES = (3968, 7040)
_P_WIDTH = 7040


def _pack_w_in(w_in):
    d = w_in.shape[0]
    sizes = (A_HEADS * A_DH, A_KV_HEADS * A_DH, A_KV_HEADS * A_DH, IDX_HEADS * IDX_DH, IDX_DH, IDX_HEADS,
             2 * G_HEADS * G_DK + G_HEADS * G_DV, G_HEADS * G_DV, G_HEADS, G_HEADS, M_HEADS * M_DH)
    offs = np.cumsum((0,) + sizes)
    aq, ak, av, iq, ik, iw, gqkv, gz, gb, ga, mq = (w_in[:, offs[i]:offs[i + 1]] for i in range(len(sizes)))
    gates = w_in[:, offs[-1]:]
    pad = jnp.zeros((d, LANES - (IDX_DH + IDX_HEADS + 2 * G_HEADS)), w_in.dtype)
    misc = jnp.concatenate([ik, iw, gb, ga, pad], axis=1)
    packed = jnp.concatenate([aq, ak, av, iq, misc, gqkv, gz, mq, gates], axis=1)
    assert packed.shape[1] == _P_WIDTH
    return packed.astype(BF16)


_T_AQ = (0, 512)
_T_IQ = (512, 768)
_T_IW = (768, 776)
_T_AV = (776, 1032)
_T_ROWS = 1032


def _pack_w_in_t(w_in):
    sizes = (A_HEADS * A_DH, A_KV_HEADS * A_DH, A_KV_HEADS * A_DH, IDX_HEADS * IDX_DH, IDX_DH, IDX_HEADS)
    offs = np.cumsum((0,) + sizes)
    aq, _, av, iq, _, iw = (w_in[:, offs[i]:offs[i + 1]] for i in range(len(sizes)))
    pad = jnp.zeros((w_in.shape[0], _T_IW[1] - _T_IW[0] - IDX_HEADS), w_in.dtype)
    packed = jnp.concatenate([aq, iq, iw, pad, av], axis=1).T
    assert packed.shape[0] == _T_ROWS
    return packed.astype(BF16)


def _inproj_common(xn, w_ref, akg_ref, mqg_ref, misc_scale_ref, ak_ref, misc_ref, gqkv_ref, gz_ref, mq_ref, gates_ref):
    def proj(rng):
        return jnp.dot(xn, w_ref[:, rng[0]:rng[1]], preferred_element_type=F32)

    for c, y in enumerate(_headnorm_pairs(proj(_P_AK), akg_ref[...])):
        ak_ref[:, c * LANES:(c + 1) * LANES] = y
    misc_ref[...] = proj(_P_MISC) * misc_scale_ref[...]
    gqkv_ref[...] = proj(_P_GQKV)
    gz_ref[...] = proj(_P_GZ)
    for c, y in enumerate(_headnorm_full(proj(_P_MQ), mqg_ref[...])):
        mq_ref[:, c * LANES:(c + 1) * LANES] = y
    gates_ref[...] = _sigmoid(proj(_P_GATES))
    return proj


def _inproj_kernel(x_ref, gain_ref, w_ref, aqg_ref, akg_ref, mqg_ref, misc_scale_ref,
                   aq_ref, ak_ref, av_ref, iq_ref, misc_ref, gqkv_ref, gz_ref, mq_ref, gates_ref):
    xn = _rms_rows(x_ref[...], gain_ref[...]).astype(BF16)
    proj = _inproj_common(xn, w_ref, akg_ref, mqg_ref, misc_scale_ref, ak_ref, misc_ref, gqkv_ref, gz_ref, mq_ref,
                          gates_ref)
    for c, y in enumerate(_headnorm_pairs(proj(_P_AQ), aqg_ref[...])):
        aq_ref[:, c * LANES:(c + 1) * LANES] = y
    av_ref[...] = proj(_P_AV)
    iq_ref[...] = proj(_P_IQ)


def _inproj_t_kernel(x_ref, gain_ref, w_ref, wt_ref, aqg_ref, akg_ref, mqg_ref, misc_scale_ref,
                     aqt_ref, iqt_ref, iwt_ref, avt_ref, ak_ref, misc_ref, gqkv_ref, gz_ref, mq_ref, gates_ref):
    xn = _rms_rows(x_ref[...], gain_ref[...]).astype(BF16)
    _inproj_common(xn, w_ref, akg_ref, mqg_ref, misc_scale_ref, ak_ref, misc_ref, gqkv_ref, gz_ref, mq_ref, gates_ref)

    def proj_t(rng):
        return _dg(wt_ref[rng[0]:rng[1], :], xn, NT)

    aqt = proj_t(_T_AQ)
    gain_col = aqg_ref[...]
    for h in range(A_HEADS):
        qh = aqt[h * A_DH:(h + 1) * A_DH, :]
        ms = jnp.mean(qh * qh, axis=0, keepdims=True)
        aqt_ref[h * A_DH:(h + 1) * A_DH, :] = qh * lax.rsqrt(ms + EPS) * gain_col
    iqt_ref[...] = proj_t(_T_IQ)
    iwt_ref[...] = proj_t(_T_IW) * (IDX_HEADS ** -0.5)
    avt_ref[...] = proj_t(_T_AV)


def _inproj(x2d, norm_mix, w_packed, w_packed_t, a_q_norm, a_k_norm, m_q_norm, tm, seq=None):
    n, d = x2d.shape
    transposed = seq is not None
    lane = np.arange(LANES)
    misc_scale = np.where((lane >= MISC_IW) & (lane < MISC_GB), IDX_HEADS ** -0.5, 1.0).astype(np.float32)[None]
    akg = jnp.tile(a_k_norm.reshape(1, A_DH), (1, 2))
    mqg = m_q_norm.reshape(1, M_DH)
    common = [r[1] - r[0] for r in (_P_MISC, _P_GQKV, _P_GZ, _P_MQ, _P_GATES)]
    row_spec = lambda w: pl.BlockSpec((tm, w), lambda i: (i, 0))
    row_shape = lambda w: jax.ShapeDtypeStruct((n, w), F32)
    x_specs = [row_spec(d), _const_spec((1, d)), _const_spec((d, _P_WIDTH))]
    g_specs = [_const_spec((1, LANES)), _const_spec((1, LANES)), _const_spec((1, LANES))]
    if not transposed:
        widths = [512, 256, 256, 256] + common
        return pl.pallas_call(
            _inproj_kernel,
            grid=(n // tm,),
            in_specs=x_specs + [_const_spec((1, LANES))] + g_specs,
            out_specs=[row_spec(w) for w in widths],
            out_shape=[row_shape(w) for w in widths],
            compiler_params=_params(("parallel",)),
            name="inproj",
        )(x2d, norm_mix.reshape(1, d), w_packed, jnp.tile(a_q_norm.reshape(1, A_DH), (1, 2)), akg, mqg,
          jnp.asarray(misc_scale))
    t_rows = [r[1] - r[0] for r in (_T_AQ, _T_IQ, _T_IW, _T_AV)]
    widths = [256] + common
    tiles = seq // tm
    col_spec = lambda r: pl.BlockSpec((None, r, tm), lambda i: (i // tiles, 0, i % tiles))
    return pl.pallas_call(
        _inproj_t_kernel,
        grid=(n // tm,),
        in_specs=x_specs + [_const_spec((_T_ROWS, d)), _const_spec((A_DH, 1))] + g_specs,
        out_specs=[col_spec(r) for r in t_rows] + [row_spec(w) for w in widths],
        out_shape=[jax.ShapeDtypeStruct((n // seq, r, seq), F32) for r in t_rows] + [row_shape(w) for w in widths],
        compiler_params=_params(("parallel",)),
        name="inproj_t",
    )(x2d, norm_mix.reshape(1, d), w_packed, w_packed_t, a_q_norm.reshape(A_DH, 1), akg, mqg, jnp.asarray(misc_scale))


KEY_NEG_INF = np.int32(-2 ** 31 + 0x7FFFFF)


def _canon_zero(score):
    return jnp.where(score == 0.0, 0.0, score)


def _key_to_float(key):
    key = jnp.maximum(key, KEY_NEG_INF)
    return lax.bitcast_convert_type(key ^ ((key >> 31) & np.int32(0x7FFFFFFF)), F32)


def _as_i32(v):
    return np.int32(v - (1 << 32) if v >= (1 << 31) else v)


def _radix_select(count_fn, shape, bcast, k_top, idx_bits, bits_per_pass):
    thr_f, need, n_eq = _radix_threshold(count_fn, shape, bcast, k_top, bits_per_pass)
    del n_eq
    return thr_f, _tie_cut(count_fn, shape, bcast, thr_f, need, idx_bits, bits_per_pass)


def _radix_threshold(count_fn, shape, bcast, k_top, bits_per_pass):
    thr = jnp.full(shape, INT_MIN, I32)
    for hi in range(32, 0, -bits_per_pass):
        lo = max(hi - bits_per_pass, 0)
        cands = [bcast(_key_to_float(thr ^ _as_i32(v << lo))) for v in range(1, 2 ** (hi - lo))]
        cnts = count_fn([lambda s, c, cf=cf: s >= cf for cf in cands])
        digit = sum(jnp.where(cnt >= k_top, 1, 0) for cnt in cnts)
        thr = thr ^ (digit << lo)
    thr_f = _key_to_float(thr)
    thr_b = bcast(thr_f)
    n_gt, n_eq = count_fn([lambda s, c: s > thr_b, lambda s, c: s == thr_b])
    return thr_f, k_top - n_gt, n_eq


def _tie_cut(count_fn, shape, bcast, thr_f, need, idx_bits, bits_per_pass):
    thr_b = bcast(thr_f)
    cut = jnp.zeros(shape, I32)
    for hi in range(idx_bits, 0, -bits_per_pass):
        lo = max(hi - bits_per_pass, 0)
        cands = [bcast(cut | np.int32(v << lo)) for v in range(1, 2 ** (hi - lo))]
        cnts = count_fn([lambda s, c, cb=cb: (s == thr_b) & (c < cb) for cb in cands])
        digit = sum(jnp.where(cnt < need, 1, 0) for cnt in cnts)
        cut = cut | (digit << lo)
    return cut


def _prompt_attn_kernel(aqt_ref, iqt_ref, iwt_ref, k_ref, vt_ref, misc_ref, out_ref,
                        score_ref, kh_ref, ki_ref, vth_ref, acc_ref, cut_ref, *, tq, k_top, idx_bits):
    j = pl.program_id(1)
    nk = j + 1
    ck = tq
    n_chunks = score_ref.shape[0]
    sub = 8
    kt = ck
    group = A_HEADS // A_KV_HEADS

    @pl.when(j == 0)
    def _():
        for c in range(n_chunks):
            rows = slice(c * ck, (c + 1) * ck)
            for g in range(A_KV_HEADS):
                kh_ref[g, rows, :] = k_ref[rows, g * A_DH:(g + 1) * A_DH].astype(BF16)
            ki_ref[rows, :] = misc_ref[rows, MISC_IK:MISC_IK + IDX_DH].astype(BF16)
            vth_ref[c] = vt_ref[:, rows].astype(BF16)

    key_in_chunk = lax.broadcasted_iota(I32, (ck, tq), 0)
    q_pos = j * tq + lax.broadcasted_iota(I32, (ck, tq), 1)
    key_in_tile = lax.broadcasted_iota(I32, (kt, tq), 0)
    q_pos_tile = j * tq + lax.broadcasted_iota(I32, (kt, tq), 1)

    iqb = (iqt_ref[...] * (IDX_DH ** -0.5)).astype(BF16)
    iw = iwt_ref[...]

    def idx_body(c, carry):
        start = pl.multiple_of(c * ck, ck)
        for part in range(ck // kt):
            rows = slice(part * kt, (part + 1) * kt)
            kic = ki_ref[pl.ds(pl.multiple_of(start + part * kt, kt), kt), :]
            dots = [_dg(kic, iqb[h * IDX_DH:(h + 1) * IDX_DH, :]) for h in range(IDX_HEADS)]
            acc = jnp.zeros((kt, tq), F32)
            for h in range(IDX_HEADS):
                acc = acc + jnp.maximum(dots[h], 0.0) * iw[h:h + 1, :]
            score = jnp.where(c * ck + part * kt + key_in_tile <= q_pos_tile, acc, -jnp.inf)
            score_ref[c, rows, :] = _canon_zero(score)
        return carry

    lax.fori_loop(0, nk, idx_body, 0)

    key_in_group = lax.broadcasted_iota(I32, (sub, tq), 0)
    n_acc = 4

    def count_fn(preds):
        def body(c, cnts):
            cnts = [list(a) for a in cnts]
            for i in range(ck // sub):
                sc = score_ref[c, i * sub:(i + 1) * sub, :]
                key_id = c * ck + i * sub + key_in_group
                for n, pred in enumerate(preds):
                    cnts[n][i % n_acc] = cnts[n][i % n_acc] + jnp.where(pred(sc, key_id), 1.0, 0.0)
            return tuple(tuple(a) for a in cnts)
        zero = tuple(jnp.zeros((sub, tq), F32) for _ in range(n_acc))
        cnts = lax.fori_loop(0, nk, body, tuple(zero for _ in preds))
        return [jnp.sum(sum(a), axis=0, keepdims=True) for a in cnts]

    bcast = lambda v: jnp.broadcast_to(v, (sub, tq))
    thr, need, n_eq = _radix_threshold(count_fn, (1, tq), bcast, k_top, 1)
    cut_ref[...] = jnp.full((1, tq), 2 ** idx_bits - 1, I32)

    @pl.when(jnp.max(n_eq - need) > 0.5)
    def _():
        cut_ref[...] = _tie_cut(count_fn, (1, tq), bcast, thr, need, idx_bits, 1)

    cut = cut_ref[...]

    def mask_body(c, carry):
        sc = score_ref[c]
        key_id = c * ck + key_in_chunk
        sel = ((sc > thr) | ((sc == thr) & (key_id <= cut))) & (key_id <= q_pos)
        score_ref[c] = jnp.where(sel, 0.0, -jnp.inf)
        return carry

    lax.fori_loop(0, nk, mask_body, 0)

    qb = (aqt_ref[...] * (A_DH ** -0.5)).astype(BF16)
    acc_ref[...] = jnp.zeros(acc_ref.shape, F32)

    def att_body(c, carry):
        ms, ls = list(carry[0]), list(carry[1])
        head_rows = [slice(h * A_DH, (h + 1) * A_DH) for h in range(A_HEADS)]
        for part in range(ck // kt):
            start = pl.multiple_of(c * ck + part * kt, kt)
            lanes = slice(part * kt, (part + 1) * kt)
            bias = score_ref[c, lanes, :]
            scores = [_dg(kh_ref[h // group, pl.ds(start, kt), :], qb[head_rows[h], :]) for h in range(A_HEADS)]
            alphas, pvs = [], []
            for h in range(A_HEADS):
                s = scores[h] + bias
                m_new = jnp.maximum(ms[h], jnp.max(s, axis=0, keepdims=True))
                alpha = jnp.exp(ms[h] - m_new)
                p = jnp.exp(s - m_new)
                ls[h] = alpha * ls[h] + jnp.sum(p, axis=0, keepdims=True)
                ms[h] = m_new
                alphas.append(alpha)
                g = h // group
                pvs.append(_dg(vth_ref[c, g * A_DH:(g + 1) * A_DH, lanes], p.astype(BF16)))
            for h in range(A_HEADS):
                acc_ref[head_rows[h], :] = alphas[h] * acc_ref[head_rows[h], :] + pvs[h]
        return tuple(ms), tuple(ls)

    init = (tuple(jnp.full((1, tq), -1e30, F32) for _ in range(A_HEADS)),
            tuple(jnp.zeros((1, tq), F32) for _ in range(A_HEADS)))
    _, ls = lax.fori_loop(0, nk, att_body, init)
    for h in range(A_HEADS):
        rows = slice(h * A_DH, (h + 1) * A_DH)
        acc_ref[rows, :] = acc_ref[rows, :] / ls[h]
    out_ref[...] = acc_ref[...].T


def _prompt_attention(aqt, iqt, iwt, ak, avt, misc, tq):
    b, s, kvw = ak.shape
    nq = s // tq
    k_top = min(TOPK_MAX, s // 4)
    idx_bits = max(1, int(np.ceil(np.log2(s))))
    kern = functools.partial(_prompt_attn_kernel, tq=tq, k_top=k_top, idx_bits=idx_bits)
    qspec = lambda r: pl.BlockSpec((None, r, tq), lambda bi, j: (bi, 0, j))
    kspec = lambda w: pl.BlockSpec((None, s, w), lambda bi, j: (bi, 0, 0))
    return pl.pallas_call(
        kern,
        grid=(b, nq),
        in_specs=[qspec(A_HEADS * A_DH), qspec(IDX_HEADS * IDX_DH), qspec(_T_IW[1] - _T_IW[0]),
                  kspec(kvw), pl.BlockSpec((None, kvw, s), lambda bi, j: (bi, 0, 0)), kspec(LANES)],
        out_specs=pl.BlockSpec((None, tq, A_HEADS * A_DH), lambda bi, j: (bi, j, 0)),
        out_shape=jax.ShapeDtypeStruct((b, s, A_HEADS * A_DH), F32),
        scratch_shapes=[pltpu.VMEM((nq, tq, tq), F32),
                        pltpu.VMEM((A_KV_HEADS, s, A_DH), BF16), pltpu.VMEM((s, IDX_DH), BF16),
                        pltpu.VMEM((nq, kvw, tq), BF16),
                        pltpu.VMEM((A_HEADS * A_DH, tq), F32), pltpu.VMEM((1, tq), I32)],
        compiler_params=_params(("arbitrary", "arbitrary")),
        name="prompt_attn",
    )(aqt, iqt, iwt, ak, avt, misc)


def _sample_attn_kernel(pt_ref, aq_ref, iq_ref, misc_ref, kn_ref, vn_ref, *refs, n_pages, page, k_top, idx_bits):
    ki_refs = refs[:n_pages]
    k_refs = refs[n_pages:2 * n_pages]
    v_refs = refs[2 * n_pages:3 * n_pages]
    out_ref = refs[3 * n_pages]
    t = aq_ref.shape[0]
    assert page == LANES and t <= LANES
    row_ids = lax.broadcasted_iota(I32, (t, LANES), 0)
    lane_ids = lax.broadcasted_iota(I32, (t, LANES), 1)
    group = A_HEADS // A_KV_HEADS

    iqb = (iq_ref[...] * (IDX_DH ** -0.5)).astype(BF16)
    iw = misc_ref[:, MISC_IW:MISC_IW + IDX_HEADS]

    def idx_scores(kic, dn):
        acc = jnp.zeros((t, LANES), F32)
        for h in range(IDX_HEADS):
            d = _dg(iqb[:, h * IDX_DH:(h + 1) * IDX_DH], kic, dn)
            acc = acc + jnp.maximum(d, 0.0) * iw[:, h:h + 1]
        return acc

    keys = [_canon_zero(idx_scores(ki_refs[p][...].astype(BF16), NN)) for p in range(n_pages)]
    new_rows = jnp.concatenate([misc_ref[...], jnp.zeros((LANES - t, LANES), F32)], axis=0)
    new_ok = lane_ids <= row_ids
    new_scores = idx_scores(new_rows[:, MISC_IK:MISC_IK + IDX_DH].astype(BF16), NT)
    keys.append(jnp.where(new_ok, _canon_zero(new_scores), -jnp.inf))

    def count_fn(preds):
        cnts = []
        for pred in preds:
            cnt = jnp.zeros((t, LANES), F32)
            for p, kc in enumerate(keys):
                cnt = cnt + jnp.where(pred(kc, p * LANES + lane_ids), 1.0, 0.0)
            cnts.append(jnp.sum(cnt, axis=-1, keepdims=True))
        return cnts

    thr, cut = _radix_select(count_fn, (t, 1), lambda v: jnp.broadcast_to(v, (t, LANES)), k_top, idx_bits, 4)

    qb = (aq_ref[...] * (A_DH ** -0.5)).astype(BF16)
    q_stack = [jnp.concatenate([qb[:, (g * group + r) * A_DH:(g * group + r + 1) * A_DH] for r in range(group)], axis=0)
               for g in range(A_KV_HEADS)]
    pad_kv = jnp.zeros((LANES - t, A_KV_HEADS * A_DH), F32)
    k_new = jnp.concatenate([kn_ref[...], pad_kv], axis=0).astype(BF16)
    v_new = jnp.concatenate([vn_ref[...], pad_kv], axis=0).astype(BF16)
    biases = []
    for p in range(n_pages + 1):
        cols = p * LANES + lane_ids
        sel = (keys[p] > thr) | ((keys[p] == thr) & (cols <= cut))
        if p == n_pages:
            sel = sel & new_ok
        bias = jnp.where(sel, 0.0, -jnp.inf)
        biases.append(jnp.concatenate([bias] * group, axis=0))
    chs = [slice(g * A_DH, (g + 1) * A_DH) for g in range(A_KV_HEADS)]
    scores = []
    for g in range(A_KV_HEADS):
        sg = [_dg(q_stack[g], k_refs[p][chs[g], :].astype(BF16)) for p in range(n_pages)]
        sg.append(_dg(q_stack[g], k_new[:, chs[g]], NT))
        scores.append(sg)
    probs, dens = [], []
    for g in range(A_KV_HEADS):
        sg = [s + biases[p] for p, s in enumerate(scores[g])]
        mx = sg[0]
        for s in sg[1:]:
            mx = jnp.maximum(mx, s)
        mx = jnp.max(mx, axis=-1, keepdims=True)
        es = [jnp.exp(s - mx) for s in sg]
        dens.append(jnp.sum(sum(es), axis=-1, keepdims=True))
        probs.append([e.astype(BF16) for e in es])
    for g in range(A_KV_HEADS):
        pvs = [_dg(probs[g][p], v_refs[p][chs[g], :].astype(BF16), NT) for p in range(n_pages)]
        pvs.append(_dg(probs[g][n_pages], v_new[:, chs[g]]))
        o = sum(pvs) / dens[g]
        for r in range(group):
            h = g * group + r
            out_ref[:, h * A_DH:(h + 1) * A_DH] = o[r * t:(r + 1) * t, :]


def _sample_attention(aq, ak, av, iq, misc, pool_kt, pool_vt, pool_kit, page_table):
    db, t, _ = aq.shape
    n_pages = page_table.shape[1]
    page = pool_kit.shape[2]
    total = n_pages * page + t
    k_top = min(TOPK_MAX, total // 4)
    idx_bits = max(1, int(np.ceil(np.log2((n_pages + 1) * LANES))))
    kern = functools.partial(_sample_attn_kernel, n_pages=n_pages, page=page, k_top=k_top, idx_bits=idx_bits)
    qspec = lambda w: pl.BlockSpec((None, t, w), lambda bi, pt: (bi, 0, 0))
    pspec = lambda w, p: pl.BlockSpec((None, w, page), lambda bi, pt: (pt[bi, p], 0, 0))
    kvw = A_KV_HEADS * A_DH
    in_specs = [qspec(A_HEADS * A_DH), qspec(IDX_HEADS * IDX_DH), qspec(LANES), qspec(kvw), qspec(kvw)]
    in_specs += [pspec(IDX_DH, p) for p in range(n_pages)]
    in_specs += [pspec(kvw, p) for p in range(n_pages)]
    in_specs += [pspec(kvw, p) for p in range(n_pages)]
    return pl.pallas_call(
        kern,
        grid_spec=pltpu.PrefetchScalarGridSpec(
            num_scalar_prefetch=1, grid=(db,), in_specs=in_specs, out_specs=qspec(A_HEADS * A_DH)),
        out_shape=jax.ShapeDtypeStruct((db, t, A_HEADS * A_DH), F32),
        compiler_params=_params(("arbitrary",)),
        name="sample_attn",
    )(page_table, aq, iq, misc, ak, av, *([pool_kit] * n_pages), *([pool_kt] * n_pages), *([pool_vt] * n_pages))


def _gdn_kernel(u_ref, gz_ref, misc_ref, conv0_ref, s0_ref, convw_ref, alog_ref, dtb_ref, onorm_ref,
                out_ref, s_out_ref, conv_out_ref, state_ref, ubuf_ref, *, chunk, n_seq):
    c = pl.program_id(1)
    n_c = pl.num_programs(1)
    hist = CONV_W - 1
    base = 8

    @pl.when(c == 0)
    def _():
        state_ref[...] = s0_ref[...]
        ubuf_ref[:, base - hist:base, :] = conv0_ref[...]

    ii = lax.broadcasted_iota(I32, (chunk, chunk), 0)
    jj = lax.broadcasted_iota(I32, (chunk, chunk), 1)
    lower = jnp.where(ii >= jj, 1.0, 0.0).astype(BF16)
    upper = jnp.where(ii <= jj, 1.0, 0.0).astype(BF16)
    lane = lax.broadcasted_iota(I32, (chunk, LANES), 1)
    ga_lanes = (lane >= MISC_GA) & (lane < MISC_GA + G_HEADS)
    qk = G_HEADS * G_DK
    n_dbl = int(np.log2(chunk))
    assert 2 ** n_dbl == chunk

    per_seq = []
    for b in range(n_seq):
        ubuf_ref[b, base:base + chunk, :] = u_ref[b]
        conv = 0.0
        for jt in range(CONV_W):
            conv = conv + ubuf_ref[b, base - hist + jt:base - hist + jt + chunk, :] * convw_ref[jt:jt + 1, :]
        act = _silu(conv)
        tail = ubuf_ref[b, base + chunk - hist:base + chunk, :]
        ubuf_ref[b, base - hist:base, :] = tail
        conv_out_ref[b] = tail

        misc = misc_ref[b]
        beta_all = _sigmoid(misc)
        g_all = jnp.where(ga_lanes, -jnp.exp(alog_ref[...]) * _softplus(misc + dtb_ref[...]), 0.0)
        g1 = g_all.astype(BF16)
        r1 = g_all - g1.astype(F32)
        g2 = r1.astype(BF16)
        g3 = (r1 - g2.astype(F32)).astype(BF16)
        gc_col = _dg(lower, g1) + _dg(lower, g2) + _dg(lower, g3)
        gc_row = _dg(g1, upper, TN) + _dg(g2, upper, TN) + _dg(g3, upper, TN)
        per_seq.append((act, beta_all, gc_col, gc_row))

    chains = [(b, h) for b in range(n_seq) for h in range(G_HEADS)]
    qs, ks, kbs, gammas, egs, gcols, sols = [], [], [], [], [], [], []
    for b, h in chains:
        act, beta_all, gc_col, gc_row = per_seq[b]
        q = act[:, h * G_DK:(h + 1) * G_DK]
        k = act[:, qk + h * G_DK:qk + (h + 1) * G_DK]
        v = act[:, 2 * qk + h * G_DV:2 * qk + (h + 1) * G_DV]
        q = q * lax.rsqrt(jnp.sum(q * q, axis=-1, keepdims=True) + EPS) * (G_DK ** -0.5)
        k = k * lax.rsqrt(jnp.sum(k * k, axis=-1, keepdims=True) + EPS)
        beta = beta_all[:, MISC_GB + h:MISC_GB + h + 1]
        gcol = gc_col[:, MISC_GA + h:MISC_GA + h + 1]
        grow = gc_row[MISC_GA + h:MISC_GA + h + 1, :]
        eg = jnp.exp(gcol)
        kb = k * beta
        qs.append(q)
        ks.append(k)
        kbs.append(kb)
        gammas.append(jnp.exp(jnp.where(ii >= jj, gcol - grow, -jnp.inf)))
        egs.append(eg)
        gcols.append(gcol)
        sols.append(jnp.concatenate([v * beta, kb * eg], axis=-1))
    n_ch = len(chains)
    kk = [_dot3(kbs[i], ks[i], NT) for i in range(n_ch)]
    pws = [jnp.where(ii > jj, -(kk[i] * gammas[i]), 0.0) for i in range(n_ch)]
    for it in range(n_dbl):
        upd = [_dot3(pws[i], sols[i]) for i in range(n_ch)]
        if it + 1 < n_dbl:
            pws = [_dot3(pws[i], pws[i]) for i in range(n_ch)]
        sols = [sols[i] + upd[i] for i in range(n_ch)]
    aqk = [_dot3(qs[i], ks[i], NT) for i in range(n_ch)]
    sts = [state_ref[b, h] for b, h in chains]
    ws = [_dot3(sols[i][:, G_DV:], sts[i]) for i in range(n_ch)]
    o1 = [_dot3(qs[i] * egs[i], sts[i]) for i in range(n_ch)]
    v_new = [sols[i][:, :G_DV] - ws[i] for i in range(n_ch)]
    o2 = [_dot3(aqk[i] * gammas[i], v_new[i]) for i in range(n_ch)]
    g_last = [gcols[i][chunk - 1:chunk, :] for i in range(n_ch)]
    kv = [_dot3(ks[i] * jnp.exp(g_last[i] - gcols[i]), v_new[i], TN) for i in range(n_ch)]
    for i, (b, h) in enumerate(chains):
        state_ref[b, h] = sts[i] * jnp.exp(g_last[i]) + kv[i]
        o = _rms_rows(o1[i] + o2[i], onorm_ref[...]) * _silu(gz_ref[b, :, h * G_DV:(h + 1) * G_DV])
        out_ref[b, :, h * G_DV:(h + 1) * G_DV] = o

    @pl.when(c == n_c - 1)
    def _():
        s_out_ref[...] = state_ref[...]


def _gdn(gqkv, gz, misc, conv0, s0, g_conv, g_a_log, g_dt_bias, g_o_norm, n_seq):
    b, t, cch = gqkv.shape
    chunk = min(G_CHUNK, t)
    assert t % chunk == 0 and chunk % 8 == 0 and b % n_seq == 0
    hist = CONV_W - 1
    alog = jnp.zeros((1, LANES), F32).at[0, MISC_GA:MISC_GA + G_HEADS].set(g_a_log.astype(F32))
    dtb = jnp.zeros((1, LANES), F32).at[0, MISC_GA:MISC_GA + G_HEADS].set(g_dt_bias.astype(F32))
    kern = functools.partial(_gdn_kernel, chunk=chunk, n_seq=n_seq)
    tspec = lambda w: pl.BlockSpec((n_seq, chunk, w), lambda bi, c: (bi, c, 0))
    sspec = pl.BlockSpec((n_seq, G_HEADS, G_DK, G_DV), lambda bi, c: (bi, 0, 0, 0))
    cspec = pl.BlockSpec((n_seq, hist, cch), lambda bi, c: (bi, 0, 0))
    return pl.pallas_call(
        kern,
        grid=(b // n_seq, t // chunk),
        in_specs=[tspec(cch), tspec(G_HEADS * G_DV), tspec(LANES), cspec, sspec,
                  _const_spec((CONV_W, cch)), _const_spec((1, LANES)), _const_spec((1, LANES)),
                  _const_spec((1, G_DV))],
        out_specs=[tspec(G_HEADS * G_DV), sspec, cspec],
        out_shape=[jax.ShapeDtypeStruct((b, t, G_HEADS * G_DV), F32),
                   jax.ShapeDtypeStruct((b, G_HEADS, G_DK, G_DV), F32),
                   jax.ShapeDtypeStruct((b, hist, cch), F32)],
        scratch_shapes=[pltpu.VMEM((n_seq, G_HEADS, G_DK, G_DV), F32), pltpu.VMEM((n_seq, 8 + chunk, cch), F32)],
        compiler_params=_params(("arbitrary", "arbitrary")),
        name="gdn",
    )(gqkv, gz, misc, conv0, s0, g_conv, alog, dtb, g_o_norm.reshape(1, G_DV))


def _mem_kv_kernel(mem_ref, gain_ref, w_ref, kg_ref, mk_ref, mv_ref):
    xn = _rms_rows(mem_ref[...], gain_ref[...]).astype(BF16)
    half = M_HEADS * M_DH
    kk = jnp.dot(xn, w_ref[:, :half], preferred_element_type=F32)
    for c, y in enumerate(_headnorm_full(kk, kg_ref[...])):
        mk_ref[:, c * LANES:(c + 1) * LANES] = y
    mv_ref[...] = jnp.dot(xn, w_ref[:, half:], preferred_element_type=F32)


def _mem_kv(mem2d, mem_norm, w_mem_kv, m_k_norm, tm):
    n, d = mem2d.shape
    half = M_HEADS * M_DH
    return pl.pallas_call(
        _mem_kv_kernel,
        grid=(n // tm,),
        in_specs=[pl.BlockSpec((tm, d), lambda i: (i, 0)), _const_spec((1, d)), _const_spec((d, 2 * half)),
                  _const_spec((1, M_DH))],
        out_specs=[pl.BlockSpec((tm, half), lambda i: (i, 0))] * 2,
        out_shape=[jax.ShapeDtypeStruct((n, half), F32)] * 2,
        compiler_params=_params(("parallel",)),
        name="mem_kv",
    )(mem2d, mem_norm.reshape(1, d), w_mem_kv.astype(BF16), m_k_norm.reshape(1, M_DH))


def _mem_attn_kernel(q_ref, k_ref, v_ref, out_ref):
    for h in range(M_HEADS):
        sl = slice(h * M_DH, (h + 1) * M_DH)
        s = _dg(q_ref[:, sl].astype(BF16), k_ref[:, sl].astype(BF16), NT) * (M_DH ** -0.5)
        e = jnp.exp(s - jnp.max(s, axis=-1, keepdims=True))
        p = e / jnp.sum(e, axis=-1, keepdims=True)
        out_ref[:, sl] = _dg(p.astype(BF16), v_ref[:, sl].astype(BF16))


def _mem_attention(mq, mk, mv, tm):
    b, t, w = mq.shape
    m = mk.shape[1]
    return pl.pallas_call(
        _mem_attn_kernel,
        grid=(b, t // tm),
        in_specs=[pl.BlockSpec((None, tm, w), lambda bi, i: (bi, i, 0)),
                  pl.BlockSpec((None, m, w), lambda bi, i: (bi, 0, 0)),
                  pl.BlockSpec((None, m, w), lambda bi, i: (bi, 0, 0))],
        out_specs=pl.BlockSpec((None, tm, w), lambda bi, i: (bi, i, 0)),
        out_shape=jax.ShapeDtypeStruct((b, t, w), F32),
        compiler_params=_params(("parallel", "parallel")),
        name="mem_attn",
    )(mq, mk, mv)


def _merge_ffn_kernel(x_ref, a_ref, g_ref, m_ref, gates_ref, wa_ref, wg_ref, wm_ref, wo_ref, nf_ref,
                      win_ref, wout_ref, y_ref, *, d_ff, ff_chunk):
    d = x_ref.shape[1]
    gates = gates_ref[...]
    h = (gates[:, :d] * jnp.dot(a_ref[...].astype(BF16), wa_ref[...], preferred_element_type=F32)
         + gates[:, d:2 * d] * jnp.dot(g_ref[...].astype(BF16), wg_ref[...], preferred_element_type=F32)
         + gates[:, 2 * d:] * jnp.dot(m_ref[...].astype(BF16), wm_ref[...], preferred_element_type=F32))
    x1 = x_ref[...] + jnp.dot(h.astype(BF16), wo_ref[...], preferred_element_type=F32)
    xn = _rms_rows(x1, nf_ref[...]).astype(BF16)
    acc = jnp.zeros_like(x1)
    for c in range(d_ff // ff_chunk):
        lo = c * ff_chunk
        gate = jnp.dot(xn, win_ref[:, lo:lo + ff_chunk], preferred_element_type=F32)
        up = jnp.dot(xn, win_ref[:, d_ff + lo:d_ff + lo + ff_chunk], preferred_element_type=F32)
        acc = acc + jnp.dot((_silu(gate) * up).astype(BF16), wout_ref[lo:lo + ff_chunk, :],
                            preferred_element_type=F32)
    y_ref[...] = x1 + acc


def _merge_ffn(x2d, a_out, g_out, m_out, gates, w_a, w_g, w_m, w_o, norm_ffn, w_in, w_out, tm):
    n, d = x2d.shape
    d_ff = w_out.shape[0]
    ff_chunk = 2 * LANES
    assert d_ff % ff_chunk == 0
    kern = functools.partial(_merge_ffn_kernel, d_ff=d_ff, ff_chunk=ff_chunk)
    row = lambda w: pl.BlockSpec((tm, w), lambda i: (i, 0))
    return pl.pallas_call(
        kern,
        grid=(n // tm,),
        in_specs=[row(d), row(a_out.shape[1]), row(g_out.shape[1]), row(m_out.shape[1]), row(3 * d),
                  _const_spec(w_a.shape), _const_spec(w_g.shape), _const_spec(w_m.shape), _const_spec(w_o.shape),
                  _const_spec((1, d)), _const_spec(w_in.shape), _const_spec(w_out.shape)],
        out_specs=row(d),
        out_shape=jax.ShapeDtypeStruct((n, d), F32),
        compiler_params=_params(("parallel",)),
        name="merge_ffn",
    )(x2d, a_out, g_out, m_out, gates, w_a, w_g, w_m, w_o, norm_ffn.reshape(1, d), w_in, w_out)


def _tile(n, pref):
    t = min(n, pref)
    assert n % t == 0
    return t


def kernel(x_prompt, x_sample, mem_prompt, cache_k, cache_v, cache_idx_k, page_table, state_gdn, state_conv,
           cache_mem_k, cache_mem_v, norm_mix, w_in, a_q_norm, a_k_norm, g_conv, g_a_log, g_dt_bias, g_o_norm,
           mem_norm, w_mem_kv, m_q_norm, m_k_norm, w_a_out, w_g_out, w_m_out, w_o, norm_ffn, w_ffn_in, w_ffn_out):
    depth = w_in.shape[0]
    b, s, d = x_prompt.shape
    db, t, _ = x_sample.shape
    n_mem = mem_prompt.shape[1]
    kvw = A_KV_HEADS * A_DH
    cch = g_conv.shape[2]
    yp, ys = x_prompt, x_sample
    p_states, s_states = [], []
    for l in range(depth):
        bf = lambda w: w.astype(BF16)
        proj_w = (norm_mix[l], _pack_w_in(w_in[l]), _pack_w_in_t(w_in[l]), a_q_norm[l], a_k_norm[l], m_q_norm[l])
        ffn_w = (bf(w_a_out[l]), bf(w_g_out[l]), bf(w_m_out[l]), bf(w_o[l]), norm_ffn[l], bf(w_ffn_in[l]),
                 bf(w_ffn_out[l]))
        gdn_w = (g_conv[l], g_a_log[l], g_dt_bias[l], g_o_norm[l])

        x2 = yp.reshape(b * s, d)
        tq = _tile(s, 256)
        aqt, iqt, iwt, avt, ak, misc, gqkv, gz, mq, gates = _inproj(x2, *proj_w, tm=tq, seq=s)
        r3 = lambda a: a.reshape(b, s, a.shape[-1])
        a_out = _prompt_attention(aqt, iqt, iwt, r3(ak), avt, r3(misc), tq=tq)
        av = jnp.transpose(avt.reshape(b, A_KV_HEADS, A_DH, s), (0, 3, 1, 2))
        g_out, p_gdn, p_conv = _gdn(r3(gqkv), r3(gz), r3(misc),
                                    jnp.zeros((b, CONV_W - 1, cch), F32),
                                    jnp.zeros((b, G_HEADS, G_DK, G_DV), F32), *gdn_w, n_seq=b)
        mk, mv = _mem_kv(mem_prompt.reshape(b * n_mem, d), mem_norm[l], w_mem_kv[l], m_k_norm[l],
                         tm=_tile(b * n_mem, 256))
        m_out = _mem_attention(r3(mq), mk.reshape(b, n_mem, -1), mv.reshape(b, n_mem, -1), tm=_tile(s, 512))
        y2 = _merge_ffn(x2, a_out.reshape(b * s, -1), g_out.reshape(b * s, -1), m_out.reshape(b * s, -1), gates,
                        *ffn_w, tm=_tile(b * s, 256))
        yp = y2.reshape(b, s, d)
        p_states.append((ak.reshape(b, s, A_KV_HEADS, A_DH), av,
                         misc[:, MISC_IK:MISC_IK + IDX_DH].reshape(b, s, IDX_DH), p_gdn, p_conv,
                         mk.reshape(b, n_mem, M_HEADS, M_DH), mv.reshape(b, n_mem, M_HEADS, M_DH)))

        xs2 = ys.reshape(db * t, d)
        aq, ak, av, iq, misc, gqkv, gz, mq, gates = _inproj(xs2, *proj_w, tm=_tile(db * t, 256))
        r3 = lambda a: a.reshape(db, t, a.shape[-1])
        n_phys, page = cache_idx_k.shape[1], cache_idx_k.shape[2]
        pool_t = lambda c: jnp.transpose(c, (0, 2, 3, 1)).reshape(n_phys, kvw, page)
        a_out = _sample_attention(r3(aq), r3(ak), r3(av), r3(iq), r3(misc), pool_t(cache_k[l]), pool_t(cache_v[l]),
                                  jnp.transpose(cache_idx_k[l], (0, 2, 1)), page_table)
        g_out, s_gdn, s_conv = _gdn(r3(gqkv), r3(gz), r3(misc), state_conv[l], state_gdn[l], *gdn_w,
                                    n_seq=_tile(db, 8))
        m_out = _mem_attention(r3(mq), cache_mem_k[l].reshape(db, n_mem, -1), cache_mem_v[l].reshape(db, n_mem, -1),
                               tm=t)
        y2 = _merge_ffn(xs2, a_out.reshape(db * t, -1), g_out.reshape(db * t, -1), m_out.reshape(db * t, -1), gates,
                        *ffn_w, tm=_tile(db * t, 256))
        ys = y2.reshape(db, t, d)
        s_states.append((ak.reshape(db, t, A_KV_HEADS, A_DH), av.reshape(db, t, A_KV_HEADS, A_DH),
                         misc[:, MISC_IK:MISC_IK + IDX_DH].reshape(db, t, IDX_DH), s_gdn, s_conv))

    p_k, p_v, p_idx_k, p_gdn, p_conv, p_mem_k, p_mem_v = [jnp.stack(z) for z in zip(*p_states)]
    s_k, s_v, s_idx_k, s_gdn, s_conv = [jnp.stack(z) for z in zip(*s_states)]
    return (yp, ys, p_k, p_v, p_idx_k, p_gdn, p_conv, p_mem_k, p_mem_v, s_k, s_v, s_idx_k, s_gdn, s_conv)
```

```python
import functools

import numpy as np
import jax
import jax.numpy as jnp
from jax import lax
from jax.experimental import pallas as pl
from jax.experimental.pallas import tpu as pltpu

F32 = jnp.float32
BF16 = jnp.bfloat16
I32 = jnp.int32

A_HEADS = 8
A_KV_HEADS = 4
A_DH = 64
IDX_HEADS = 4
IDX_DH = 64
TOPK_MAX = 256
G_HEADS = 4
G_DK = 128
G_DV = 128
CONV_W = 4
G_CHUNK = 64
M_HEADS = 4
M_DH = 128
EPS = 1e-6

LANES = 128
VMEM_LIMIT = 56 * 1024 * 1024
INT_MIN = np.int32(-2 ** 31)

MISC_IK = 0
MISC_IW = IDX_DH
MISC_GB = MISC_IW + IDX_HEADS
MISC_GA = MISC_GB + G_HEADS

NN = (((1,), (0,)), ((), ()))
NT = (((1,), (1,)), ((), ()))
TN = (((0,), (0,)), ((), ()))


def _dg(a, b, dn=NN):
    return lax.dot_general(a, b, dn, preferred_element_type=F32)


def _split2(x):
    hi = x.astype(BF16)
    lo = (x - hi.astype(F32)).astype(BF16)
    return hi, lo


def _dot3(a, b, dn=NN):
    ah, al = _split2(a)
    bh, bl = _split2(b)
    return _dg(ah, bh, dn) + (_dg(ah, bl, dn) + _dg(al, bh, dn))


def _sigmoid(x):
    return 1.0 / (1.0 + jnp.exp(-x))


def _silu(x):
    return x * _sigmoid(x)


def _softplus(x):
    return jnp.maximum(x, 0.0) + jnp.log1p(jnp.exp(-jnp.abs(x)))


def _rms_rows(x, gain):
    ms = jnp.mean(x * x, axis=-1, keepdims=True)
    return x * lax.rsqrt(ms + EPS) * gain


def _headnorm_pairs(h, gain128):
    rows, width = h.shape
    lo_mask = lax.broadcasted_iota(I32, (rows, LANES), 1) < (LANES // 2)
    outs = []
    for c in range(width // LANES):
        s = h[:, c * LANES:(c + 1) * LANES]
        sq = s * s
        lo = jnp.sum(jnp.where(lo_mask, sq, 0.0), axis=-1, keepdims=True)
        hi = jnp.sum(jnp.where(lo_mask, 0.0, sq), axis=-1, keepdims=True)
        ms = jnp.where(lo_mask, lo, hi) * (2.0 / LANES)
        outs.append(s * lax.rsqrt(ms + EPS) * gain128)
    return outs


def _headnorm_full(h, gain128):
    outs = []
    for c in range(h.shape[1] // LANES):
        outs.append(_rms_rows(h[:, c * LANES:(c + 1) * LANES], gain128))
    return outs


def _const_spec(shape):
    nd = len(shape)
    return pl.BlockSpec(shape, lambda *_: (0,) * nd, pipeline_mode=pl.Buffered(1))


def _params(sem):
    return pltpu.CompilerParams(dimension_semantics=sem, vmem_limit_bytes=VMEM_LIMIT)


_P_AQ = (0, 512)
_P_AK = (512, 768)
_P_AV = (768, 1024)
_P_IQ = (1024, 1280)
_P_MISC = (1280, 1408)
_P_GQKV = (1408, 2944)
_P_GZ = (2944, 3456)
_P_MQ = (3456, 3968)
_P_GATES = (3968, 7040)
_P_WIDTH = 7040


def _pack_w_in(w_in):
    d = w_in.shape[0]
    sizes = (A_HEADS * A_DH, A_KV_HEADS * A_DH, A_KV_HEADS * A_DH, IDX_HEADS * IDX_DH, IDX_DH, IDX_HEADS,
             2 * G_HEADS * G_DK + G_HEADS * G_DV, G_HEADS * G_DV, G_HEADS, G_HEADS, M_HEADS * M_DH)
    offs = np.cumsum((0,) + sizes)
    aq, ak, av, iq, ik, iw, gqkv, gz, gb, ga, mq = (w_in[:, offs[i]:offs[i + 1]] for i in range(len(sizes)))
    gates = w_in[:, offs[-1]:]
    pad = jnp.zeros((d, LANES - (IDX_DH + IDX_HEADS + 2 * G_HEADS)), w_in.dtype)
    misc = jnp.concatenate([ik, iw, gb, ga, pad], axis=1)
    packed = jnp.concatenate([aq, ak, av, iq, misc, gqkv, gz, mq, gates], axis=1)
    assert packed.shape[1] == _P_WIDTH
    return packed.astype(BF16)


_T_AQ = (0, 512)
_T_IQ = (512, 768)
_T_IW = (768, 776)
_T_AV = (776, 1032)
_T_ROWS = 1032


def _pack_w_in_t(w_in):
    sizes = (A_HEADS * A_DH, A_KV_HEADS * A_DH, A_KV_HEADS * A_DH, IDX_HEADS * IDX_DH, IDX_DH, IDX_HEADS)
    offs = np.cumsum((0,) + sizes)
    aq, _, av, iq, _, iw = (w_in[:, offs[i]:offs[i + 1]] for i in range(len(sizes)))
    pad = jnp.zeros((w_in.shape[0], _T_IW[1] - _T_IW[0] - IDX_HEADS), w_in.dtype)
    packed = jnp.concatenate([aq, iq, iw, pad, av], axis=1).T
    assert packed.shape[0] == _T_ROWS
    return packed.astype(BF16)


def _inproj_common(xn, w_ref, akg_ref, mqg_ref, misc_scale_ref, ak_ref, misc_ref, gqkv_ref, gz_ref, mq_ref, gates_ref):
    def proj(rng):
        return jnp.dot(xn, w_ref[:, rng[0]:rng[1]], preferred_element_type=F32)

    for c, y in enumerate(_headnorm_pairs(proj(_P_AK), akg_ref[...])):
        ak_ref[:, c * LANES:(c + 1) * LANES] = y
    misc_ref[...] = proj(_P_MISC) * misc_scale_ref[...]
    gqkv_ref[...] = proj(_P_GQKV)
    gz_ref[...] = proj(_P_GZ)
    for c, y in enumerate(_headnorm_full(proj(_P_MQ), mqg_ref[...])):
        mq_ref[:, c * LANES:(c + 1) * LANES] = y
    gates_ref[...] = _sigmoid(proj(_P_GATES))
    return proj


def _inproj_kernel(x_ref, gain_ref, w_ref, aqg_ref, akg_ref, mqg_ref, misc_scale_ref,
                   aq_ref, ak_ref, av_ref, iq_ref, misc_ref, gqkv_ref, gz_ref, mq_ref, gates_ref):
    xn = _rms_rows(x_ref[...], gain_ref[...]).astype(BF16)
    proj = _inproj_common(xn, w_ref, akg_ref, mqg_ref, misc_scale_ref, ak_ref, misc_ref, gqkv_ref, gz_ref, mq_ref,
                          gates_ref)
    for c, y in enumerate(_headnorm_pairs(proj(_P_AQ), aqg_ref[...])):
        aq_ref[:, c * LANES:(c + 1) * LANES] = y
    av_ref[...] = proj(_P_AV)
    iq_ref[...] = proj(_P_IQ)


def _inproj_t_kernel(x_ref, gain_ref, w_ref, wt_ref, aqg_ref, akg_ref, mqg_ref, misc_scale_ref,
                     aqt_ref, iqt_ref, iwt_ref, avt_ref, ak_ref, misc_ref, gqkv_ref, gz_ref, mq_ref, gates_ref):
    xn = _rms_rows(x_ref[...], gain_ref[...]).astype(BF16)
    _inproj_common(xn, w_ref, akg_ref, mqg_ref, misc_scale_ref, ak_ref, misc_ref, gqkv_ref, gz_ref, mq_ref, gates_ref)

    def proj_t(rng):
        return _dg(wt_ref[rng[0]:rng[1], :], xn, NT)

    aqt = proj_t(_T_AQ)
    gain_col = aqg_ref[...]
    for h in range(A_HEADS):
        qh = aqt[h * A_DH:(h + 1) * A_DH, :]
        ms = jnp.mean(qh * qh, axis=0, keepdims=True)
        aqt_ref[h * A_DH:(h + 1) * A_DH, :] = qh * lax.rsqrt(ms + EPS) * gain_col
    iqt_ref[...] = proj_t(_T_IQ)
    iwt_ref[...] = proj_t(_T_IW) * (IDX_HEADS ** -0.5)
    avt_ref[...] = proj_t(_T_AV)


def _inproj(x2d, norm_mix, w_packed, w_packed_t, a_q_norm, a_k_norm, m_q_norm, tm, seq=None):
    n, d = x2d.shape
    transposed = seq is not None
    lane = np.arange(LANES)
    misc_scale = np.where((lane >= MISC_IW) & (lane < MISC_GB), IDX_HEADS ** -0.5, 1.0).astype(np.float32)[None]
    akg = jnp.tile(a_k_norm.reshape(1, A_DH), (1, 2))
    mqg = m_q_norm.reshape(1, M_DH)
    common = [r[1] - r[0] for r in (_P_MISC, _P_GQKV, _P_GZ, _P_MQ, _P_GATES)]
    row_spec = lambda w: pl.BlockSpec((tm, w), lambda i: (i, 0))
    row_shape = lambda w: jax.ShapeDtypeStruct((n, w), F32)
    x_specs = [row_spec(d), _const_spec((1, d)), _const_spec((d, _P_WIDTH))]
    g_specs = [_const_spec((1, LANES)), _const_spec((1, LANES)), _const_spec((1, LANES))]
    if not transposed:
        widths = [512, 256, 256, 256] + common
        return pl.pallas_call(
            _inproj_kernel,
            grid=(n // tm,),
            in_specs=x_specs + [_const_spec((1, LANES))] + g_specs,
            out_specs=[row_spec(w) for w in widths],
            out_shape=[row_shape(w) for w in widths],
            compiler_params=_params(("parallel",)),
            name="inproj",
        )(x2d, norm_mix.reshape(1, d), w_packed, jnp.tile(a_q_norm.reshape(1, A_DH), (1, 2)), akg, mqg,
          jnp.asarray(misc_scale))
    t_rows = [r[1] - r[0] for r in (_T_AQ, _T_IQ, _T_IW, _T_AV)]
    widths = [256] + common
    tiles = seq // tm
    col_spec = lambda r: pl.BlockSpec((None, r, tm), lambda i: (i // tiles, 0, i % tiles))
    return pl.pallas_call(
        _inproj_t_kernel,
        grid=(n // tm,),
        in_specs=x_specs + [_const_spec((_T_ROWS, d)), _const_spec((A_DH, 1))] + g_specs,
        out_specs=[col_spec(r) for r in t_rows] + [row_spec(w) for w in widths],
        out_shape=[jax.ShapeDtypeStruct((n // seq, r, seq), F32) for r in t_rows] + [row_shape(w) for w in widths],
        compiler_params=_params(("parallel",)),
        name="inproj_t",
    )(x2d, norm_mix.reshape(1, d), w_packed, w_packed_t, a_q_norm.reshape(A_DH, 1), akg, mqg, jnp.asarray(misc_scale))


KEY_NEG_INF = np.int32(-2 ** 31 + 0x7FFFFF)


def _canon_zero(score):
    return jnp.where(score == 0.0, 0.0, score)


def _key_to_float(key):
    key = jnp.maximum(key, KEY_NEG_INF)
    return lax.bitcast_convert_type(key ^ ((key >> 31) & np.int32(0x7FFFFFFF)), F32)


def _as_i32(v):
    return np.int32(v - (1 << 32) if v >= (1 << 31) else v)


def _radix_select(count_fn, shape, bcast, k_top, idx_bits, bits_per_pass):
    thr_f, need, n_eq = _radix_threshold(count_fn, shape, bcast, k_top, bits_per_pass)
    del n_eq
    return thr_f, _tie_cut(count_fn, shape, bcast, thr_f, need, idx_bits, bits_per_pass)


def _radix_threshold(count_fn, shape, bcast, k_top, bits_per_pass):
    thr = jnp.full(shape, INT_MIN, I32)
    for hi in range(32, 0, -bits_per_pass):
        lo = max(hi - bits_per_pass, 0)
        cands = [bcast(_key_to_float(thr ^ _as_i32(v << lo))) for v in range(1, 2 ** (hi - lo))]
        cnts = count_fn([lambda s, c, cf=cf: s >= cf for cf in cands])
        digit = sum(jnp.where(cnt >= k_top, 1, 0) for cnt in cnts)
        thr = thr ^ (digit << lo)
    thr_f = _key_to_float(thr)
    thr_b = bcast(thr_f)
    n_gt, n_eq = count_fn([lambda s, c: s > thr_b, lambda s, c: s == thr_b])
    return thr_f, k_top - n_gt, n_eq


def _tie_cut(count_fn, shape, bcast, thr_f, need, idx_bits, bits_per_pass):
    thr_b = bcast(thr_f)
    cut = jnp.zeros(shape, I32)
    for hi in range(idx_bits, 0, -bits_per_pass):
        lo = max(hi - bits_per_pass, 0)
        cands = [bcast(cut | np.int32(v << lo)) for v in range(1, 2 ** (hi - lo))]
        cnts = count_fn([lambda s, c, cb=cb: (s == thr_b) & (c < cb) for cb in cands])
        digit = sum(jnp.where(cnt < need, 1, 0) for cnt in cnts)
        cut = cut | (digit << lo)
    return cut


def _prompt_attn_kernel(aqt_ref, iqt_ref, iwt_ref, k_ref, vt_ref, misc_ref, out_ref,
                        score_ref, kb_ref, ki_ref, vth_ref, qblk_ref, acc_ref, cut_ref, *, tq, k_top, idx_bits):
    j = pl.program_id(1)
    nk = j + 1
    ck = tq
    n_chunks = score_ref.shape[0]
    sub = 8
    group = A_HEADS // A_KV_HEADS
    head_rows = [slice(h * A_DH, (h + 1) * A_DH) for h in range(A_HEADS)]
    head_cols = [slice(h * tq, (h + 1) * tq) for h in range(A_HEADS)]
    kv_rows = [slice((h // group) * A_DH, (h // group + 1) * A_DH) for h in range(A_HEADS)]

    @pl.when(j == 0)
    def _():
        for c in range(n_chunks):
            rows = slice(c * ck, (c + 1) * ck)
            kb_ref[rows, :] = k_ref[rows, :].astype(BF16)
            ki_ref[rows, :] = misc_ref[rows, MISC_IK:MISC_IK + IDX_DH].astype(BF16)
            vth_ref[c] = vt_ref[:, rows].astype(BF16)
        qblk_ref[...] = jnp.zeros(qblk_ref.shape, BF16)

    key_in_chunk = lax.broadcasted_iota(I32, (ck, tq), 0)
    q_pos = j * tq + lax.broadcasted_iota(I32, (ck, tq), 1)

    iqb = (iqt_ref[...] * (IDX_DH ** -0.5)).astype(BF16)
    iq_cat = jnp.concatenate([iqb[h * IDX_DH:(h + 1) * IDX_DH, :] for h in range(IDX_HEADS)], axis=1)
    iw = iwt_ref[...]

    def idx_body(c, carry):
        start = pl.multiple_of(c * ck, ck)
        dots = _dg(ki_ref[pl.ds(start, ck), :], iq_cat)
        acc = jnp.zeros((ck, tq), F32)
        for h in range(IDX_HEADS):
            acc = acc + jnp.maximum(dots[:, h * tq:(h + 1) * tq], 0.0) * iw[h:h + 1, :]
        score = jnp.where(c * ck + key_in_chunk <= q_pos, acc, -jnp.inf)
        score_ref[c] = _canon_zero(score)
        return carry

    lax.fori_loop(0, nk, idx_body, 0)

    key_in_group = lax.broadcasted_iota(I32, (sub, tq), 0)
    n_acc = 4

    def count_fn(preds):
        def body(c, cnts):
            cnts = [list(a) for a in cnts]
            for i in range(ck // sub):
                sc = score_ref[c, i * sub:(i + 1) * sub, :]
                key_id = c * ck + i * sub + key_in_group
                for n, pred in enumerate(preds):
                    cnts[n][i % n_acc] = cnts[n][i % n_acc] + jnp.where(pred(sc, key_id), 1.0, 0.0)
            return tuple(tuple(a) for a in cnts)
        zero = tuple(jnp.zeros((sub, tq), F32) for _ in range(n_acc))
        cnts = lax.fori_loop(0, nk, body, tuple(zero for _ in preds))
        return [jnp.sum(sum(a), axis=0, keepdims=True) for a in cnts]

    bcast = lambda v: jnp.broadcast_to(v, (sub, tq))
    thr, need, n_eq = _radix_threshold(count_fn, (1, tq), bcast, k_top, 1)
    cut_ref[...] = jnp.full((1, tq), 2 ** idx_bits - 1, I32)

    @pl.when(jnp.max(n_eq - need) > 0.5)
    def _():
        cut_ref[...] = _tie_cut(count_fn, (1, tq), bcast, thr, need, idx_bits, 1)

    cut = cut_ref[...]

    def mask_body(c, carry):
        sc = score_ref[c]
        key_id = c * ck + key_in_chunk
        sel = ((sc > thr) | ((sc == thr) & (key_id <= cut))) & (key_id <= q_pos)
        score_ref[c] = jnp.where(sel, 0.0, -jnp.inf)
        return carry

    lax.fori_loop(0, nk, mask_body, 0)

    qb = (aqt_ref[...] * (A_DH ** -0.5)).astype(BF16)
    for h in range(A_HEADS):
        qblk_ref[kv_rows[h], head_cols[h]] = qb[head_rows[h], :]
    acc_ref[...] = jnp.zeros(acc_ref.shape, F32)

    def chunk_scores(c):
        start = pl.multiple_of(c * ck, ck)
        return _dg(kb_ref[pl.ds(start, ck), :], qblk_ref[...])

    def fold(x, acc, op):
        for i in range(ck // sub):
            acc = op(acc, x[i * sub:(i + 1) * sub, :])
        return acc

    def max_body(c, mparts):
        s_all = chunk_scores(c)
        bias = score_ref[c]
        return tuple(fold(s_all[:, head_cols[h]] + bias, mparts[h], jnp.maximum) for h in range(A_HEADS))

    mparts = lax.fori_loop(0, nk, max_body, tuple(jnp.full((sub, tq), -1e30, F32) for _ in range(A_HEADS)))
    ms = [jnp.max(mp, axis=0, keepdims=True) for mp in mparts]

    def pv_body(c, lparts):
        s_all = chunk_scores(c)
        bias = score_ref[c]
        probs, new_l = [], []
        for h in range(A_HEADS):
            p = jnp.exp(s_all[:, head_cols[h]] + bias - ms[h])
            new_l.append(fold(p, lparts[h], jnp.add))
            probs.append(p.astype(BF16))
        pv = _dg(vth_ref[c], jnp.concatenate(probs, axis=1))
        for h in range(A_HEADS):
            acc_ref[head_rows[h], :] = acc_ref[head_rows[h], :] + pv[kv_rows[h], head_cols[h]]
        return tuple(new_l)

    lparts = lax.fori_loop(0, nk, pv_body, tuple(jnp.zeros((sub, tq), F32) for _ in range(A_HEADS)))
    for h in range(A_HEADS):
        acc_ref[head_rows[h], :] = acc_ref[head_rows[h], :] / jnp.sum(lparts[h], axis=0, keepdims=True)
    out_ref[...] = acc_ref[...].T


def _prompt_attention(aqt, iqt, iwt, ak, avt, misc, tq):
    b, s, kvw = ak.shape
    nq = s // tq
    k_top = min(TOPK_MAX, s // 4)
    idx_bits = max(1, int(np.ceil(np.log2(s))))
    kern = functools.partial(_prompt_attn_kernel, tq=tq, k_top=k_top, idx_bits=idx_bits)
    qspec = lambda r: pl.BlockSpec((None, r, tq), lambda bi, j: (bi, 0, j))
    kspec = lambda w: pl.BlockSpec((None, s, w), lambda bi, j: (bi, 0, 0))
    return pl.pallas_call(
        kern,
        grid=(b, nq),
        in_specs=[qspec(A_HEADS * A_DH), qspec(IDX_HEADS * IDX_DH), qspec(_T_IW[1] - _T_IW[0]),
                  kspec(kvw), pl.BlockSpec((None, kvw, s), lambda bi, j: (bi, 0, 0)), kspec(LANES)],
        out_specs=pl.BlockSpec((None, tq, A_HEADS * A_DH), lambda bi, j: (bi, j, 0)),
        out_shape=jax.ShapeDtypeStruct((b, s, A_HEADS * A_DH), F32),
        scratch_shapes=[pltpu.VMEM((nq, tq, tq), F32),
                        pltpu.VMEM((s, kvw), BF16), pltpu.VMEM((s, IDX_DH), BF16),
                        pltpu.VMEM((nq, kvw, tq), BF16), pltpu.VMEM((kvw, A_HEADS * tq), BF16),
                        pltpu.VMEM((A_HEADS * A_DH, tq), F32), pltpu.VMEM((1, tq), I32)],
        compiler_params=_params(("arbitrary", "arbitrary")),
        name="prompt_attn",
    )(aqt, iqt, iwt, ak, avt, misc)


def _sample_attn_kernel(pt_ref, aq_ref, iq_ref, misc_ref, kn_ref, vn_ref, *refs, n_pages, page, k_top, idx_bits):
    ki_refs = refs[:n_pages]
    k_refs = refs[n_pages:2 * n_pages]
    v_refs = refs[2 * n_pages:3 * n_pages]
    out_ref = refs[3 * n_pages]
    t = aq_ref.shape[0]
    assert page == LANES and t <= LANES
    row_ids = lax.broadcasted_iota(I32, (t, LANES), 0)
    lane_ids = lax.broadcasted_iota(I32, (t, LANES), 1)
    group = A_HEADS // A_KV_HEADS

    iqb = (iq_ref[...] * (IDX_DH ** -0.5)).astype(BF16)
    iw = misc_ref[:, MISC_IW:MISC_IW + IDX_HEADS]

    def idx_scores(kic, dn):
        acc = jnp.zeros((t, LANES), F32)
        for h in range(IDX_HEADS):
            d = _dg(iqb[:, h * IDX_DH:(h + 1) * IDX_DH], kic, dn)
            acc = acc + jnp.maximum(d, 0.0) * iw[:, h:h + 1]
        return acc

    keys = [_canon_zero(idx_scores(ki_refs[p][...].astype(BF16), NN)) for p in range(n_pages)]
    new_rows = jnp.concatenate([misc_ref[...], jnp.zeros((LANES - t, LANES), F32)], axis=0)
    new_ok = lane_ids <= row_ids
    new_scores = idx_scores(new_rows[:, MISC_IK:MISC_IK + IDX_DH].astype(BF16), NT)
    keys.append(jnp.where(new_ok, _canon_zero(new_scores), -jnp.inf))

    def count_fn(preds):
        cnts = []
        for pred in preds:
            cnt = jnp.zeros((t, LANES), F32)
            for p, kc in enumerate(keys):
                cnt = cnt + jnp.where(pred(kc, p * LANES + lane_ids), 1.0, 0.0)
            cnts.append(jnp.sum(cnt, axis=-1, keepdims=True))
        return cnts

    thr, cut = _radix_select(count_fn, (t, 1), lambda v: jnp.broadcast_to(v, (t, LANES)), k_top, idx_bits, 4)

    qb = (aq_ref[...] * (A_DH ** -0.5)).astype(BF16)
    q_stack = [jnp.concatenate([qb[:, (g * group + r) * A_DH:(g * group + r + 1) * A_DH] for r in range(group)], axis=0)
               for g in range(A_KV_HEADS)]
    pad_kv = jnp.zeros((LANES - t, A_KV_HEADS * A_DH), F32)
    k_new = jnp.concatenate([kn_ref[...], pad_kv], axis=0).astype(BF16)
    v_new = jnp.concatenate([vn_ref[...], pad_kv], axis=0).astype(BF16)
    biases = []
    for p in range(n_pages + 1):
        cols = p * LANES + lane_ids
        sel = (keys[p] > thr) | ((keys[p] == thr) & (cols <= cut))
        if p == n_pages:
            sel = sel & new_ok
        bias = jnp.where(sel, 0.0, -jnp.inf)
        biases.append(jnp.concatenate([bias] * group, axis=0))
    chs = [slice(g * A_DH, (g + 1) * A_DH) for g in range(A_KV_HEADS)]
    scores = []
    for g in range(A_KV_HEADS):
        sg = [_dg(q_stack[g], k_refs[p][chs[g], :].astype(BF16)) for p in range(n_pages)]
        sg.append(_dg(q_stack[g], k_new[:, chs[g]], NT))
        scores.append(sg)
    probs, dens = [], []
    for g in range(A_KV_HEADS):
        sg = [s + biases[p] for p, s in enumerate(scores[g])]
        mx = sg[0]
        for s in sg[1:]:
            mx = jnp.maximum(mx, s)
        mx = jnp.max(mx, axis=-1, keepdims=True)
        es = [jnp.exp(s - mx) for s in sg]
        dens.append(jnp.sum(sum(es), axis=-1, keepdims=True))
        probs.append([e.astype(BF16) for e in es])
    for g in range(A_KV_HEADS):
        pvs = [_dg(probs[g][p], v_refs[p][chs[g], :].astype(BF16), NT) for p in range(n_pages)]
        pvs.append(_dg(probs[g][n_pages], v_new[:, chs[g]]))
        o = sum(pvs) / dens[g]
        for r in range(group):
            h = g * group + r
            out_ref[:, h * A_DH:(h + 1) * A_DH] = o[r * t:(r + 1) * t, :]


def _sample_attention(aq, ak, av, iq, misc, pool_kt, pool_vt, pool_kit, page_table):
    db, t, _ = aq.shape
    n_pages = page_table.shape[1]
    page = pool_kit.shape[2]
    total = n_pages * page + t
    k_top = min(TOPK_MAX, total // 4)
    idx_bits = max(1, int(np.ceil(np.log2((n_pages + 1) * LANES))))
    kern = functools.partial(_sample_attn_kernel, n_pages=n_pages, page=page, k_top=k_top, idx_bits=idx_bits)
    qspec = lambda w: pl.BlockSpec((None, t, w), lambda bi, pt: (bi, 0, 0))
    pspec = lambda w, p: pl.BlockSpec((None, w, page), lambda bi, pt: (pt[bi, p], 0, 0))
    kvw = A_KV_HEADS * A_DH
    in_specs = [qspec(A_HEADS * A_DH), qspec(IDX_HEADS * IDX_DH), qspec(LANES), qspec(kvw), qspec(kvw)]
    in_specs += [pspec(IDX_DH, p) for p in range(n_pages)]
    in_specs += [pspec(kvw, p) for p in range(n_pages)]
    in_specs += [pspec(kvw, p) for p in range(n_pages)]
    return pl.pallas_call(
        kern,
        grid_spec=pltpu.PrefetchScalarGridSpec(
            num_scalar_prefetch=1, grid=(db,), in_specs=in_specs, out_specs=qspec(A_HEADS * A_DH)),
        out_shape=jax.ShapeDtypeStruct((db, t, A_HEADS * A_DH), F32),
        compiler_params=_params(("arbitrary",)),
        name="sample_attn",
    )(page_table, aq, iq, misc, ak, av, *([pool_kit] * n_pages), *([pool_kt] * n_pages), *([pool_vt] * n_pages))


def _gdn_kernel(u_ref, gz_ref, misc_ref, conv0_ref, s0_ref, convw_ref, alog_ref, dtb_ref, onorm_ref,
                out_ref, s_out_ref, conv_out_ref, state_ref, ubuf_ref, *, chunk, n_seq):
    c = pl.program_id(1)
    n_c = pl.num_programs(1)
    hist = CONV_W - 1
    base = 8

    @pl.when(c == 0)
    def _():
        state_ref[...] = s0_ref[...]
        ubuf_ref[:, base - hist:base, :] = conv0_ref[...]

    ii = lax.broadcasted_iota(I32, (chunk, chunk), 0)
    jj = lax.broadcasted_iota(I32, (chunk, chunk), 1)
    lower = jnp.where(ii >= jj, 1.0, 0.0).astype(BF16)
    upper = jnp.where(ii <= jj, 1.0, 0.0).astype(BF16)
    lane = lax.broadcasted_iota(I32, (chunk, LANES), 1)
    ga_lanes = (lane >= MISC_GA) & (lane < MISC_GA + G_HEADS)
    qk = G_HEADS * G_DK
    n_dbl = int(np.log2(chunk))
    assert 2 ** n_dbl == chunk

    per_seq = []
    for b in range(n_seq):
        ubuf_ref[b, base:base + chunk, :] = u_ref[b]
        conv = 0.0
        for jt in range(CONV_W):
            conv = conv + ubuf_ref[b, base - hist + jt:base - hist + jt + chunk, :] * convw_ref[jt:jt + 1, :]
        act = _silu(conv)
        tail = ubuf_ref[b, base + chunk - hist:base + chunk, :]
        ubuf_ref[b, base - hist:base, :] = tail
        conv_out_ref[b] = tail

        misc = misc_ref[b]
        beta_all = _sigmoid(misc)
        g_all = jnp.where(ga_lanes, -jnp.exp(alog_ref[...]) * _softplus(misc + dtb_ref[...]), 0.0)
        g1 = g_all.astype(BF16)
        r1 = g_all - g1.astype(F32)
        g2 = r1.astype(BF16)
        g3 = (r1 - g2.astype(F32)).astype(BF16)
        gc_col = _dg(lower, g1) + _dg(lower, g2) + _dg(lower, g3)
        gc_row = _dg(g1, upper, TN) + _dg(g2, upper, TN) + _dg(g3, upper, TN)
        per_seq.append((act, beta_all, gc_col, gc_row))

    chains = [(b, h) for b in range(n_seq) for h in range(G_HEADS)]
    qs, ks, kbs, gammas, egs, gcols, sols = [], [], [], [], [], [], []
    for b, h in chains:
        act, beta_all, gc_col, gc_row = per_seq[b]
        q = act[:, h * G_DK:(h + 1) * G_DK]
        k = act[:, qk + h * G_DK:qk + (h + 1) * G_DK]
        v = act[:, 2 * qk + h * G_DV:2 * qk + (h + 1) * G_DV]
        q = q * lax.rsqrt(jnp.sum(q * q, axis=-1, keepdims=True) + EPS) * (G_DK ** -0.5)
        k = k * lax.rsqrt(jnp.sum(k * k, axis=-1, keepdims=True) + EPS)
        beta = beta_all[:, MISC_GB + h:MISC_GB + h + 1]
        gcol = gc_col[:, MISC_GA + h:MISC_GA + h + 1]
        grow = gc_row[MISC_GA + h:MISC_GA + h + 1, :]
        eg = jnp.exp(gcol)
        kb = k * beta
        qs.append(q)
        ks.append(k)
        kbs.append(kb)
        gammas.append(jnp.exp(jnp.where(ii >= jj, gcol - grow, -jnp.inf)))
        egs.append(eg)
        gcols.append(gcol)
        sols.append(jnp.concatenate([v * beta, kb * eg], axis=-1))
    n_ch = len(chains)
    kk = [_dot3(kbs[i], ks[i], NT) for i in range(n_ch)]
    pws = [jnp.where(ii > jj, -(kk[i] * gammas[i]), 0.0) for i in range(n_ch)]
    for it in range(n_dbl):
        upd = [_dot3(pws[i], sols[i]) for i in range(n_ch)]
        if it + 1 < n_dbl:
            pws = [_dot3(pws[i], pws[i]) for i in range(n_ch)]
        sols = [sols[i] + upd[i] for i in range(n_ch)]
    aqk = [_dot3(qs[i], ks[i], NT) for i in range(n_ch)]
    sts = [state_ref[b, h] for b, h in chains]
    ws = [_dot3(sols[i][:, G_DV:], sts[i]) for i in range(n_ch)]
    o1 = [_dot3(qs[i] * egs[i], sts[i]) for i in range(n_ch)]
    v_new = [sols[i][:, :G_DV] - ws[i] for i in range(n_ch)]
    o2 = [_dot3(aqk[i] * gammas[i], v_new[i]) for i in range(n_ch)]
    g_last = [gcols[i][chunk - 1:chunk, :] for i in range(n_ch)]
    kv = [_dot3(ks[i] * jnp.exp(g_last[i] - gcols[i]), v_new[i], TN) for i in range(n_ch)]
    for i, (b, h) in enumerate(chains):
        state_ref[b, h] = sts[i] * jnp.exp(g_last[i]) + kv[i]
        o = _rms_rows(o1[i] + o2[i], onorm_ref[...]) * _silu(gz_ref[b, :, h * G_DV:(h + 1) * G_DV])
        out_ref[b, :, h * G_DV:(h + 1) * G_DV] = o

    @pl.when(c == n_c - 1)
    def _():
        s_out_ref[...] = state_ref[...]


def _gdn(gqkv, gz, misc, conv0, s0, g_conv, g_a_log, g_dt_bias, g_o_norm, n_seq):
    b, t, cch = gqkv.shape
    chunk = min(G_CHUNK, t)
    assert t % chunk == 0 and chunk % 8 == 0 and b % n_seq == 0
    hist = CONV_W - 1
    alog = jnp.zeros((1, LANES), F32).at[0, MISC_GA:MISC_GA + G_HEADS].set(g_a_log.astype(F32))
    dtb = jnp.zeros((1, LANES), F32).at[0, MISC_GA:MISC_GA + G_HEADS].set(g_dt_bias.astype(F32))
    kern = functools.partial(_gdn_kernel, chunk=chunk, n_seq=n_seq)
    tspec = lambda w: pl.BlockSpec((n_seq, chunk, w), lambda bi, c: (bi, c, 0))
    sspec = pl.BlockSpec((n_seq, G_HEADS, G_DK, G_DV), lambda bi, c: (bi, 0, 0, 0))
    cspec = pl.BlockSpec((n_seq, hist, cch), lambda bi, c: (bi, 0, 0))
    return pl.pallas_call(
        kern,
        grid=(b // n_seq, t // chunk),
        in_specs=[tspec(cch), tspec(G_HEADS * G_DV), tspec(LANES), cspec, sspec,
                  _const_spec((CONV_W, cch)), _const_spec((1, LANES)), _const_spec((1, LANES)),
                  _const_spec((1, G_DV))],
        out_specs=[tspec(G_HEADS * G_DV), sspec, cspec],
        out_shape=[jax.ShapeDtypeStruct((b, t, G_HEADS * G_DV), F32),
                   jax.ShapeDtypeStruct((b, G_HEADS, G_DK, G_DV), F32),
                   jax.ShapeDtypeStruct((b, hist, cch), F32)],
        scratch_shapes=[pltpu.VMEM((n_seq, G_HEADS, G_DK, G_DV), F32), pltpu.VMEM((n_seq, 8 + chunk, cch), F32)],
        compiler_params=_params(("arbitrary", "arbitrary")),
        name="gdn",
    )(gqkv, gz, misc, conv0, s0, g_conv, alog, dtb, g_o_norm.reshape(1, G_DV))


def _mem_kv_kernel(mem_ref, gain_ref, w_ref, kg_ref, mk_ref, mv_ref):
    xn = _rms_rows(mem_ref[...], gain_ref[...]).astype(BF16)
    half = M_HEADS * M_DH
    kk = jnp.dot(xn, w_ref[:, :half], preferred_element_type=F32)
    for c, y in enumerate(_headnorm_full(kk, kg_ref[...])):
        mk_ref[:, c * LANES:(c + 1) * LANES] = y
    mv_ref[...] = jnp.dot(xn, w_ref[:, half:], preferred_element_type=F32)


def _mem_kv(mem2d, mem_norm, w_mem_kv, m_k_norm, tm):
    n, d = mem2d.shape
    half = M_HEADS * M_DH
    return pl.pallas_call(
        _mem_kv_kernel,
        grid=(n // tm,),
        in_specs=[pl.BlockSpec((tm, d), lambda i: (i, 0)), _const_spec((1, d)), _const_spec((d, 2 * half)),
                  _const_spec((1, M_DH))],
        out_specs=[pl.BlockSpec((tm, half), lambda i: (i, 0))] * 2,
        out_shape=[jax.ShapeDtypeStruct((n, half), F32)] * 2,
        compiler_params=_params(("parallel",)),
        name="mem_kv",
    )(mem2d, mem_norm.reshape(1, d), w_mem_kv.astype(BF16), m_k_norm.reshape(1, M_DH))


def _mem_attn_kernel(q_ref, k_ref, v_ref, out_ref, *, interleaved):
    n_mem = k_ref.shape[0] // M_HEADS if interleaved else k_ref.shape[0]
    for h in range(M_HEADS):
        sl = slice(h * M_DH, (h + 1) * M_DH)
        if interleaved:
            kh = k_ref[pl.ds(h, n_mem, stride=M_HEADS), :]
            vh = v_ref[pl.ds(h, n_mem, stride=M_HEADS), :]
        else:
            kh, vh = k_ref[:, sl], v_ref[:, sl]
        s = _dg(q_ref[:, sl].astype(BF16), kh.astype(BF16), NT) * (M_DH ** -0.5)
        e = jnp.exp(s - jnp.max(s, axis=-1, keepdims=True))
        p = e / jnp.sum(e, axis=-1, keepdims=True)
        out_ref[:, sl] = _dg(p.astype(BF16), vh.astype(BF16))


def _mem_attn_wide_kernel(q_ref, k_ref, v_ref, out_ref, kblk_ref, vblk_ref):
    n_mem = k_ref.shape[0]

    @pl.when(pl.program_id(1) == 0)
    def _():
        kblk_ref[...] = jnp.zeros(kblk_ref.shape, BF16)
        vblk_ref[...] = jnp.zeros(vblk_ref.shape, BF16)
        for h in range(M_HEADS):
            sl = slice(h * M_DH, (h + 1) * M_DH)
            kblk_ref[h * n_mem:(h + 1) * n_mem, sl] = k_ref[:, sl].astype(BF16)
            vblk_ref[h * n_mem:(h + 1) * n_mem, sl] = v_ref[:, sl].astype(BF16)

    s_all = _dg(q_ref[...].astype(BF16), kblk_ref[...], NT) * (M_DH ** -0.5)
    probs = []
    for h in range(M_HEADS):
        s = s_all[:, h * n_mem:(h + 1) * n_mem]
        e = jnp.exp(s - jnp.max(s, axis=-1, keepdims=True))
        probs.append((e / jnp.sum(e, axis=-1, keepdims=True)).astype(BF16))
    out_ref[...] = _dg(jnp.concatenate(probs, axis=1), vblk_ref[...])


def _mem_attention_wide(mq, mk, mv, tm):
    b, t, w = mq.shape
    m = mk.shape[1]
    return pl.pallas_call(
        _mem_attn_wide_kernel,
        grid=(b, t // tm),
        in_specs=[pl.BlockSpec((None, tm, w), lambda bi, i: (bi, i, 0)),
                  pl.BlockSpec((None, m, w), lambda bi, i: (bi, 0, 0)),
                  pl.BlockSpec((None, m, w), lambda bi, i: (bi, 0, 0))],
        out_specs=pl.BlockSpec((None, tm, w), lambda bi, i: (bi, i, 0)),
        out_shape=jax.ShapeDtypeStruct((b, t, w), F32),
        scratch_shapes=[pltpu.VMEM((M_HEADS * m, w), BF16), pltpu.VMEM((M_HEADS * m, w), BF16)],
        compiler_params=_params(("arbitrary", "arbitrary")),
        name="mem_attn_wide",
    )(mq, mk, mv)


def _mem_attention(mq, mk, mv, tm, interleaved=False):
    b, t, w = mq.shape
    m, kw = mk.shape[1], mk.shape[2]
    return pl.pallas_call(
        functools.partial(_mem_attn_kernel, interleaved=interleaved),
        grid=(b, t // tm),
        in_specs=[pl.BlockSpec((None, tm, w), lambda bi, i: (bi, i, 0)),
                  pl.BlockSpec((None, m, kw), lambda bi, i: (bi, 0, 0)),
                  pl.BlockSpec((None, m, kw), lambda bi, i: (bi, 0, 0))],
        out_specs=pl.BlockSpec((None, tm, w), lambda bi, i: (bi, i, 0)),
        out_shape=jax.ShapeDtypeStruct((b, t, w), F32),
        compiler_params=_params(("parallel", "parallel")),
        name="mem_attn",
    )(mq, mk, mv)


def _merge_ffn_kernel(x_ref, a_ref, g_ref, m_ref, gates_ref, wa_ref, wg_ref, wm_ref, wo_ref, nf_ref,
                      win_ref, wout_ref, y_ref, *, d_ff, ff_chunk):
    d = x_ref.shape[1]
    gates = gates_ref[...]
    h = (gates[:, :d] * jnp.dot(a_ref[...].astype(BF16), wa_ref[...], preferred_element_type=F32)
         + gates[:, d:2 * d] * jnp.dot(g_ref[...].astype(BF16), wg_ref[...], preferred_element_type=F32)
         + gates[:, 2 * d:] * jnp.dot(m_ref[...].astype(BF16), wm_ref[...], preferred_element_type=F32))
    x1 = x_ref[...] + jnp.dot(h.astype(BF16), wo_ref[...], preferred_element_type=F32)
    xn = _rms_rows(x1, nf_ref[...]).astype(BF16)
    acc = jnp.zeros_like(x1)
    for c in range(d_ff // ff_chunk):
        lo = c * ff_chunk
        gate = jnp.dot(xn, win_ref[:, lo:lo + ff_chunk], preferred_element_type=F32)
        up = jnp.dot(xn, win_ref[:, d_ff + lo:d_ff + lo + ff_chunk], preferred_element_type=F32)
        acc = acc + jnp.dot((_silu(gate) * up).astype(BF16), wout_ref[lo:lo + ff_chunk, :],
                            preferred_element_type=F32)
    y_ref[...] = x1 + acc


def _merge_ffn(x2d, a_out, g_out, m_out, gates, w_a, w_g, w_m, w_o, norm_ffn, w_in, w_out, tm):
    n, d = x2d.shape
    d_ff = w_out.shape[0]
    ff_chunk = 2 * LANES
    assert d_ff % ff_chunk == 0
    kern = functools.partial(_merge_ffn_kernel, d_ff=d_ff, ff_chunk=ff_chunk)
    row = lambda w: pl.BlockSpec((tm, w), lambda i: (i, 0))
    return pl.pallas_call(
        kern,
        grid=(n // tm,),
        in_specs=[row(d), row(a_out.shape[1]), row(g_out.shape[1]), row(m_out.shape[1]), row(3 * d),
                  _const_spec(w_a.shape), _const_spec(w_g.shape), _const_spec(w_m.shape), _const_spec(w_o.shape),
                  _const_spec((1, d)), _const_spec(w_in.shape), _const_spec(w_out.shape)],
        out_specs=row(d),
        out_shape=jax.ShapeDtypeStruct((n, d), F32),
        compiler_params=_params(("parallel",)),
        name="merge_ffn",
    )(x2d, a_out, g_out, m_out, gates, w_a, w_g, w_m, w_o, norm_ffn.reshape(1, d), w_in, w_out)


def _tile(n, pref):
    t = min(n, pref)
    assert n % t == 0
    return t


def kernel(x_prompt, x_sample, mem_prompt, cache_k, cache_v, cache_idx_k, page_table, state_gdn, state_conv,
           cache_mem_k, cache_mem_v, norm_mix, w_in, a_q_norm, a_k_norm, g_conv, g_a_log, g_dt_bias, g_o_norm,
           mem_norm, w_mem_kv, m_q_norm, m_k_norm, w_a_out, w_g_out, w_m_out, w_o, norm_ffn, w_ffn_in, w_ffn_out):
    depth = w_in.shape[0]
    b, s, d = x_prompt.shape
    db, t, _ = x_sample.shape
    n_mem = mem_prompt.shape[1]
    kvw = A_KV_HEADS * A_DH
    cch = g_conv.shape[2]
    yp, ys = x_prompt, x_sample
    p_states, s_states = [], []
    for l in range(depth):
        bf = lambda w: w.astype(BF16)
        proj_w = (norm_mix[l], _pack_w_in(w_in[l]), _pack_w_in_t(w_in[l]), a_q_norm[l], a_k_norm[l], m_q_norm[l])
        ffn_w = (bf(w_a_out[l]), bf(w_g_out[l]), bf(w_m_out[l]), bf(w_o[l]), norm_ffn[l], bf(w_ffn_in[l]),
                 bf(w_ffn_out[l]))
        gdn_w = (g_conv[l], g_a_log[l], g_dt_bias[l], g_o_norm[l])

        x2 = yp.reshape(b * s, d)
        tq = _tile(s, 256)
        aqt, iqt, iwt, avt, ak, misc, gqkv, gz, mq, gates = _inproj(x2, *proj_w, tm=tq, seq=s)
        r3 = lambda a: a.reshape(b, s, a.shape[-1])
        a_out = _prompt_attention(aqt, iqt, iwt, r3(ak), avt, r3(misc), tq=tq)
        av = jnp.transpose(avt.reshape(b, A_KV_HEADS, A_DH, s), (0, 3, 1, 2))
        g_out, p_gdn, p_conv = _gdn(r3(gqkv), r3(gz), r3(misc),
                                    jnp.zeros((b, CONV_W - 1, cch), F32),
                                    jnp.zeros((b, G_HEADS, G_DK, G_DV), F32), *gdn_w, n_seq=b)
        mk, mv = _mem_kv(mem_prompt.reshape(b * n_mem, d), mem_norm[l], w_mem_kv[l], m_k_norm[l],
                         tm=_tile(b * n_mem, 256))
        m_out = _mem_attention_wide(r3(mq), mk.reshape(b, n_mem, -1), mv.reshape(b, n_mem, -1), tm=_tile(s, 512))
        y2 = _merge_ffn(x2, a_out.reshape(b * s, -1), g_out.reshape(b * s, -1), m_out.reshape(b * s, -1), gates,
                        *ffn_w, tm=_tile(b * s, 256))
        yp = y2.reshape(b, s, d)
        p_states.append((ak.reshape(b, s, A_KV_HEADS, A_DH), av,
                         misc[:, MISC_IK:MISC_IK + IDX_DH].reshape(b, s, IDX_DH), p_gdn, p_conv,
                         mk.reshape(b, n_mem, M_HEADS, M_DH), mv.reshape(b, n_mem, M_HEADS, M_DH)))

        xs2 = ys.reshape(db * t, d)
        aq, ak, av, iq, misc, gqkv, gz, mq, gates = _inproj(xs2, *proj_w, tm=_tile(db * t, 256))
        r3 = lambda a: a.reshape(db, t, a.shape[-1])
        n_phys, page = cache_idx_k.shape[1], cache_idx_k.shape[2]
        pool_t = lambda c: jnp.transpose(c, (0, 2, 3, 1)).reshape(n_phys, kvw, page)
        a_out = _sample_attention(r3(aq), r3(ak), r3(av), r3(iq), r3(misc), pool_t(cache_k[l]), pool_t(cache_v[l]),
                                  jnp.transpose(cache_idx_k[l], (0, 2, 1)), page_table)
        g_out, s_gdn, s_conv = _gdn(r3(gqkv), r3(gz), r3(misc), state_conv[l], state_gdn[l], *gdn_w,
                                    n_seq=_tile(db, 8))
        m_out = _mem_attention(r3(mq), cache_mem_k[l].reshape(db, n_mem * M_HEADS, M_DH),
                               cache_mem_v[l].reshape(db, n_mem * M_HEADS, M_DH), tm=t, interleaved=True)
        y2 = _merge_ffn(xs2, a_out.reshape(db * t, -1), g_out.reshape(db * t, -1), m_out.reshape(db * t, -1), gates,
                        *ffn_w, tm=_tile(db * t, 256))
        ys = y2.reshape(db, t, d)
        s_states.append((ak.reshape(db, t, A_KV_HEADS, A_DH), av.reshape(db, t, A_KV_HEADS, A_DH),
                         misc[:, MISC_IK:MISC_IK + IDX_DH].reshape(db, t, IDX_DH), s_gdn, s_conv))

    p_k, p_v, p_idx_k, p_gdn, p_conv, p_mem_k, p_mem_v = [jnp.stack(z) for z in zip(*p_states)]
    s_k, s_v, s_idx_k, s_gdn, s_conv = [jnp.stack(z) for z in zip(*s_states)]
    return (yp, ys, p_k, p_v, p_idx_k, p_gdn, p_conv, p_mem_k, p_mem_v, s_k, s_v, s_idx_k, s_gdn, s_conv)
```

```python
import functools

import numpy as np
import jax
import jax.numpy as jnp
from jax import lax
from jax.experimental import pallas as pl
from jax.experimental.pallas import tpu as pltpu

F32 = jnp.float32
BF16 = jnp.bfloat16
I32 = jnp.int32

A_HEADS = 8
A_KV_HEADS = 4
A_DH = 64
IDX_HEADS = 4
IDX_DH = 64
TOPK_MAX = 256
G_HEADS = 4
G_DK = 128
G_DV = 128
CONV_W = 4
G_CHUNK = 64
M_HEADS = 4
M_DH = 128
EPS = 1e-6

LANES = 128
VMEM_LIMIT = 56 * 1024 * 1024
INT_MIN = np.int32(-2 ** 31)

MISC_IK = 0
MISC_IW = IDX_DH
MISC_GB = MISC_IW + IDX_HEADS
MISC_GA = MISC_GB + G_HEADS

NN = (((1,), (0,)), ((), ()))
NT = (((1,), (1,)), ((), ()))
TN = (((0,), (0,)), ((), ()))


def _dg(a, b, dn=NN):
    return lax.dot_general(a, b, dn, preferred_element_type=F32)


def _dot1(a, b, dn=NN):
    return _dg(a.astype(BF16), b.astype(BF16), dn)


def _split2(x):
    hi = x.astype(BF16)
    return hi, (x - hi.astype(F32)).astype(BF16)


def _dot3(a, b, dn=NN):
    ah, al = _split2(a)
    bh, bl = _split2(b)
    return _dg(ah, bh, dn) + (_dg(ah, bl, dn) + _dg(al, bh, dn))


def _sigmoid(x):
    return 1.0 / (1.0 + jnp.exp(-x))


def _silu(x):
    return x * _sigmoid(x)


def _softplus(x):
    return jnp.maximum(x, 0.0) + jnp.log1p(jnp.exp(-jnp.abs(x)))


def _rms_rows(x, gain):
    ms = jnp.mean(x * x, axis=-1, keepdims=True)
    return x * lax.rsqrt(ms + EPS) * gain


def _headnorm_pairs(h, gain128):
    rows, width = h.shape
    lo_mask = lax.broadcasted_iota(I32, (rows, LANES), 1) < (LANES // 2)
    outs = []
    for c in range(width // LANES):
        s = h[:, c * LANES:(c + 1) * LANES]
        sq = s * s
        lo = jnp.sum(jnp.where(lo_mask, sq, 0.0), axis=-1, keepdims=True)
        hi = jnp.sum(jnp.where(lo_mask, 0.0, sq), axis=-1, keepdims=True)
        ms = jnp.where(lo_mask, lo, hi) * (2.0 / LANES)
        outs.append(s * lax.rsqrt(ms + EPS) * gain128)
    return outs


def _headnorm_full(h, gain128):
    outs = []
    for c in range(h.shape[1] // LANES):
        outs.append(_rms_rows(h[:, c * LANES:(c + 1) * LANES], gain128))
    return outs


def _const_spec(shape):
    nd = len(shape)
    return pl.BlockSpec(shape, lambda *_: (0,) * nd, pipeline_mode=pl.Buffered(1))


def _params(sem):
    return pltpu.CompilerParams(dimension_semantics=sem, vmem_limit_bytes=VMEM_LIMIT)


_P_AQ = (0, 512)
_P_AK = (512, 768)
_P_AV = (768, 1024)
_P_IQ = (1024, 1280)
_P_MISC = (1280, 1408)
_P_GQKV = (1408, 2944)
_P_GZ = (2944, 3456)
_P_MQ = (3456, 3968)
_P_GATES = (3968, 7040)
_P_WIDTH = 7040


def _pack_w_in(w_in):
    d = w_in.shape[0]
    sizes = (A_HEADS * A_DH, A_KV_HEADS * A_DH, A_KV_HEADS * A_DH, IDX_HEADS * IDX_DH, IDX_DH, IDX_HEADS,
             2 * G_HEADS * G_DK + G_HEADS * G_DV, G_HEADS * G_DV, G_HEADS, G_HEADS, M_HEADS * M_DH)
    offs = np.cumsum((0,) + sizes)
    aq, ak, av, iq, ik, iw, gqkv, gz, gb, ga, mq = (w_in[:, offs[i]:offs[i + 1]] for i in range(len(sizes)))
    gates = w_in[:, offs[-1]:]
    pad = jnp.zeros((d, LANES - (IDX_DH + IDX_HEADS + 2 * G_HEADS)), w_in.dtype)
    misc = jnp.concatenate([ik, iw, gb, ga, pad], axis=1)
    packed = jnp.concatenate([aq, ak, av, iq, misc, gqkv, gz, mq, gates], axis=1)
    assert packed.shape[1] == _P_WIDTH
    return packed.astype(BF16)


_T_AQ = (0, 512)
_T_IQ = (512, 768)
_T_IW = (768, 776)
_T_AV = (776, 1032)
_T_ROWS = 1032


def _pack_w_in_t(w_in):
    sizes = (A_HEADS * A_DH, A_KV_HEADS * A_DH, A_KV_HEADS * A_DH, IDX_HEADS * IDX_DH, IDX_DH, IDX_HEADS)
    offs = np.cumsum((0,) + sizes)
    aq, _, av, iq, _, iw = (w_in[:, offs[i]:offs[i + 1]] for i in range(len(sizes)))
    pad = jnp.zeros((w_in.shape[0], _T_IW[1] - _T_IW[0] - IDX_HEADS), w_in.dtype)
    packed = jnp.concatenate([aq, iq, iw, pad, av], axis=1).T
    assert packed.shape[0] == _T_ROWS
    return packed.astype(BF16)


def _inproj_common(xn, w_ref, akg_ref, mqg_ref, misc_scale_ref, ak_ref, misc_ref, gqkv_ref, gz_ref, mq_ref, gates_ref):
    def proj(rng):
        return jnp.dot(xn, w_ref[:, rng[0]:rng[1]], preferred_element_type=F32)

    for c, y in enumerate(_headnorm_pairs(proj(_P_AK), akg_ref[...])):
        ak_ref[:, c * LANES:(c + 1) * LANES] = y
    misc_ref[...] = proj(_P_MISC) * misc_scale_ref[...]
    gqkv_ref[...] = proj(_P_GQKV)
    gz_ref[...] = proj(_P_GZ)
    for c, y in enumerate(_headnorm_full(proj(_P_MQ), mqg_ref[...])):
        mq_ref[:, c * LANES:(c + 1) * LANES] = y
    gates_ref[...] = _sigmoid(proj(_P_GATES))
    return proj


def _inproj_kernel(x_ref, gain_ref, w_ref, aqg_ref, akg_ref, mqg_ref, misc_scale_ref,
                   aq_ref, ak_ref, av_ref, iq_ref, misc_ref, gqkv_ref, gz_ref, mq_ref, gates_ref):
    xn = _rms_rows(x_ref[...], gain_ref[...]).astype(BF16)
    proj = _inproj_common(xn, w_ref, akg_ref, mqg_ref, misc_scale_ref, ak_ref, misc_ref, gqkv_ref, gz_ref, mq_ref,
                          gates_ref)
    for c, y in enumerate(_headnorm_pairs(proj(_P_AQ), aqg_ref[...])):
        aq_ref[:, c * LANES:(c + 1) * LANES] = y
    av_ref[...] = proj(_P_AV)
    iq_ref[...] = proj(_P_IQ)


def _inproj_t_kernel(x_ref, gain_ref, w_ref, wt_ref, aqg_ref, akg_ref, mqg_ref, misc_scale_ref,
                     aqt_ref, iqt_ref, iwt_ref, avt_ref, ak_ref, misc_ref, gqkv_ref, gz_ref, mq_ref, gates_ref):
    xn = _rms_rows(x_ref[...], gain_ref[...]).astype(BF16)
    _inproj_common(xn, w_ref, akg_ref, mqg_ref, misc_scale_ref, ak_ref, misc_ref, gqkv_ref, gz_ref, mq_ref, gates_ref)

    def proj_t(rng):
        return _dg(wt_ref[rng[0]:rng[1], :], xn, NT)

    aqt = proj_t(_T_AQ)
    gain_col = aqg_ref[...]
    for h in range(A_HEADS):
        qh = aqt[h * A_DH:(h + 1) * A_DH, :]
        ms = jnp.mean(qh * qh, axis=0, keepdims=True)
        aqt_ref[h * A_DH:(h + 1) * A_DH, :] = qh * lax.rsqrt(ms + EPS) * gain_col
    iqt_ref[...] = proj_t(_T_IQ)
    iwt_ref[...] = proj_t(_T_IW) * (IDX_HEADS ** -0.5)
    avt_ref[...] = proj_t(_T_AV)


def _inproj(x2d, norm_mix, w_packed, w_packed_t, a_q_norm, a_k_norm, m_q_norm, tm, seq=None):
    n, d = x2d.shape
    transposed = seq is not None
    lane = np.arange(LANES)
    misc_scale = np.where((lane >= MISC_IW) & (lane < MISC_GB), IDX_HEADS ** -0.5, 1.0).astype(np.float32)[None]
    akg = jnp.tile(a_k_norm.reshape(1, A_DH), (1, 2))
    mqg = m_q_norm.reshape(1, M_DH)
    common = [r[1] - r[0] for r in (_P_MISC, _P_GQKV, _P_GZ, _P_MQ, _P_GATES)]
    row_spec = lambda w: pl.BlockSpec((tm, w), lambda i: (i, 0))
    row_shape = lambda w: jax.ShapeDtypeStruct((n, w), F32)
    x_specs = [row_spec(d), _const_spec((1, d)), _const_spec((d, _P_WIDTH))]
    g_specs = [_const_spec((1, LANES)), _const_spec((1, LANES)), _const_spec((1, LANES))]
    if not transposed:
        widths = [512, 256, 256, 256] + common
        return pl.pallas_call(
            _inproj_kernel,
            grid=(n // tm,),
            in_specs=x_specs + [_const_spec((1, LANES))] + g_specs,
            out_specs=[row_spec(w) for w in widths],
            out_shape=[row_shape(w) for w in widths],
            compiler_params=_params(("parallel",)),
            name="inproj",
        )(x2d, norm_mix.reshape(1, d), w_packed, jnp.tile(a_q_norm.reshape(1, A_DH), (1, 2)), akg, mqg,
          jnp.asarray(misc_scale))
    t_rows = [r[1] - r[0] for r in (_T_AQ, _T_IQ, _T_IW, _T_AV)]
    widths = [256] + common
    tiles = seq // tm
    col_spec = lambda r: pl.BlockSpec((None, r, tm), lambda i: (i // tiles, 0, i % tiles))
    return pl.pallas_call(
        _inproj_t_kernel,
        grid=(n // tm,),
        in_specs=x_specs + [_const_spec((_T_ROWS, d)), _const_spec((A_DH, 1))] + g_specs,
        out_specs=[col_spec(r) for r in t_rows] + [row_spec(w) for w in widths],
        out_shape=[jax.ShapeDtypeStruct((n // seq, r, seq), F32) for r in t_rows] + [row_shape(w) for w in widths],
        compiler_params=_params(("parallel",)),
        name="inproj_t",
    )(x2d, norm_mix.reshape(1, d), w_packed, w_packed_t, a_q_norm.reshape(A_DH, 1), akg, mqg, jnp.asarray(misc_scale))


KEY_NEG_INF = np.int32(-2 ** 31 + 0x7FFFFF)


def _canon_zero(score):
    return jnp.where(score == 0.0, 0.0, score)


def _key_to_float(key):
    key = jnp.maximum(key, KEY_NEG_INF)
    return lax.bitcast_convert_type(key ^ ((key >> 31) & np.int32(0x7FFFFFFF)), F32)


def _as_i32(v):
    return np.int32(v - (1 << 32) if v >= (1 << 31) else v)


def _radix_select(count_fn, shape, bcast, k_top, idx_bits, bits_per_pass):
    thr_f, need, n_eq = _radix_threshold(count_fn, shape, bcast, k_top, bits_per_pass)
    del n_eq
    return thr_f, _tie_cut(count_fn, shape, bcast, thr_f, need, idx_bits, bits_per_pass)


def _radix_threshold(count_fn, shape, bcast, k_top, bits_per_pass):
    thr = jnp.full(shape, INT_MIN, I32)
    for hi in range(32, 0, -bits_per_pass):
        lo = max(hi - bits_per_pass, 0)
        cands = [bcast(_key_to_float(thr ^ _as_i32(v << lo))) for v in range(1, 2 ** (hi - lo))]
        cnts = count_fn([lambda s, c, cf=cf: s >= cf for cf in cands])
        digit = sum(jnp.where(cnt >= k_top, 1, 0) for cnt in cnts)
        thr = thr ^ (digit << lo)
    thr_f = _key_to_float(thr)
    thr_b = bcast(thr_f)
    n_gt, n_eq = count_fn([lambda s, c: s > thr_b, lambda s, c: s == thr_b])
    return thr_f, k_top - n_gt, n_eq


def _tie_cut(count_fn, shape, bcast, thr_f, need, idx_bits, bits_per_pass):
    thr_b = bcast(thr_f)
    cut = jnp.zeros(shape, I32)
    for hi in range(idx_bits, 0, -bits_per_pass):
        lo = max(hi - bits_per_pass, 0)
        cands = [bcast(cut | np.int32(v << lo)) for v in range(1, 2 ** (hi - lo))]
        cnts = count_fn([lambda s, c, cb=cb: (s == thr_b) & (c < cb) for cb in cands])
        digit = sum(jnp.where(cnt < need, 1, 0) for cnt in cnts)
        cut = cut | (digit << lo)
    return cut


def _prompt_attn_kernel(aqt_ref, iqt_ref, iwt_ref, k_ref, vt_ref, misc_ref, out_ref,
                        score_ref, kb_ref, ki_ref, vth_ref, qblk_ref, acc_ref, cut_ref, *, tq, k_top, idx_bits):
    j = pl.program_id(1)
    nk = j + 1
    ck = tq
    n_chunks = score_ref.shape[0]
    sub = 8
    group = A_HEADS // A_KV_HEADS
    head_rows = [slice(h * A_DH, (h + 1) * A_DH) for h in range(A_HEADS)]
    head_cols = [slice(h * tq, (h + 1) * tq) for h in range(A_HEADS)]
    kv_rows = [slice((h // group) * A_DH, (h // group + 1) * A_DH) for h in range(A_HEADS)]

    @pl.when(j == 0)
    def _():
        for c in range(n_chunks):
            rows = slice(c * ck, (c + 1) * ck)
            kb_ref[rows, :] = k_ref[rows, :].astype(BF16)
            ki_ref[rows, :] = misc_ref[rows, MISC_IK:MISC_IK + IDX_DH].astype(BF16)
            vth_ref[c] = vt_ref[:, rows].astype(BF16)
        qblk_ref[...] = jnp.zeros(qblk_ref.shape, BF16)

    key_in_chunk = lax.broadcasted_iota(I32, (ck, tq), 0)
    q_pos = j * tq + lax.broadcasted_iota(I32, (ck, tq), 1)

    iqb = (iqt_ref[...] * (IDX_DH ** -0.5)).astype(BF16)
    iq_cat = jnp.concatenate([iqb[h * IDX_DH:(h + 1) * IDX_DH, :] for h in range(IDX_HEADS)], axis=1)
    iw = iwt_ref[...]

    def idx_body(c, carry):
        start = pl.multiple_of(c * ck, ck)
        dots = _dg(ki_ref[pl.ds(start, ck), :], iq_cat)
        acc = jnp.zeros((ck, tq), F32)
        for h in range(IDX_HEADS):
            acc = acc + jnp.maximum(dots[:, h * tq:(h + 1) * tq], 0.0) * iw[h:h + 1, :]
        score = jnp.where(c * ck + key_in_chunk <= q_pos, acc, -jnp.inf)
        score_ref[c] = _canon_zero(score)
        return carry

    lax.fori_loop(0, nk, idx_body, 0)

    key_in_group = lax.broadcasted_iota(I32, (sub, tq), 0)
    n_acc = 4

    def count_fn(preds):
        def body(c, cnts):
            cnts = [list(a) for a in cnts]
            for i in range(ck // sub):
                sc = score_ref[c, i * sub:(i + 1) * sub, :]
                key_id = c * ck + i * sub + key_in_group
                for n, pred in enumerate(preds):
                    cnts[n][i % n_acc] = cnts[n][i % n_acc] + jnp.where(pred(sc, key_id), 1.0, 0.0)
            return tuple(tuple(a) for a in cnts)
        zero = tuple(jnp.zeros((sub, tq), F32) for _ in range(n_acc))
        cnts = lax.fori_loop(0, nk, body, tuple(zero for _ in preds))
        return [jnp.sum(sum(a), axis=0, keepdims=True) for a in cnts]

    bcast = lambda v: jnp.broadcast_to(v, (sub, tq))
    thr, need, n_eq = _radix_threshold(count_fn, (1, tq), bcast, k_top, 1)
    cut_ref[...] = jnp.full((1, tq), 2 ** idx_bits - 1, I32)

    @pl.when(jnp.max(n_eq - need) > 0.5)
    def _():
        cut_ref[...] = _tie_cut(count_fn, (1, tq), bcast, thr, need, idx_bits, 1)

    cut = cut_ref[...]

    def mask_body(c, carry):
        sc = score_ref[c]
        key_id = c * ck + key_in_chunk
        sel = ((sc > thr) | ((sc == thr) & (key_id <= cut))) & (key_id <= q_pos)
        score_ref[c] = jnp.where(sel, 0.0, -jnp.inf)
        return carry

    lax.fori_loop(0, nk, mask_body, 0)

    qb = (aqt_ref[...] * (A_DH ** -0.5)).astype(BF16)
    for h in range(A_HEADS):
        qblk_ref[kv_rows[h], head_cols[h]] = qb[head_rows[h], :]
    acc_ref[...] = jnp.zeros(acc_ref.shape, F32)

    def chunk_scores(c):
        start = pl.multiple_of(c * ck, ck)
        return _dg(kb_ref[pl.ds(start, ck), :], qblk_ref[...])

    def fold(x, acc, op):
        for i in range(ck // sub):
            acc = op(acc, x[i * sub:(i + 1) * sub, :])
        return acc

    def att_body(c, carry):
        ms, lparts = carry
        s_all = chunk_scores(c)
        bias = score_ref[c]
        scores = [s_all[:, head_cols[h]] + bias for h in range(A_HEADS)]
        m_new = [jnp.maximum(ms[h], jnp.max(fold(scores[h], jnp.full((sub, tq), -1e30, F32), jnp.maximum),
                                            axis=0, keepdims=True)) for h in range(A_HEADS)]
        alphas = [jnp.exp(ms[h] - m_new[h]) for h in range(A_HEADS)]
        probs, new_l = [], []
        for h in range(A_HEADS):
            p = jnp.exp(scores[h] - m_new[h])
            new_l.append(fold(p, alphas[h] * lparts[h], jnp.add))
            probs.append(p.astype(BF16))
        pv = _dg(vth_ref[c], jnp.concatenate(probs, axis=1))
        for h in range(A_HEADS):
            acc_ref[head_rows[h], :] = alphas[h] * acc_ref[head_rows[h], :] + pv[kv_rows[h], head_cols[h]]
        return tuple(m_new), tuple(new_l)

    init = (tuple(jnp.full((1, tq), -1e30, F32) for _ in range(A_HEADS)),
            tuple(jnp.zeros((sub, tq), F32) for _ in range(A_HEADS)))
    _, lparts = lax.fori_loop(0, nk, att_body, init)
    for h in range(A_HEADS):
        acc_ref[head_rows[h], :] = acc_ref[head_rows[h], :] / jnp.sum(lparts[h], axis=0, keepdims=True)
    out_ref[...] = acc_ref[...].T


def _prompt_attention(aqt, iqt, iwt, ak, avt, misc, tq):
    b, s, kvw = ak.shape
    nq = s // tq
    k_top = min(TOPK_MAX, s // 4)
    idx_bits = max(1, int(np.ceil(np.log2(s))))
    kern = functools.partial(_prompt_attn_kernel, tq=tq, k_top=k_top, idx_bits=idx_bits)
    qspec = lambda r: pl.BlockSpec((None, r, tq), lambda bi, j: (bi, 0, j))
    kspec = lambda w: pl.BlockSpec((None, s, w), lambda bi, j: (bi, 0, 0))
    return pl.pallas_call(
        kern,
        grid=(b, nq),
        in_specs=[qspec(A_HEADS * A_DH), qspec(IDX_HEADS * IDX_DH), qspec(_T_IW[1] - _T_IW[0]),
                  kspec(kvw), pl.BlockSpec((None, kvw, s), lambda bi, j: (bi, 0, 0)), kspec(LANES)],
        out_specs=pl.BlockSpec((None, tq, A_HEADS * A_DH), lambda bi, j: (bi, j, 0)),
        out_shape=jax.ShapeDtypeStruct((b, s, A_HEADS * A_DH), F32),
        scratch_shapes=[pltpu.VMEM((nq, tq, tq), F32),
                        pltpu.VMEM((s, kvw), BF16), pltpu.VMEM((s, IDX_DH), BF16),
                        pltpu.VMEM((nq, kvw, tq), BF16), pltpu.VMEM((kvw, A_HEADS * tq), BF16),
                        pltpu.VMEM((A_HEADS * A_DH, tq), F32), pltpu.VMEM((1, tq), I32)],
        compiler_params=_params(("arbitrary", "arbitrary")),
        name="prompt_attn",
    )(aqt, iqt, iwt, ak, avt, misc)


def _sample_attn_kernel(pt_ref, aq_ref, iq_ref, misc_ref, kn_ref, vn_ref, *refs, n_pages, page, k_top, idx_bits):
    ki_refs = refs[:n_pages]
    k_refs = refs[n_pages:2 * n_pages]
    v_refs = refs[2 * n_pages:3 * n_pages]
    out_ref = refs[3 * n_pages]
    cut_ref = refs[3 * n_pages + 1]
    t = aq_ref.shape[0]
    assert page == LANES and t <= LANES
    row_ids = lax.broadcasted_iota(I32, (t, LANES), 0)
    lane_ids = lax.broadcasted_iota(I32, (t, LANES), 1)
    group = A_HEADS // A_KV_HEADS

    iqb = (iq_ref[...] * (IDX_DH ** -0.5)).astype(BF16)
    iw = misc_ref[:, MISC_IW:MISC_IW + IDX_HEADS]

    def idx_scores(kic, dn):
        acc = jnp.zeros((t, LANES), F32)
        for h in range(IDX_HEADS):
            d = _dg(iqb[:, h * IDX_DH:(h + 1) * IDX_DH], kic, dn)
            acc = acc + jnp.maximum(d, 0.0) * iw[:, h:h + 1]
        return acc

    keys = [_canon_zero(idx_scores(ki_refs[p][...].astype(BF16), NN)) for p in range(n_pages)]
    new_rows = jnp.concatenate([misc_ref[...], jnp.zeros((LANES - t, LANES), F32)], axis=0)
    new_ok = lane_ids <= row_ids
    new_scores = idx_scores(new_rows[:, MISC_IK:MISC_IK + IDX_DH].astype(BF16), NT)
    keys.append(jnp.where(new_ok, _canon_zero(new_scores), -jnp.inf))

    def count_fn(preds):
        cnts = []
        for pred in preds:
            cnt = jnp.zeros((t, LANES), F32)
            for p, kc in enumerate(keys):
                cnt = cnt + jnp.where(pred(kc, p * LANES + lane_ids), 1.0, 0.0)
            cnts.append(jnp.sum(cnt, axis=-1, keepdims=True))
        return cnts

    bcast = lambda v: jnp.broadcast_to(v, (t, LANES))
    thr, need, n_eq = _radix_threshold(count_fn, (t, 1), bcast, k_top, 4)
    cut_ref[...] = jnp.full((t, 1), 2 ** idx_bits - 1, I32)

    @pl.when(jnp.max(n_eq - need) > 0.5)
    def _():
        cut_ref[...] = _tie_cut(count_fn, (t, 1), bcast, thr, need, idx_bits, 4)

    cut = cut_ref[...]

    qb = (aq_ref[...] * (A_DH ** -0.5)).astype(BF16)
    q_stack = [jnp.concatenate([qb[:, (g * group + r) * A_DH:(g * group + r + 1) * A_DH] for r in range(group)], axis=0)
               for g in range(A_KV_HEADS)]
    pad_kv = jnp.zeros((LANES - t, A_KV_HEADS * A_DH), F32)
    k_new = jnp.concatenate([kn_ref[...], pad_kv], axis=0).astype(BF16)
    v_new = jnp.concatenate([vn_ref[...], pad_kv], axis=0).astype(BF16)
    biases = []
    for p in range(n_pages + 1):
        cols = p * LANES + lane_ids
        sel = (keys[p] > thr) | ((keys[p] == thr) & (cols <= cut))
        if p == n_pages:
            sel = sel & new_ok
        bias = jnp.where(sel, 0.0, -jnp.inf)
        biases.append(jnp.concatenate([bias] * group, axis=0))
    chs = [slice(g * A_DH, (g + 1) * A_DH) for g in range(A_KV_HEADS)]
    scores = []
    for g in range(A_KV_HEADS):
        sg = [_dg(q_stack[g], k_refs[p][chs[g], :].astype(BF16)) for p in range(n_pages)]
        sg.append(_dg(q_stack[g], k_new[:, chs[g]], NT))
        scores.append(sg)
    probs, dens = [], []
    for g in range(A_KV_HEADS):
        sg = [s + biases[p] for p, s in enumerate(scores[g])]
        mx = sg[0]
        for s in sg[1:]:
            mx = jnp.maximum(mx, s)
        mx = jnp.max(mx, axis=-1, keepdims=True)
        es = [jnp.exp(s - mx) for s in sg]
        dens.append(jnp.sum(sum(es), axis=-1, keepdims=True))
        probs.append([e.astype(BF16) for e in es])
    for g in range(A_KV_HEADS):
        pvs = [_dg(probs[g][p], v_refs[p][chs[g], :].astype(BF16), NT) for p in range(n_pages)]
        pvs.append(_dg(probs[g][n_pages], v_new[:, chs[g]]))
        o = sum(pvs) / dens[g]
        for r in range(group):
            h = g * group + r
            out_ref[:, h * A_DH:(h + 1) * A_DH] = o[r * t:(r + 1) * t, :]


def _sample_attention(aq, ak, av, iq, misc, pool_kt, pool_vt, pool_kit, page_table):
    db, t, _ = aq.shape
    n_pages = page_table.shape[1]
    page = pool_kit.shape[2]
    total = n_pages * page + t
    k_top = min(TOPK_MAX, total // 4)
    idx_bits = max(1, int(np.ceil(np.log2((n_pages + 1) * LANES))))
    kern = functools.partial(_sample_attn_kernel, n_pages=n_pages, page=page, k_top=k_top, idx_bits=idx_bits)
    qspec = lambda w: pl.BlockSpec((None, t, w), lambda bi, pt: (bi, 0, 0))
    pspec = lambda w, p: pl.BlockSpec((None, w, page), lambda bi, pt: (pt[bi, p], 0, 0))
    kvw = A_KV_HEADS * A_DH
    in_specs = [qspec(A_HEADS * A_DH), qspec(IDX_HEADS * IDX_DH), qspec(LANES), qspec(kvw), qspec(kvw)]
    in_specs += [pspec(IDX_DH, p) for p in range(n_pages)]
    in_specs += [pspec(kvw, p) for p in range(n_pages)]
    in_specs += [pspec(kvw, p) for p in range(n_pages)]
    return pl.pallas_call(
        kern,
        grid_spec=pltpu.PrefetchScalarGridSpec(
            num_scalar_prefetch=1, grid=(db,), in_specs=in_specs, out_specs=qspec(A_HEADS * A_DH),
            scratch_shapes=[pltpu.VMEM((t, 1), I32)]),
        out_shape=jax.ShapeDtypeStruct((db, t, A_HEADS * A_DH), F32),
        compiler_params=_params(("arbitrary",)),
        name="sample_attn",
    )(page_table, aq, iq, misc, ak, av, *([pool_kit] * n_pages), *([pool_kt] * n_pages), *([pool_vt] * n_pages))


def _gdn_kernel(u_ref, gz_ref, misc_ref, conv0_ref, s0_ref, convw_ref, alog_ref, dtb_ref, onorm_ref,
                out_ref, s_out_ref, conv_out_ref, state_ref, ubuf_ref, *, chunk, n_seq):
    c = pl.program_id(1)
    n_c = pl.num_programs(1)
    hist = CONV_W - 1
    base = 8

    @pl.when(c == 0)
    def _():
        state_ref[...] = s0_ref[...]
        ubuf_ref[:, base - hist:base, :] = conv0_ref[...]

    ii = lax.broadcasted_iota(I32, (chunk, chunk), 0)
    jj = lax.broadcasted_iota(I32, (chunk, chunk), 1)
    lower = jnp.where(ii >= jj, 1.0, 0.0).astype(BF16)
    upper = jnp.where(ii <= jj, 1.0, 0.0).astype(BF16)
    lane = lax.broadcasted_iota(I32, (chunk, LANES), 1)
    ga_lanes = (lane >= MISC_GA) & (lane < MISC_GA + G_HEADS)
    qk = G_HEADS * G_DK
    n_dbl = int(np.log2(chunk))
    assert 2 ** n_dbl == chunk

    per_seq = []
    for b in range(n_seq):
        ubuf_ref[b, base:base + chunk, :] = u_ref[b]
        conv = 0.0
        for jt in range(CONV_W):
            conv = conv + ubuf_ref[b, base - hist + jt:base - hist + jt + chunk, :] * convw_ref[jt:jt + 1, :]
        act = _silu(conv)
        tail = ubuf_ref[b, base + chunk - hist:base + chunk, :]
        ubuf_ref[b, base - hist:base, :] = tail
        conv_out_ref[b] = tail

        misc = misc_ref[b]
        beta_all = _sigmoid(misc)
        g_all = jnp.where(ga_lanes, -jnp.exp(alog_ref[...]) * _softplus(misc + dtb_ref[...]), 0.0)
        g1 = g_all.astype(BF16)
        r1 = g_all - g1.astype(F32)
        g2 = r1.astype(BF16)
        g3 = (r1 - g2.astype(F32)).astype(BF16)
        gc_col = _dg(lower, g1) + _dg(lower, g2) + _dg(lower, g3)
        gc_row = _dg(g1, upper, TN) + _dg(g2, upper, TN) + _dg(g3, upper, TN)
        per_seq.append((act, beta_all, gc_col, gc_row))

    chains = [(b, h) for b in range(n_seq) for h in range(G_HEADS)]
    qs, ks, kbs, gammas, egs, gcols, sols = [], [], [], [], [], [], []
    for b, h in chains:
        act, beta_all, gc_col, gc_row = per_seq[b]
        q = act[:, h * G_DK:(h + 1) * G_DK]
        k = act[:, qk + h * G_DK:qk + (h + 1) * G_DK]
        v = act[:, 2 * qk + h * G_DV:2 * qk + (h + 1) * G_DV]
        q = q * lax.rsqrt(jnp.sum(q * q, axis=-1, keepdims=True) + EPS) * (G_DK ** -0.5)
        k = k * lax.rsqrt(jnp.sum(k * k, axis=-1, keepdims=True) + EPS)
        beta = beta_all[:, MISC_GB + h:MISC_GB + h + 1]
        gcol = gc_col[:, MISC_GA + h:MISC_GA + h + 1]
        grow = gc_row[MISC_GA + h:MISC_GA + h + 1, :]
        eg = jnp.exp(gcol)
        kb = k * beta
        qs.append(q)
        ks.append(k)
        kbs.append(kb)
        gammas.append(jnp.exp(jnp.where(ii >= jj, gcol - grow, -jnp.inf)))
        egs.append(eg)
        gcols.append(gcol)
        sols.append(jnp.concatenate([v * beta, kb * eg], axis=-1))
    n_ch = len(chains)
    kk = [_dot1(kbs[i], ks[i], NT) for i in range(n_ch)]
    pws = [jnp.where(ii > jj, -(kk[i] * gammas[i]), 0.0) for i in range(n_ch)]
    for it in range(n_dbl):
        upd = [_dot3(pws[i], sols[i]) for i in range(n_ch)]
        if it + 1 < n_dbl:
            pws = [_dot1(pws[i], pws[i]) for i in range(n_ch)]
        sols = [sols[i] + upd[i] for i in range(n_ch)]
    aqk = [_dot1(qs[i], ks[i], NT) for i in range(n_ch)]
    sts = [state_ref[b, h] for b, h in chains]
    ws = [_dot1(sols[i][:, G_DV:], sts[i]) for i in range(n_ch)]
    o1 = [_dot1(qs[i] * egs[i], sts[i]) for i in range(n_ch)]
    v_new = [sols[i][:, :G_DV] - ws[i] for i in range(n_ch)]
    o2 = [_dot1(aqk[i] * gammas[i], v_new[i]) for i in range(n_ch)]
    g_last = [gcols[i][chunk - 1:chunk, :] for i in range(n_ch)]
    kv = [_dot1(ks[i] * jnp.exp(g_last[i] - gcols[i]), v_new[i], TN) for i in range(n_ch)]
    for i, (b, h) in enumerate(chains):
        state_ref[b, h] = sts[i] * jnp.exp(g_last[i]) + kv[i]
        o = _rms_rows(o1[i] + o2[i], onorm_ref[...]) * _silu(gz_ref[b, :, h * G_DV:(h + 1) * G_DV])
        out_ref[b, :, h * G_DV:(h + 1) * G_DV] = o

    @pl.when(c == n_c - 1)
    def _():
        s_out_ref[...] = state_ref[...]


def _gdn(gqkv, gz, misc, conv0, s0, g_conv, g_a_log, g_dt_bias, g_o_norm, n_seq):
    b, t, cch = gqkv.shape
    chunk = min(G_CHUNK, t)
    assert t % chunk == 0 and chunk % 8 == 0 and b % n_seq == 0
    hist = CONV_W - 1
    alog = jnp.zeros((1, LANES), F32).at[0, MISC_GA:MISC_GA + G_HEADS].set(g_a_log.astype(F32))
    dtb = jnp.zeros((1, LANES), F32).at[0, MISC_GA:MISC_GA + G_HEADS].set(g_dt_bias.astype(F32))
    kern = functools.partial(_gdn_kernel, chunk=chunk, n_seq=n_seq)
    tspec = lambda w: pl.BlockSpec((n_seq, chunk, w), lambda bi, c: (bi, c, 0))
    sspec = pl.BlockSpec((n_seq, G_HEADS, G_DK, G_DV), lambda bi, c: (bi, 0, 0, 0))
    cspec = pl.BlockSpec((n_seq, hist, cch), lambda bi, c: (bi, 0, 0))
    return pl.pallas_call(
        kern,
        grid=(b // n_seq, t // chunk),
        in_specs=[tspec(cch), tspec(G_HEADS * G_DV), tspec(LANES), cspec, sspec,
                  _const_spec((CONV_W, cch)), _const_spec((1, LANES)), _const_spec((1, LANES)),
                  _const_spec((1, G_DV))],
        out_specs=[tspec(G_HEADS * G_DV), sspec, cspec],
        out_shape=[jax.ShapeDtypeStruct((b, t, G_HEADS * G_DV), F32),
                   jax.ShapeDtypeStruct((b, G_HEADS, G_DK, G_DV), F32),
                   jax.ShapeDtypeStruct((b, hist, cch), F32)],
        scratch_shapes=[pltpu.VMEM((n_seq, G_HEADS, G_DK, G_DV), F32), pltpu.VMEM((n_seq, 8 + chunk, cch), F32)],
        compiler_params=_params(("arbitrary", "arbitrary")),
        name="gdn",
    )(gqkv, gz, misc, conv0, s0, g_conv, alog, dtb, g_o_norm.reshape(1, G_DV))


def _mem_kv_kernel(mem_ref, gain_ref, w_ref, kg_ref, mk_ref, mv_ref):
    xn = _rms_rows(mem_ref[...], gain_ref[...]).astype(BF16)
    half = M_HEADS * M_DH
    kk = jnp.dot(xn, w_ref[:, :half], preferred_element_type=F32)
    for c, y in enumerate(_headnorm_full(kk, kg_ref[...])):
        mk_ref[:, c * LANES:(c + 1) * LANES] = y
    mv_ref[...] = jnp.dot(xn, w_ref[:, half:], preferred_element_type=F32)


def _mem_kv(mem2d, mem_norm, w_mem_kv, m_k_norm, tm):
    n, d = mem2d.shape
    half = M_HEADS * M_DH
    return pl.pallas_call(
        _mem_kv_kernel,
        grid=(n // tm,),
        in_specs=[pl.BlockSpec((tm, d), lambda i: (i, 0)), _const_spec((1, d)), _const_spec((d, 2 * half)),
                  _const_spec((1, M_DH))],
        out_specs=[pl.BlockSpec((tm, half), lambda i: (i, 0))] * 2,
        out_shape=[jax.ShapeDtypeStruct((n, half), F32)] * 2,
        compiler_params=_params(("parallel",)),
        name="mem_kv",
    )(mem2d, mem_norm.reshape(1, d), w_mem_kv.astype(BF16), m_k_norm.reshape(1, M_DH))


def _mem_attn_kernel(q_ref, k_ref, v_ref, out_ref, *, interleaved):
    if interleaved:
        kb = k_ref[...].astype(BF16)
        vb = v_ref[...].astype(BF16)
        row_head = lax.broadcasted_iota(I32, (q_ref.shape[0], k_ref.shape[0]), 1) % M_HEADS
    for h in range(M_HEADS):
        sl = slice(h * M_DH, (h + 1) * M_DH)
        if interleaved:
            s = _dg(q_ref[:, sl].astype(BF16), kb, NT) * (M_DH ** -0.5)
            s = jnp.where(row_head == h, s, -jnp.inf)
            vh = vb
        else:
            s = _dg(q_ref[:, sl].astype(BF16), k_ref[:, sl].astype(BF16), NT) * (M_DH ** -0.5)
            vh = v_ref[:, sl].astype(BF16)
        e = jnp.exp(s - jnp.max(s, axis=-1, keepdims=True))
        p = e / jnp.sum(e, axis=-1, keepdims=True)
        out_ref[:, sl] = _dg(p.astype(BF16), vh)


def _mem_attn_wide_kernel(q_ref, k_ref, v_ref, out_ref, kblk_ref, vblk_ref):
    n_mem = k_ref.shape[0]

    @pl.when(pl.program_id(1) == 0)
    def _():
        kblk_ref[...] = jnp.zeros(kblk_ref.shape, BF16)
        vblk_ref[...] = jnp.zeros(vblk_ref.shape, BF16)
        for h in range(M_HEADS):
            sl = slice(h * M_DH, (h + 1) * M_DH)
            kblk_ref[h * n_mem:(h + 1) * n_mem, sl] = k_ref[:, sl].astype(BF16)
            vblk_ref[h * n_mem:(h + 1) * n_mem, sl] = v_ref[:, sl].astype(BF16)

    s_all = _dg(q_ref[...].astype(BF16), kblk_ref[...], NT) * (M_DH ** -0.5)
    probs = []
    for h in range(M_HEADS):
        s = s_all[:, h * n_mem:(h + 1) * n_mem]
        e = jnp.exp(s - jnp.max(s, axis=-1, keepdims=True))
        probs.append((e / jnp.sum(e, axis=-1, keepdims=True)).astype(BF16))
    out_ref[...] = _dg(jnp.concatenate(probs, axis=1), vblk_ref[...])


def _mem_attention_wide(mq, mk, mv, tm):
    b, t, w = mq.shape
    m = mk.shape[1]
    return pl.pallas_call(
        _mem_attn_wide_kernel,
        grid=(b, t // tm),
        in_specs=[pl.BlockSpec((None, tm, w), lambda bi, i: (bi, i, 0)),
                  pl.BlockSpec((None, m, w), lambda bi, i: (bi, 0, 0)),
                  pl.BlockSpec((None, m, w), lambda bi, i: (bi, 0, 0))],
        out_specs=pl.BlockSpec((None, tm, w), lambda bi, i: (bi, i, 0)),
        out_shape=jax.ShapeDtypeStruct((b, t, w), F32),
        scratch_shapes=[pltpu.VMEM((M_HEADS * m, w), BF16), pltpu.VMEM((M_HEADS * m, w), BF16)],
        compiler_params=_params(("arbitrary", "arbitrary")),
        name="mem_attn_wide",
    )(mq, mk, mv)


def _mem_attention(mq, mk, mv, tm, interleaved=False):
    b, t, w = mq.shape
    m, kw = mk.shape[1], mk.shape[2]
    return pl.pallas_call(
        functools.partial(_mem_attn_kernel, interleaved=interleaved),
        grid=(b, t // tm),
        in_specs=[pl.BlockSpec((None, tm, w), lambda bi, i: (bi, i, 0)),
                  pl.BlockSpec((None, m, kw), lambda bi, i: (bi, 0, 0)),
                  pl.BlockSpec((None, m, kw), lambda bi, i: (bi, 0, 0))],
        out_specs=pl.BlockSpec((None, tm, w), lambda bi, i: (bi, i, 0)),
        out_shape=jax.ShapeDtypeStruct((b, t, w), F32),
        compiler_params=_params(("parallel", "parallel")),
        name="mem_attn",
    )(mq, mk, mv)


def _merge_ffn_kernel(x_ref, a_ref, g_ref, m_ref, gates_ref, wa_ref, wg_ref, wm_ref, wo_ref, nf_ref,
                      win_ref, wout_ref, y_ref, *, d_ff, ff_chunk):
    d = x_ref.shape[1]
    gates = gates_ref[...]
    h = (gates[:, :d] * jnp.dot(a_ref[...].astype(BF16), wa_ref[...], preferred_element_type=F32)
         + gates[:, d:2 * d] * jnp.dot(g_ref[...].astype(BF16), wg_ref[...], preferred_element_type=F32)
         + gates[:, 2 * d:] * jnp.dot(m_ref[...].astype(BF16), wm_ref[...], preferred_element_type=F32))
    x1 = x_ref[...] + jnp.dot(h.astype(BF16), wo_ref[...], preferred_element_type=F32)
    xn = _rms_rows(x1, nf_ref[...]).astype(BF16)
    acc = jnp.zeros_like(x1)
    for c in range(d_ff // ff_chunk):
        lo = c * ff_chunk
        gate = jnp.dot(xn, win_ref[:, lo:lo + ff_chunk], preferred_element_type=F32)
        up = jnp.dot(xn, win_ref[:, d_ff + lo:d_ff + lo + ff_chunk], preferred_element_type=F32)
        acc = acc + jnp.dot((_silu(gate) * up).astype(BF16), wout_ref[lo:lo + ff_chunk, :],
                            preferred_element_type=F32)
    y_ref[...] = x1 + acc


def _merge_ffn(x2d, a_out, g_out, m_out, gates, w_a, w_g, w_m, w_o, norm_ffn, w_in, w_out, tm):
    n, d = x2d.shape
    d_ff = w_out.shape[0]
    ff_chunk = 2 * LANES
    assert d_ff % ff_chunk == 0
    kern = functools.partial(_merge_ffn_kernel, d_ff=d_ff, ff_chunk=ff_chunk)
    row = lambda w: pl.BlockSpec((tm, w), lambda i: (i, 0))
    return pl.pallas_call(
        kern,
        grid=(n // tm,),
        in_specs=[row(d), row(a_out.shape[1]), row(g_out.shape[1]), row(m_out.shape[1]), row(3 * d),
                  _const_spec(w_a.shape), _const_spec(w_g.shape), _const_spec(w_m.shape), _const_spec(w_o.shape),
                  _const_spec((1, d)), _const_spec(w_in.shape), _const_spec(w_out.shape)],
        out_specs=row(d),
        out_shape=jax.ShapeDtypeStruct((n, d), F32),
        compiler_params=_params(("parallel",)),
        name="merge_ffn",
    )(x2d, a_out, g_out, m_out, gates, w_a, w_g, w_m, w_o, norm_ffn.reshape(1, d), w_in, w_out)


def _tile(n, pref):
    t = min(n, pref)
    assert n % t == 0
    return t


def kernel(x_prompt, x_sample, mem_prompt, cache_k, cache_v, cache_idx_k, page_table, state_gdn, state_conv,
           cache_mem_k, cache_mem_v, norm_mix, w_in, a_q_norm, a_k_norm, g_conv, g_a_log, g_dt_bias, g_o_norm,
           mem_norm, w_mem_kv, m_q_norm, m_k_norm, w_a_out, w_g_out, w_m_out, w_o, norm_ffn, w_ffn_in, w_ffn_out):
    depth = w_in.shape[0]
    b, s, d = x_prompt.shape
    db, t, _ = x_sample.shape
    n_mem = mem_prompt.shape[1]
    kvw = A_KV_HEADS * A_DH
    cch = g_conv.shape[2]
    yp, ys = x_prompt, x_sample
    p_states, s_states = [], []
    for l in range(depth):
        bf = lambda w: w.astype(BF16)
        proj_w = (norm_mix[l], _pack_w_in(w_in[l]), _pack_w_in_t(w_in[l]), a_q_norm[l], a_k_norm[l], m_q_norm[l])
        ffn_w = (bf(w_a_out[l]), bf(w_g_out[l]), bf(w_m_out[l]), bf(w_o[l]), norm_ffn[l], bf(w_ffn_in[l]),
                 bf(w_ffn_out[l]))
        gdn_w = (g_conv[l], g_a_log[l], g_dt_bias[l], g_o_norm[l])

        x2 = yp.reshape(b * s, d)
        tq = _tile(s, 256)
        aqt, iqt, iwt, avt, ak, misc, gqkv, gz, mq, gates = _inproj(x2, *proj_w, tm=tq, seq=s)
        r3 = lambda a: a.reshape(b, s, a.shape[-1])
        a_out = _prompt_attention(aqt, iqt, iwt, r3(ak), avt, r3(misc), tq=tq)
        av = jnp.transpose(avt.reshape(b, A_KV_HEADS, A_DH, s), (0, 3, 1, 2))
        g_out, p_gdn, p_conv = _gdn(r3(gqkv), r3(gz), r3(misc),
                                    jnp.zeros((b, CONV_W - 1, cch), F32),
                                    jnp.zeros((b, G_HEADS, G_DK, G_DV), F32), *gdn_w, n_seq=b)
        mk, mv = _mem_kv(mem_prompt.reshape(b * n_mem, d), mem_norm[l], w_mem_kv[l], m_k_norm[l],
                         tm=_tile(b * n_mem, 256))
        m_out = _mem_attention_wide(r3(mq), mk.reshape(b, n_mem, -1), mv.reshape(b, n_mem, -1), tm=_tile(s, 512))
        y2 = _merge_ffn(x2, a_out.reshape(b * s, -1), g_out.reshape(b * s, -1), m_out.reshape(b * s, -1), gates,
                        *ffn_w, tm=_tile(b * s, 256))
        yp = y2.reshape(b, s, d)
        p_states.append((ak.reshape(b, s, A_KV_HEADS, A_DH), av,
                         misc[:, MISC_IK:MISC_IK + IDX_DH].reshape(b, s, IDX_DH), p_gdn, p_conv,
                         mk.reshape(b, n_mem, M_HEADS, M_DH), mv.reshape(b, n_mem, M_HEADS, M_DH)))

        xs2 = ys.reshape(db * t, d)
        aq, ak, av, iq, misc, gqkv, gz, mq, gates = _inproj(xs2, *proj_w, tm=_tile(db * t, 256))
        r3 = lambda a: a.reshape(db, t, a.shape[-1])
        n_phys, page = cache_idx_k.shape[1], cache_idx_k.shape[2]
        pool_t = lambda c: jnp.transpose(c, (0, 2, 3, 1)).reshape(n_phys, kvw, page)
        a_out = _sample_attention(r3(aq), r3(ak), r3(av), r3(iq), r3(misc), pool_t(cache_k[l]), pool_t(cache_v[l]),
                                  jnp.transpose(cache_idx_k[l], (0, 2, 1)), page_table)
        g_out, s_gdn, s_conv = _gdn(r3(gqkv), r3(gz), r3(misc), state_conv[l], state_gdn[l], *gdn_w,
                                    n_seq=_tile(db, 8))
        m_out = _mem_attention(r3(mq), cache_mem_k[l].reshape(db, n_mem * M_HEADS, M_DH),
                               cache_mem_v[l].reshape(db, n_mem * M_HEADS, M_DH), tm=t, interleaved=True)
        y2 = _merge_ffn(xs2, a_out.reshape(db * t, -1), g_out.reshape(db * t, -1), m_out.reshape(db * t, -1), gates,
                        *ffn_w, tm=_tile(db * t, 256))
        ys = y2.reshape(db, t, d)
        s_states.append((ak.reshape(db, t, A_KV_HEADS, A_DH), av.reshape(db, t, A_KV_HEADS, A_DH),
                         misc[:, MISC_IK:MISC_IK + IDX_DH].reshape(db, t, IDX_DH), s_gdn, s_conv))

    p_k, p_v, p_idx_k, p_gdn, p_conv, p_mem_k, p_mem_v = [jnp.stack(z) for z in zip(*p_states)]
    s_k, s_v, s_idx_k, s_gdn, s_conv = [jnp.stack(z) for z in zip(*s_states)]
    return (yp, ys, p_k, p_v, p_idx_k, p_gdn, p_conv, p_mem_k, p_mem_v, s_k, s_v, s_idx_k, s_gdn, s_conv)
```

```python
import functools

import numpy as np
import jax
import jax.numpy as jnp
from jax import lax
from jax.experimental import pallas as pl
from jax.experimental.pallas import tpu as pltpu

F32 = jnp.float32
BF16 = jnp.bfloat16
I32 = jnp.int32

A_HEADS = 8
A_KV_HEADS = 4
A_DH = 64
IDX_HEADS = 4
IDX_DH = 64
TOPK_MAX = 256
G_HEADS = 4
G_DK = 128
G_DV = 128
CONV_W = 4
G_CHUNK = 64
M_HEADS = 4
M_DH = 128
EPS = 1e-6

LANES = 128
VMEM_LIMIT = 56 * 1024 * 1024
INT_MIN = np.int32(-2 ** 31)

MISC_IK = 0
MISC_IW = IDX_DH
MISC_GB = MISC_IW + IDX_HEADS
MISC_GA = MISC_GB + G_HEADS

NN = (((1,), (0,)), ((), ()))
NT = (((1,), (1,)), ((), ()))
TN = (((0,), (0,)), ((), ()))


def _dg(a, b, dn=NN):
    return lax.dot_general(a, b, dn, preferred_element_type=F32)


def _dot1(a, b, dn=NN):
    return _dg(a.astype(BF16), b.astype(BF16), dn)


def _split2(x):
    hi = x.astype(BF16)
    return hi, (x - hi.astype(F32)).astype(BF16)


def _dot3(a, b, dn=NN):
    ah, al = _split2(a)
    bh, bl = _split2(b)
    return _dg(ah, bh, dn) + (_dg(ah, bl, dn) + _dg(al, bh, dn))


def _sigmoid(x):
    return 1.0 / (1.0 + jnp.exp(-x))


def _silu(x):
    return x * _sigmoid(x)


def _softplus(x):
    return jnp.maximum(x, 0.0) + jnp.log1p(jnp.exp(-jnp.abs(x)))


def _rms_rows(x, gain):
    ms = jnp.mean(x * x, axis=-1, keepdims=True)
    return x * lax.rsqrt(ms + EPS) * gain


def _headnorm_pairs(h, gain128):
    rows, width = h.shape
    lo_mask = lax.broadcasted_iota(I32, (rows, LANES), 1) < (LANES // 2)
    outs = []
    for c in range(width // LANES):
        s = h[:, c * LANES:(c + 1) * LANES]
        sq = s * s
        lo = jnp.sum(jnp.where(lo_mask, sq, 0.0), axis=-1, keepdims=True)
        hi = jnp.sum(jnp.where(lo_mask, 0.0, sq), axis=-1, keepdims=True)
        ms = jnp.where(lo_mask, lo, hi) * (2.0 / LANES)
        outs.append(s * lax.rsqrt(ms + EPS) * gain128)
    return outs


def _headnorm_full(h, gain128):
    outs = []
    for c in range(h.shape[1] // LANES):
        outs.append(_rms_rows(h[:, c * LANES:(c + 1) * LANES], gain128))
    return outs


def _const_spec(shape):
    nd = len(shape)
    return pl.BlockSpec(shape, lambda *_: (0,) * nd, pipeline_mode=pl.Buffered(1))


def _params(sem):
    return pltpu.CompilerParams(dimension_semantics=sem, vmem_limit_bytes=VMEM_LIMIT)


_P_AQ = (0, 512)
_P_AK = (512, 768)
_P_AV = (768, 1024)
_P_IQ = (1024, 1280)
_P_MISC = (1280, 1408)
_P_GQKV = (1408, 2944)
_P_GZ = (2944, 3456)
_P_MQ = (3456, 3968)
_P_GATES = (3968, 7040)
_P_WIDTH = 7040


def _pack_w_in(w_in):
    d = w_in.shape[0]
    sizes = (A_HEADS * A_DH, A_KV_HEADS * A_DH, A_KV_HEADS * A_DH, IDX_HEADS * IDX_DH, IDX_DH, IDX_HEADS,
             2 * G_HEADS * G_DK + G_HEADS * G_DV, G_HEADS * G_DV, G_HEADS, G_HEADS, M_HEADS * M_DH)
    offs = np.cumsum((0,) + sizes)
    aq, ak, av, iq, ik, iw, gqkv, gz, gb, ga, mq = (w_in[:, offs[i]:offs[i + 1]] for i in range(len(sizes)))
    gates = w_in[:, offs[-1]:]
    pad = jnp.zeros((d, LANES - (IDX_DH + IDX_HEADS + 2 * G_HEADS)), w_in.dtype)
    misc = jnp.concatenate([ik, iw, gb, ga, pad], axis=1)
    packed = jnp.concatenate([aq, ak, av, iq, misc, gqkv, gz, mq, gates], axis=1)
    assert packed.shape[1] == _P_WIDTH
    return packed.astype(BF16)


_T_AQ = (0, 512)
_T_IQ = (512, 768)
_T_IW = (768, 776)
_T_AV = (776, 1032)
_T_ROWS = 1032


def _pack_w_in_t(w_in):
    sizes = (A_HEADS * A_DH, A_KV_HEADS * A_DH, A_KV_HEADS * A_DH, IDX_HEADS * IDX_DH, IDX_DH, IDX_HEADS)
    offs = np.cumsum((0,) + sizes)
    aq, _, av, iq, _, iw = (w_in[:, offs[i]:offs[i + 1]] for i in range(len(sizes)))
    pad = jnp.zeros((w_in.shape[0], _T_IW[1] - _T_IW[0] - IDX_HEADS), w_in.dtype)
    packed = jnp.concatenate([aq, iq, iw, pad, av], axis=1).T
    assert packed.shape[0] == _T_ROWS
    return packed.astype(BF16)


def _inproj_common(xn, w_ref, akg_ref, mqg_ref, misc_scale_ref, ak_ref, misc_ref, gqkv_ref, gz_ref, mq_ref, gates_ref):
    def proj(rng):
        return jnp.dot(xn, w_ref[:, rng[0]:rng[1]], preferred_element_type=F32)

    for c, y in enumerate(_headnorm_pairs(proj(_P_AK), akg_ref[...])):
        ak_ref[:, c * LANES:(c + 1) * LANES] = y
    misc_ref[...] = proj(_P_MISC) * misc_scale_ref[...]
    gqkv_ref[...] = proj(_P_GQKV)
    gz_ref[...] = proj(_P_GZ)
    for c, y in enumerate(_headnorm_full(proj(_P_MQ), mqg_ref[...])):
        mq_ref[:, c * LANES:(c + 1) * LANES] = y
    gates_ref[...] = _sigmoid(proj(_P_GATES))
    return proj


def _inproj_kernel(x_ref, gain_ref, w_ref, aqg_ref, akg_ref, mqg_ref, misc_scale_ref,
                   aq_ref, ak_ref, av_ref, iq_ref, misc_ref, gqkv_ref, gz_ref, mq_ref, gates_ref):
    xn = _rms_rows(x_ref[...], gain_ref[...]).astype(BF16)
    proj = _inproj_common(xn, w_ref, akg_ref, mqg_ref, misc_scale_ref, ak_ref, misc_ref, gqkv_ref, gz_ref, mq_ref,
                          gates_ref)
    for c, y in enumerate(_headnorm_pairs(proj(_P_AQ), aqg_ref[...])):
        aq_ref[:, c * LANES:(c + 1) * LANES] = y
    av_ref[...] = proj(_P_AV)
    iq_ref[...] = proj(_P_IQ)


def _inproj_t_kernel(x_ref, gain_ref, w_ref, wt_ref, aqg_ref, akg_ref, mqg_ref, misc_scale_ref,
                     aqt_ref, iqt_ref, iwt_ref, avt_ref, ak_ref, misc_ref, gqkv_ref, gz_ref, mq_ref, gates_ref):
    xn = _rms_rows(x_ref[...], gain_ref[...]).astype(BF16)
    _inproj_common(xn, w_ref, akg_ref, mqg_ref, misc_scale_ref, ak_ref, misc_ref, gqkv_ref, gz_ref, mq_ref, gates_ref)

    def proj_t(rng):
        return _dg(wt_ref[rng[0]:rng[1], :], xn, NT)

    aqt = proj_t(_T_AQ)
    gain_col = aqg_ref[...]
    for h in range(A_HEADS):
        qh = aqt[h * A_DH:(h + 1) * A_DH, :]
        ms = jnp.mean(qh * qh, axis=0, keepdims=True)
        aqt_ref[h * A_DH:(h + 1) * A_DH, :] = qh * lax.rsqrt(ms + EPS) * gain_col
    iqt_ref[...] = proj_t(_T_IQ)
    iwt_ref[...] = proj_t(_T_IW) * (IDX_HEADS ** -0.5)
    avt_ref[...] = proj_t(_T_AV)


def _inproj(x2d, norm_mix, w_packed, w_packed_t, a_q_norm, a_k_norm, m_q_norm, tm, seq=None):
    n, d = x2d.shape
    transposed = seq is not None
    lane = np.arange(LANES)
    misc_scale = np.where((lane >= MISC_IW) & (lane < MISC_GB), IDX_HEADS ** -0.5, 1.0).astype(np.float32)[None]
    akg = jnp.tile(a_k_norm.reshape(1, A_DH), (1, 2))
    mqg = m_q_norm.reshape(1, M_DH)
    common = [r[1] - r[0] for r in (_P_MISC, _P_GQKV, _P_GZ, _P_MQ, _P_GATES)]
    row_spec = lambda w: pl.BlockSpec((tm, w), lambda i: (i, 0))
    row_shape = lambda w: jax.ShapeDtypeStruct((n, w), F32)
    x_specs = [row_spec(d), _const_spec((1, d)), _const_spec((d, _P_WIDTH))]
    g_specs = [_const_spec((1, LANES)), _const_spec((1, LANES)), _const_spec((1, LANES))]
    if not transposed:
        widths = [512, 256, 256, 256] + common
        return pl.pallas_call(
            _inproj_kernel,
            grid=(n // tm,),
            in_specs=x_specs + [_const_spec((1, LANES))] + g_specs,
            out_specs=[row_spec(w) for w in widths],
            out_shape=[row_shape(w) for w in widths],
            compiler_params=_params(("parallel",)),
            name="inproj",
        )(x2d, norm_mix.reshape(1, d), w_packed, jnp.tile(a_q_norm.reshape(1, A_DH), (1, 2)), akg, mqg,
          jnp.asarray(misc_scale))
    t_rows = [r[1] - r[0] for r in (_T_AQ, _T_IQ, _T_IW, _T_AV)]
    widths = [256] + common
    tiles = seq // tm
    col_spec = lambda r: pl.BlockSpec((None, r, tm), lambda i: (i // tiles, 0, i % tiles))
    return pl.pallas_call(
        _inproj_t_kernel,
        grid=(n // tm,),
        in_specs=x_specs + [_const_spec((_T_ROWS, d)), _const_spec((A_DH, 1))] + g_specs,
        out_specs=[col_spec(r) for r in t_rows] + [row_spec(w) for w in widths],
        out_shape=[jax.ShapeDtypeStruct((n // seq, r, seq), F32) for r in t_rows] + [row_shape(w) for w in widths],
        compiler_params=_params(("parallel",)),
        name="inproj_t",
    )(x2d, norm_mix.reshape(1, d), w_packed, w_packed_t, a_q_norm.reshape(A_DH, 1), akg, mqg, jnp.asarray(misc_scale))


KEY_NEG_INF = np.int32(-2 ** 31 + 0x7FFFFF)


def _canon_zero(score):
    return jnp.where(score == 0.0, 0.0, score)


def _key_to_float(key):
    key = jnp.maximum(key, KEY_NEG_INF)
    return lax.bitcast_convert_type(key ^ ((key >> 31) & np.int32(0x7FFFFFFF)), F32)


def _as_i32(v):
    return np.int32(v - (1 << 32) if v >= (1 << 31) else v)


def _radix_select(count_fn, shape, bcast, k_top, idx_bits, bits_per_pass):
    thr_f, need, n_eq = _radix_threshold(count_fn, shape, bcast, k_top, bits_per_pass)
    del n_eq
    return thr_f, _tie_cut(count_fn, shape, bcast, thr_f, need, idx_bits, bits_per_pass)


def _radix_threshold(count_fn, shape, bcast, k_top, bits_per_pass):
    thr = jnp.full(shape, INT_MIN, I32)
    for hi in range(32, 0, -bits_per_pass):
        lo = max(hi - bits_per_pass, 0)
        cands = [bcast(_key_to_float(thr ^ _as_i32(v << lo))) for v in range(1, 2 ** (hi - lo))]
        cnts = count_fn([lambda s, c, cf=cf: s >= cf for cf in cands])
        digit = sum(jnp.where(cnt >= k_top, 1, 0) for cnt in cnts)
        thr = thr ^ (digit << lo)
    thr_f = _key_to_float(thr)
    thr_b = bcast(thr_f)
    n_gt, n_eq = count_fn([lambda s, c: s > thr_b, lambda s, c: s == thr_b])
    return thr_f, k_top - n_gt, n_eq


def _tie_cut(count_fn, shape, bcast, thr_f, need, idx_bits, bits_per_pass):
    thr_b = bcast(thr_f)
    cut = jnp.zeros(shape, I32)
    for hi in range(idx_bits, 0, -bits_per_pass):
        lo = max(hi - bits_per_pass, 0)
        cands = [bcast(cut | np.int32(v << lo)) for v in range(1, 2 ** (hi - lo))]
        cnts = count_fn([lambda s, c, cb=cb: (s == thr_b) & (c < cb) for cb in cands])
        digit = sum(jnp.where(cnt < need, 1, 0) for cnt in cnts)
        cut = cut | (digit << lo)
    return cut


def _prompt_attn_kernel(aqt_ref, iqt_ref, iwt_ref, k_ref, vt_ref, misc_ref, out_ref,
                        score_ref, kb_ref, ki_ref, vth_ref, qblk_ref, acc_ref, cut_ref, *, tq, k_top, idx_bits):
    j = pl.program_id(1)
    nk = j + 1
    ck = tq
    n_chunks = score_ref.shape[0]
    sub = 8
    group = A_HEADS // A_KV_HEADS
    head_rows = [slice(h * A_DH, (h + 1) * A_DH) for h in range(A_HEADS)]
    head_cols = [slice(h * tq, (h + 1) * tq) for h in range(A_HEADS)]
    kv_rows = [slice((h // group) * A_DH, (h // group + 1) * A_DH) for h in range(A_HEADS)]

    @pl.when(j == 0)
    def _():
        for c in range(n_chunks):
            rows = slice(c * ck, (c + 1) * ck)
            kb_ref[rows, :] = k_ref[rows, :].astype(BF16)
            ki_ref[rows, :] = misc_ref[rows, MISC_IK:MISC_IK + IDX_DH].astype(BF16)
            vth_ref[c] = vt_ref[:, rows].astype(BF16)
        qblk_ref[...] = jnp.zeros(qblk_ref.shape, BF16)

    key_in_chunk = lax.broadcasted_iota(I32, (ck, tq), 0)
    q_pos = j * tq + lax.broadcasted_iota(I32, (ck, tq), 1)

    iqb = (iqt_ref[...] * (IDX_DH ** -0.5)).astype(BF16)
    iq_cat = jnp.concatenate([iqb[h * IDX_DH:(h + 1) * IDX_DH, :] for h in range(IDX_HEADS)], axis=1)
    iw = iwt_ref[...]

    def idx_body(c, carry):
        start = pl.multiple_of(c * ck, ck)
        dots = _dg(ki_ref[pl.ds(start, ck), :], iq_cat)
        acc = jnp.zeros((ck, tq), F32)
        for h in range(IDX_HEADS):
            acc = acc + jnp.maximum(dots[:, h * tq:(h + 1) * tq], 0.0) * iw[h:h + 1, :]
        score = jnp.where(c * ck + key_in_chunk <= q_pos, acc, -jnp.inf)
        score_ref[c] = _canon_zero(score)
        return carry

    step = 2 if n_chunks % 2 == 0 else 1
    n_iter = (nk + step - 1) // step
    lax.fori_loop(0, n_iter * step, idx_body, 0)

    key_in_group = lax.broadcasted_iota(I32, (sub, tq), 0)
    n_acc = 4

    def count_fn(preds):
        def body(it, cnts):
            cnts = [list(a) for a in cnts]
            for u in range(step):
                c = it * step + u
                for i in range(ck // sub):
                    sc = score_ref[c, i * sub:(i + 1) * sub, :]
                    key_id = c * ck + i * sub + key_in_group
                    for n, pred in enumerate(preds):
                        cnts[n][i % n_acc] = cnts[n][i % n_acc] + jnp.where(pred(sc, key_id), 1.0, 0.0)
            return tuple(tuple(a) for a in cnts)
        zero = tuple(jnp.zeros((sub, tq), F32) for _ in range(n_acc))
        cnts = lax.fori_loop(0, n_iter, body, tuple(zero for _ in preds))
        return [jnp.sum(sum(a), axis=0, keepdims=True) for a in cnts]

    bcast = lambda v: jnp.broadcast_to(v, (sub, tq))
    thr, need, n_eq = _radix_threshold(count_fn, (1, tq), bcast, k_top, 1)
    cut_ref[...] = jnp.full((1, tq), 2 ** idx_bits - 1, I32)

    @pl.when(jnp.max(n_eq - need) > 0.5)
    def _():
        cut_ref[...] = _tie_cut(count_fn, (1, tq), bcast, thr, need, idx_bits, 1)

    cut = cut_ref[...]

    def mask_body(c, carry):
        sc = score_ref[c]
        key_id = c * ck + key_in_chunk
        sel = ((sc > thr) | ((sc == thr) & (key_id <= cut))) & (key_id <= q_pos)
        score_ref[c] = jnp.where(sel, 0.0, -jnp.inf)
        return carry

    lax.fori_loop(0, nk, mask_body, 0)

    qb = (aqt_ref[...] * (A_DH ** -0.5)).astype(BF16)
    for h in range(A_HEADS):
        qblk_ref[kv_rows[h], head_cols[h]] = qb[head_rows[h], :]
    acc_ref[...] = jnp.zeros(acc_ref.shape, F32)

    def chunk_scores(c):
        start = pl.multiple_of(c * ck, ck)
        return _dg(kb_ref[pl.ds(start, ck), :], qblk_ref[...])

    def fold(x, acc, op):
        for i in range(ck // sub):
            acc = op(acc, x[i * sub:(i + 1) * sub, :])
        return acc

    def att_body(c, carry):
        ms, lparts = carry
        s_all = chunk_scores(c)
        bias = score_ref[c]
        scores = [s_all[:, head_cols[h]] + bias for h in range(A_HEADS)]
        m_new = [jnp.maximum(ms[h], jnp.max(fold(scores[h], jnp.full((sub, tq), -1e30, F32), jnp.maximum),
                                            axis=0, keepdims=True)) for h in range(A_HEADS)]
        alphas = [jnp.exp(ms[h] - m_new[h]) for h in range(A_HEADS)]
        probs, new_l = [], []
        for h in range(A_HEADS):
            p = jnp.exp(scores[h] - m_new[h])
            new_l.append(fold(p, alphas[h] * lparts[h], jnp.add))
            probs.append(p.astype(BF16))
        pv = _dg(vth_ref[c], jnp.concatenate(probs, axis=1))
        for h in range(A_HEADS):
            acc_ref[head_rows[h], :] = alphas[h] * acc_ref[head_rows[h], :] + pv[kv_rows[h], head_cols[h]]
        return tuple(m_new), tuple(new_l)

    init = (tuple(jnp.full((1, tq), -1e30, F32) for _ in range(A_HEADS)),
            tuple(jnp.zeros((sub, tq), F32) for _ in range(A_HEADS)))
    _, lparts = lax.fori_loop(0, nk, att_body, init)
    for h in range(A_HEADS):
        acc_ref[head_rows[h], :] = acc_ref[head_rows[h], :] / jnp.sum(lparts[h], axis=0, keepdims=True)
    out_ref[...] = acc_ref[...].T


def _prompt_attention(aqt, iqt, iwt, ak, avt, misc, tq):
    b, s, kvw = ak.shape
    nq = s // tq
    k_top = min(TOPK_MAX, s // 4)
    idx_bits = max(1, int(np.ceil(np.log2(s))))
    kern = functools.partial(_prompt_attn_kernel, tq=tq, k_top=k_top, idx_bits=idx_bits)
    qspec = lambda r: pl.BlockSpec((None, r, tq), lambda bi, j: (bi, 0, j))
    kspec = lambda w: pl.BlockSpec((None, s, w), lambda bi, j: (bi, 0, 0))
    return pl.pallas_call(
        kern,
        grid=(b, nq),
        in_specs=[qspec(A_HEADS * A_DH), qspec(IDX_HEADS * IDX_DH), qspec(_T_IW[1] - _T_IW[0]),
                  kspec(kvw), pl.BlockSpec((None, kvw, s), lambda bi, j: (bi, 0, 0)), kspec(LANES)],
        out_specs=pl.BlockSpec((None, tq, A_HEADS * A_DH), lambda bi, j: (bi, j, 0)),
        out_shape=jax.ShapeDtypeStruct((b, s, A_HEADS * A_DH), F32),
        scratch_shapes=[pltpu.VMEM((nq, tq, tq), F32),
                        pltpu.VMEM((s, kvw), BF16), pltpu.VMEM((s, IDX_DH), BF16),
                        pltpu.VMEM((nq, kvw, tq), BF16), pltpu.VMEM((kvw, A_HEADS * tq), BF16),
                        pltpu.VMEM((A_HEADS * A_DH, tq), F32), pltpu.VMEM((1, tq), I32)],
        compiler_params=_params(("arbitrary", "arbitrary")),
        name="prompt_attn",
    )(aqt, iqt, iwt, ak, avt, misc)


def _sample_attn_kernel(pt_ref, aq_ref, iq_ref, misc_ref, kn_ref, vn_ref, *refs, n_seq, n_pages, page, k_top, idx_bits):
    npg = n_seq * n_pages
    ki_refs = [refs[u * n_pages:(u + 1) * n_pages] for u in range(n_seq)]
    k_refs = [refs[npg + u * n_pages:npg + (u + 1) * n_pages] for u in range(n_seq)]
    v_refs = [refs[2 * npg + u * n_pages:2 * npg + (u + 1) * n_pages] for u in range(n_seq)]
    out_ref = refs[3 * npg]
    cut_ref = refs[3 * npg + 1]
    t = aq_ref.shape[1]
    rows = n_seq * t
    assert page == LANES and t <= LANES
    tok = jnp.concatenate([lax.broadcasted_iota(I32, (t, LANES), 0)] * n_seq, axis=0)
    lane_ids = lax.broadcasted_iota(I32, (rows, LANES), 1)
    new_ok = lane_ids <= tok
    group = A_HEADS // A_KV_HEADS

    page_scores = [[] for _ in range(n_pages + 1)]
    for u in range(n_seq):
        iqb = (iq_ref[u] * (IDX_DH ** -0.5)).astype(BF16)
        q4 = jnp.concatenate([iqb[:, h * IDX_DH:(h + 1) * IDX_DH] for h in range(IDX_HEADS)], axis=0)
        iw = misc_ref[u][:, MISC_IW:MISC_IW + IDX_HEADS]
        new_rows = jnp.concatenate([misc_ref[u], jnp.zeros((LANES - t, LANES), F32)], axis=0)
        dots = [_dg(q4, ki_refs[u][p][...].astype(BF16)) for p in range(n_pages)]
        dots.append(_dg(q4, new_rows[:, MISC_IK:MISC_IK + IDX_DH].astype(BF16), NT))
        for p, d in enumerate(dots):
            acc = jnp.zeros((t, LANES), F32)
            for h in range(IDX_HEADS):
                acc = acc + jnp.maximum(d[h * t:(h + 1) * t, :], 0.0) * iw[:, h:h + 1]
            page_scores[p].append(acc)
    keys = [_canon_zero(jnp.concatenate(ps, axis=0)) for ps in page_scores]
    keys[n_pages] = jnp.where(new_ok, keys[n_pages], -jnp.inf)

    def count_fn(preds):
        cnts = []
        for pred in preds:
            cnt = jnp.zeros((rows, LANES), F32)
            for p, kc in enumerate(keys):
                cnt = cnt + jnp.where(pred(kc, p * LANES + lane_ids), 1.0, 0.0)
            cnts.append(jnp.sum(cnt, axis=-1, keepdims=True))
        return cnts

    bcast = lambda v: jnp.broadcast_to(v, (rows, LANES))
    thr, need, n_eq = _radix_threshold(count_fn, (rows, 1), bcast, k_top, 4)
    cut_ref[...] = jnp.full((rows, 1), 2 ** idx_bits - 1, I32)

    @pl.when(jnp.max(n_eq - need) > 0.5)
    def _():
        cut_ref[...] = _tie_cut(count_fn, (rows, 1), bcast, thr, need, idx_bits, 4)

    cut = cut_ref[...]
    biases = []
    for p in range(n_pages + 1):
        cols = p * LANES + lane_ids
        sel = (keys[p] > thr) | ((keys[p] == thr) & (cols <= cut))
        if p == n_pages:
            sel = sel & new_ok
        biases.append(jnp.where(sel, 0.0, -jnp.inf))

    chs = [slice(g * A_DH, (g + 1) * A_DH) for g in range(A_KV_HEADS)]
    units = [(u, g) for u in range(n_seq) for g in range(A_KV_HEADS)]
    pad_kv = jnp.zeros((LANES - t, A_KV_HEADS * A_DH), F32)
    k_new = [jnp.concatenate([kn_ref[u], pad_kv], axis=0).astype(BF16) for u in range(n_seq)]
    v_new = [jnp.concatenate([vn_ref[u], pad_kv], axis=0).astype(BF16) for u in range(n_seq)]
    scores = []
    for u, g in units:
        qb = (aq_ref[u] * (A_DH ** -0.5)).astype(BF16)
        q_stack = jnp.concatenate([qb[:, (g * group + r) * A_DH:(g * group + r + 1) * A_DH] for r in range(group)], axis=0)
        sg = [_dg(q_stack, k_refs[u][p][chs[g], :].astype(BF16)) for p in range(n_pages)]
        sg.append(_dg(q_stack, k_new[u][:, chs[g]], NT))
        scores.append(sg)
    probs, dens = [], []
    for i, (u, g) in enumerate(units):
        sg = [s + jnp.concatenate([biases[p][u * t:(u + 1) * t, :]] * group, axis=0) for p, s in enumerate(scores[i])]
        mx = sg[0]
        for s in sg[1:]:
            mx = jnp.maximum(mx, s)
        mx = jnp.max(mx, axis=-1, keepdims=True)
        es = [jnp.exp(s - mx) for s in sg]
        dens.append(jnp.sum(sum(es), axis=-1, keepdims=True))
        probs.append([e.astype(BF16) for e in es])
    for i, (u, g) in enumerate(units):
        pvs = [_dg(probs[i][p], v_refs[u][p][chs[g], :].astype(BF16), NT) for p in range(n_pages)]
        pvs.append(_dg(probs[i][n_pages], v_new[u][:, chs[g]]))
        o = sum(pvs) / dens[i]
        for r in range(group):
            h = g * group + r
            out_ref[u, :, h * A_DH:(h + 1) * A_DH] = o[r * t:(r + 1) * t, :]


def _sample_attention(aq, ak, av, iq, misc, pool_kt, pool_vt, pool_kit, page_table, n_seq):
    db, t, _ = aq.shape
    assert db % n_seq == 0
    n_pages = page_table.shape[1]
    page = pool_kit.shape[2]
    total = n_pages * page + t
    k_top = min(TOPK_MAX, total // 4)
    idx_bits = max(1, int(np.ceil(np.log2((n_pages + 1) * LANES))))
    kern = functools.partial(_sample_attn_kernel, n_seq=n_seq, n_pages=n_pages, page=page, k_top=k_top,
                             idx_bits=idx_bits)
    qspec = lambda w: pl.BlockSpec((n_seq, t, w), lambda i, pt: (i, 0, 0))
    pspec = lambda w, u, p: pl.BlockSpec((None, w, page), lambda i, pt: (pt[i * n_seq + u, p], 0, 0))
    kvw = A_KV_HEADS * A_DH
    pages = [(u, p) for u in range(n_seq) for p in range(n_pages)]
    in_specs = [qspec(A_HEADS * A_DH), qspec(IDX_HEADS * IDX_DH), qspec(LANES), qspec(kvw), qspec(kvw)]
    in_specs += [pspec(IDX_DH, u, p) for u, p in pages]
    in_specs += [pspec(kvw, u, p) for u, p in pages]
    in_specs += [pspec(kvw, u, p) for u, p in pages]
    return pl.pallas_call(
        kern,
        grid_spec=pltpu.PrefetchScalarGridSpec(
            num_scalar_prefetch=1, grid=(db // n_seq,), in_specs=in_specs, out_specs=qspec(A_HEADS * A_DH),
            scratch_shapes=[pltpu.VMEM((n_seq * t, 1), I32)]),
        out_shape=jax.ShapeDtypeStruct((db, t, A_HEADS * A_DH), F32),
        compiler_params=_params(("arbitrary",)),
        name="sample_attn",
    )(page_table, aq, iq, misc, ak, av, *([pool_kit] * len(pages)), *([pool_kt] * len(pages)),
      *([pool_vt] * len(pages)))


def _gdn_kernel(u_ref, gz_ref, misc_ref, conv0_ref, s0_ref, convw_ref, alog_ref, dtb_ref, onorm_ref,
                out_ref, s_out_ref, conv_out_ref, state_ref, ubuf_ref, *, chunk, n_seq):
    c = pl.program_id(1)
    n_c = pl.num_programs(1)
    hist = CONV_W - 1
    base = 8

    @pl.when(c == 0)
    def _():
        state_ref[...] = s0_ref[...]
        ubuf_ref[:, base - hist:base, :] = conv0_ref[...]

    ii = lax.broadcasted_iota(I32, (chunk, chunk), 0)
    jj = lax.broadcasted_iota(I32, (chunk, chunk), 1)
    lower = jnp.where(ii >= jj, 1.0, 0.0).astype(BF16)
    upper = jnp.where(ii <= jj, 1.0, 0.0).astype(BF16)
    lane = lax.broadcasted_iota(I32, (chunk, LANES), 1)
    ga_lanes = (lane >= MISC_GA) & (lane < MISC_GA + G_HEADS)
    qk = G_HEADS * G_DK
    n_dbl = int(np.log2(chunk))
    assert 2 ** n_dbl == chunk

    per_seq = []
    for b in range(n_seq):
        ubuf_ref[b, base:base + chunk, :] = u_ref[b]
        conv = 0.0
        for jt in range(CONV_W):
            conv = conv + ubuf_ref[b, base - hist + jt:base - hist + jt + chunk, :] * convw_ref[jt:jt + 1, :]
        act = _silu(conv)
        tail = ubuf_ref[b, base + chunk - hist:base + chunk, :]
        ubuf_ref[b, base - hist:base, :] = tail
        conv_out_ref[b] = tail

        misc = misc_ref[b]
        beta_all = _sigmoid(misc)
        g_all = jnp.where(ga_lanes, -jnp.exp(alog_ref[...]) * _softplus(misc + dtb_ref[...]), 0.0)
        g1 = g_all.astype(BF16)
        r1 = g_all - g1.astype(F32)
        g2 = r1.astype(BF16)
        g3 = (r1 - g2.astype(F32)).astype(BF16)
        gc_col = _dg(lower, g1) + _dg(lower, g2) + _dg(lower, g3)
        gc_row = _dg(g1, upper, TN) + _dg(g2, upper, TN) + _dg(g3, upper, TN)
        per_seq.append((act, beta_all, gc_col, gc_row))

    chains = [(b, h) for b in range(n_seq) for h in range(G_HEADS)]
    qs, ks, kbs, gammas, egs, gcols, sols = [], [], [], [], [], [], []
    for b, h in chains:
        act, beta_all, gc_col, gc_row = per_seq[b]
        q = act[:, h * G_DK:(h + 1) * G_DK]
        k = act[:, qk + h * G_DK:qk + (h + 1) * G_DK]
        v = act[:, 2 * qk + h * G_DV:2 * qk + (h + 1) * G_DV]
        q = q * lax.rsqrt(jnp.sum(q * q, axis=-1, keepdims=True) + EPS) * (G_DK ** -0.5)
        k = k * lax.rsqrt(jnp.sum(k * k, axis=-1, keepdims=True) + EPS)
        beta = beta_all[:, MISC_GB + h:MISC_GB + h + 1]
        gcol = gc_col[:, MISC_GA + h:MISC_GA + h + 1]
        grow = gc_row[MISC_GA + h:MISC_GA + h + 1, :]
        eg = jnp.exp(gcol)
        kb = k * beta
        qs.append(q)
        ks.append(k)
        kbs.append(kb)
        gammas.append(jnp.exp(jnp.where(ii >= jj, gcol - grow, -jnp.inf)))
        egs.append(eg)
        gcols.append(gcol)
        sols.append(jnp.concatenate([v * beta, kb * eg], axis=-1))
    n_ch = len(chains)
    kk = [_dot1(kbs[i], ks[i], NT) for i in range(n_ch)]
    pws = [jnp.where(ii > jj, -(kk[i] * gammas[i]), 0.0) for i in range(n_ch)]
    for it in range(n_dbl):
        upd = [_dot3(pws[i], sols[i]) for i in range(n_ch)]
        if it + 1 < n_dbl:
            pws = [_dot1(pws[i], pws[i]) for i in range(n_ch)]
        sols = [sols[i] + upd[i] for i in range(n_ch)]
    aqk = [_dot1(qs[i], ks[i], NT) for i in range(n_ch)]
    sts = [state_ref[b, h] for b, h in chains]
    ws = [_dot1(sols[i][:, G_DV:], sts[i]) for i in range(n_ch)]
    o1 = [_dot1(qs[i] * egs[i], sts[i]) for i in range(n_ch)]
    v_new = [sols[i][:, :G_DV] - ws[i] for i in range(n_ch)]
    o2 = [_dot1(aqk[i] * gammas[i], v_new[i]) for i in range(n_ch)]
    g_last = [gcols[i][chunk - 1:chunk, :] for i in range(n_ch)]
    kv = [_dot1(ks[i] * jnp.exp(g_last[i] - gcols[i]), v_new[i], TN) for i in range(n_ch)]
    for i, (b, h) in enumerate(chains):
        state_ref[b, h] = sts[i] * jnp.exp(g_last[i]) + kv[i]
        o = _rms_rows(o1[i] + o2[i], onorm_ref[...]) * _silu(gz_ref[b, :, h * G_DV:(h + 1) * G_DV])
        out_ref[b, :, h * G_DV:(h + 1) * G_DV] = o

    @pl.when(c == n_c - 1)
    def _():
        s_out_ref[...] = state_ref[...]


def _gdn(gqkv, gz, misc, conv0, s0, g_conv, g_a_log, g_dt_bias, g_o_norm, n_seq):
    b, t, cch = gqkv.shape
    chunk = min(G_CHUNK, t)
    assert t % chunk == 0 and chunk % 8 == 0 and b % n_seq == 0
    hist = CONV_W - 1
    alog = jnp.zeros((1, LANES), F32).at[0, MISC_GA:MISC_GA + G_HEADS].set(g_a_log.astype(F32))
    dtb = jnp.zeros((1, LANES), F32).at[0, MISC_GA:MISC_GA + G_HEADS].set(g_dt_bias.astype(F32))
    kern = functools.partial(_gdn_kernel, chunk=chunk, n_seq=n_seq)
    tspec = lambda w: pl.BlockSpec((n_seq, chunk, w), lambda bi, c: (bi, c, 0))
    sspec = pl.BlockSpec((n_seq, G_HEADS, G_DK, G_DV), lambda bi, c: (bi, 0, 0, 0))
    cspec = pl.BlockSpec((n_seq, hist, cch), lambda bi, c: (bi, 0, 0))
    return pl.pallas_call(
        kern,
        grid=(b // n_seq, t // chunk),
        in_specs=[tspec(cch), tspec(G_HEADS * G_DV), tspec(LANES), cspec, sspec,
                  _const_spec((CONV_W, cch)), _const_spec((1, LANES)), _const_spec((1, LANES)),
                  _const_spec((1, G_DV))],
        out_specs=[tspec(G_HEADS * G_DV), sspec, cspec],
        out_shape=[jax.ShapeDtypeStruct((b, t, G_HEADS * G_DV), F32),
                   jax.ShapeDtypeStruct((b, G_HEADS, G_DK, G_DV), F32),
                   jax.ShapeDtypeStruct((b, hist, cch), F32)],
        scratch_shapes=[pltpu.VMEM((n_seq, G_HEADS, G_DK, G_DV), F32), pltpu.VMEM((n_seq, 8 + chunk, cch), F32)],
        compiler_params=_params(("arbitrary", "arbitrary")),
        name="gdn",
    )(gqkv, gz, misc, conv0, s0, g_conv, alog, dtb, g_o_norm.reshape(1, G_DV))


def _mem_kv_kernel(mem_ref, gain_ref, w_ref, kg_ref, mk_ref, mv_ref):
    xn = _rms_rows(mem_ref[...], gain_ref[...]).astype(BF16)
    half = M_HEADS * M_DH
    kk = jnp.dot(xn, w_ref[:, :half], preferred_element_type=F32)
    for c, y in enumerate(_headnorm_full(kk, kg_ref[...])):
        mk_ref[:, c * LANES:(c + 1) * LANES] = y
    mv_ref[...] = jnp.dot(xn, w_ref[:, half:], preferred_element_type=F32)


def _mem_kv(mem2d, mem_norm, w_mem_kv, m_k_norm, tm):
    n, d = mem2d.shape
    half = M_HEADS * M_DH
    return pl.pallas_call(
        _mem_kv_kernel,
        grid=(n // tm,),
        in_specs=[pl.BlockSpec((tm, d), lambda i: (i, 0)), _const_spec((1, d)), _const_spec((d, 2 * half)),
                  _const_spec((1, M_DH))],
        out_specs=[pl.BlockSpec((tm, half), lambda i: (i, 0))] * 2,
        out_shape=[jax.ShapeDtypeStruct((n, half), F32)] * 2,
        compiler_params=_params(("parallel",)),
        name="mem_kv",
    )(mem2d, mem_norm.reshape(1, d), w_mem_kv.astype(BF16), m_k_norm.reshape(1, M_DH))


def _mem_attn_kernel(q_ref, k_ref, v_ref, out_ref, *, interleaved):
    if interleaved:
        t = q_ref.shape[0]
        n_rows = k_ref.shape[0]
        q_stack = jnp.concatenate([q_ref[:, h * M_DH:(h + 1) * M_DH] for h in range(M_HEADS)], axis=0).astype(BF16)
        s = _dg(q_stack, k_ref[...].astype(BF16), NT) * (M_DH ** -0.5)
        q_head = jnp.concatenate([jnp.full((t, n_rows), h, I32) for h in range(M_HEADS)], axis=0)
        row_id = lax.broadcasted_iota(I32, (M_HEADS * t, n_rows), 1)
        row_head = row_id & (M_HEADS - 1) if M_HEADS & (M_HEADS - 1) == 0 else row_id % M_HEADS
        s = jnp.where(q_head == row_head, s, -jnp.inf)
        e = jnp.exp(s - jnp.max(s, axis=-1, keepdims=True))
        p = e / jnp.sum(e, axis=-1, keepdims=True)
        o = _dg(p.astype(BF16), v_ref[...].astype(BF16))
        for h in range(M_HEADS):
            out_ref[:, h * M_DH:(h + 1) * M_DH] = o[h * t:(h + 1) * t, :]
        return
    for h in range(M_HEADS):
        sl = slice(h * M_DH, (h + 1) * M_DH)
        s = _dg(q_ref[:, sl].astype(BF16), k_ref[:, sl].astype(BF16), NT) * (M_DH ** -0.5)
        e = jnp.exp(s - jnp.max(s, axis=-1, keepdims=True))
        p = e / jnp.sum(e, axis=-1, keepdims=True)
        out_ref[:, sl] = _dg(p.astype(BF16), v_ref[:, sl].astype(BF16))


def _mem_attn_wide_kernel(q_ref, k_ref, v_ref, out_ref, kblk_ref, vblk_ref):
    n_mem = k_ref.shape[0]

    @pl.when(pl.program_id(1) == 0)
    def _():
        kblk_ref[...] = jnp.zeros(kblk_ref.shape, BF16)
        vblk_ref[...] = jnp.zeros(vblk_ref.shape, BF16)
        for h in range(M_HEADS):
            sl = slice(h * M_DH, (h + 1) * M_DH)
            kblk_ref[h * n_mem:(h + 1) * n_mem, sl] = k_ref[:, sl].astype(BF16)
            vblk_ref[h * n_mem:(h + 1) * n_mem, sl] = v_ref[:, sl].astype(BF16)

    s_all = _dg(q_ref[...].astype(BF16), kblk_ref[...], NT) * (M_DH ** -0.5)
    probs = []
    for h in range(M_HEADS):
        s = s_all[:, h * n_mem:(h + 1) * n_mem]
        e = jnp.exp(s - jnp.max(s, axis=-1, keepdims=True))
        probs.append((e / jnp.sum(e, axis=-1, keepdims=True)).astype(BF16))
    out_ref[...] = _dg(jnp.concatenate(probs, axis=1), vblk_ref[...])


def _mem_attention_wide(mq, mk, mv, tm):
    b, t, w = mq.shape
    m = mk.shape[1]
    return pl.pallas_call(
        _mem_attn_wide_kernel,
        grid=(b, t // tm),
        in_specs=[pl.BlockSpec((None, tm, w), lambda bi, i: (bi, i, 0)),
                  pl.BlockSpec((None, m, w), lambda bi, i: (bi, 0, 0)),
                  pl.BlockSpec((None, m, w), lambda bi, i: (bi, 0, 0))],
        out_specs=pl.BlockSpec((None, tm, w), lambda bi, i: (bi, i, 0)),
        out_shape=jax.ShapeDtypeStruct((b, t, w), F32),
        scratch_shapes=[pltpu.VMEM((M_HEADS * m, w), BF16), pltpu.VMEM((M_HEADS * m, w), BF16)],
        compiler_params=_params(("arbitrary", "arbitrary")),
        name="mem_attn_wide",
    )(mq, mk, mv)


def _mem_attention(mq, mk, mv, tm, interleaved=False):
    b, t, w = mq.shape
    m, kw = mk.shape[1], mk.shape[2]
    return pl.pallas_call(
        functools.partial(_mem_attn_kernel, interleaved=interleaved),
        grid=(b, t // tm),
        in_specs=[pl.BlockSpec((None, tm, w), lambda bi, i: (bi, i, 0)),
                  pl.BlockSpec((None, m, kw), lambda bi, i: (bi, 0, 0)),
                  pl.BlockSpec((None, m, kw), lambda bi, i: (bi, 0, 0))],
        out_specs=pl.BlockSpec((None, tm, w), lambda bi, i: (bi, i, 0)),
        out_shape=jax.ShapeDtypeStruct((b, t, w), F32),
        compiler_params=_params(("parallel", "parallel")),
        name="mem_attn",
    )(mq, mk, mv)


def _merge_ffn_kernel(x_ref, a_ref, g_ref, m_ref, gates_ref, wa_ref, wg_ref, wm_ref, wo_ref, nf_ref,
                      win_ref, wout_ref, y_ref, *, d_ff, ff_chunk):
    d = x_ref.shape[1]
    gates = gates_ref[...]
    h = (gates[:, :d] * jnp.dot(a_ref[...].astype(BF16), wa_ref[...], preferred_element_type=F32)
         + gates[:, d:2 * d] * jnp.dot(g_ref[...].astype(BF16), wg_ref[...], preferred_element_type=F32)
         + gates[:, 2 * d:] * jnp.dot(m_ref[...].astype(BF16), wm_ref[...], preferred_element_type=F32))
    x1 = x_ref[...] + jnp.dot(h.astype(BF16), wo_ref[...], preferred_element_type=F32)
    xn = _rms_rows(x1, nf_ref[...]).astype(BF16)
    acc = jnp.zeros_like(x1)
    for c in range(d_ff // ff_chunk):
        lo = c * ff_chunk
        gate = jnp.dot(xn, win_ref[:, lo:lo + ff_chunk], preferred_element_type=F32)
        up = jnp.dot(xn, win_ref[:, d_ff + lo:d_ff + lo + ff_chunk], preferred_element_type=F32)
        acc = acc + jnp.dot((_silu(gate) * up).astype(BF16), wout_ref[lo:lo + ff_chunk, :],
                            preferred_element_type=F32)
    y_ref[...] = x1 + acc


def _merge_ffn(x2d, a_out, g_out, m_out, gates, w_a, w_g, w_m, w_o, norm_ffn, w_in, w_out, tm):
    n, d = x2d.shape
    d_ff = w_out.shape[0]
    ff_chunk = 2 * LANES
    assert d_ff % ff_chunk == 0
    kern = functools.partial(_merge_ffn_kernel, d_ff=d_ff, ff_chunk=ff_chunk)
    row = lambda w: pl.BlockSpec((tm, w), lambda i: (i, 0))
    return pl.pallas_call(
        kern,
        grid=(n // tm,),
        in_specs=[row(d), row(a_out.shape[1]), row(g_out.shape[1]), row(m_out.shape[1]), row(3 * d),
                  _const_spec(w_a.shape), _const_spec(w_g.shape), _const_spec(w_m.shape), _const_spec(w_o.shape),
                  _const_spec((1, d)), _const_spec(w_in.shape), _const_spec(w_out.shape)],
        out_specs=row(d),
        out_shape=jax.ShapeDtypeStruct((n, d), F32),
        compiler_params=_params(("parallel",)),
        name="merge_ffn",
    )(x2d, a_out, g_out, m_out, gates, w_a, w_g, w_m, w_o, norm_ffn.reshape(1, d), w_in, w_out)


def _tile(n, pref):
    t = min(n, pref)
    assert n % t == 0
    return t


def kernel(x_prompt, x_sample, mem_prompt, cache_k, cache_v, cache_idx_k, page_table, state_gdn, state_conv,
           cache_mem_k, cache_mem_v, norm_mix, w_in, a_q_norm, a_k_norm, g_conv, g_a_log, g_dt_bias, g_o_norm,
           mem_norm, w_mem_kv, m_q_norm, m_k_norm, w_a_out, w_g_out, w_m_out, w_o, norm_ffn, w_ffn_in, w_ffn_out):
    depth = w_in.shape[0]
    b, s, d = x_prompt.shape
    db, t, _ = x_sample.shape
    n_mem = mem_prompt.shape[1]
    kvw = A_KV_HEADS * A_DH
    cch = g_conv.shape[2]
    yp, ys = x_prompt, x_sample
    p_states, s_states = [], []
    for l in range(depth):
        bf = lambda w: w.astype(BF16)
        proj_w = (norm_mix[l], _pack_w_in(w_in[l]), _pack_w_in_t(w_in[l]), a_q_norm[l], a_k_norm[l], m_q_norm[l])
        ffn_w = (bf(w_a_out[l]), bf(w_g_out[l]), bf(w_m_out[l]), bf(w_o[l]), norm_ffn[l], bf(w_ffn_in[l]),
                 bf(w_ffn_out[l]))
        gdn_w = (g_conv[l], g_a_log[l], g_dt_bias[l], g_o_norm[l])

        x2 = yp.reshape(b * s, d)
        tq = _tile(s, 256)
        aqt, iqt, iwt, avt, ak, misc, gqkv, gz, mq, gates = _inproj(x2, *proj_w, tm=tq, seq=s)
        r3 = lambda a: a.reshape(b, s, a.shape[-1])
        a_out = _prompt_attention(aqt, iqt, iwt, r3(ak), avt, r3(misc), tq=tq)
        av = jnp.transpose(avt.reshape(b, A_KV_HEADS, A_DH, s), (0, 3, 1, 2))
        g_out, p_gdn, p_conv = _gdn(r3(gqkv), r3(gz), r3(misc),
                                    jnp.zeros((b, CONV_W - 1, cch), F32),
                                    jnp.zeros((b, G_HEADS, G_DK, G_DV), F32), *gdn_w, n_seq=b)
        mk, mv = _mem_kv(mem_prompt.reshape(b * n_mem, d), mem_norm[l], w_mem_kv[l], m_k_norm[l],
                         tm=_tile(b * n_mem, 256))
        m_out = _mem_attention_wide(r3(mq), mk.reshape(b, n_mem, -1), mv.reshape(b, n_mem, -1), tm=_tile(s, 512))
        y2 = _merge_ffn(x2, a_out.reshape(b * s, -1), g_out.reshape(b * s, -1), m_out.reshape(b * s, -1), gates,
                        *ffn_w, tm=_tile(b * s, 256))
        yp = y2.reshape(b, s, d)
        p_states.append((ak.reshape(b, s, A_KV_HEADS, A_DH), av,
                         misc[:, MISC_IK:MISC_IK + IDX_DH].reshape(b, s, IDX_DH), p_gdn, p_conv,
                         mk.reshape(b, n_mem, M_HEADS, M_DH), mv.reshape(b, n_mem, M_HEADS, M_DH)))

        xs2 = ys.reshape(db * t, d)
        aq, ak, av, iq, misc, gqkv, gz, mq, gates = _inproj(xs2, *proj_w, tm=_tile(db * t, 256))
        r3 = lambda a: a.reshape(db, t, a.shape[-1])
        n_phys, page = cache_idx_k.shape[1], cache_idx_k.shape[2]
        pool_t = lambda c: jnp.transpose(c, (0, 2, 3, 1)).reshape(n_phys, kvw, page)
        a_out = _sample_attention(r3(aq), r3(ak), r3(av), r3(iq), r3(misc), pool_t(cache_k[l]), pool_t(cache_v[l]),
                                  jnp.transpose(cache_idx_k[l], (0, 2, 1)), page_table, n_seq=_tile(db, 4))
        g_out, s_gdn, s_conv = _gdn(r3(gqkv), r3(gz), r3(misc), state_conv[l], state_gdn[l], *gdn_w,
                                    n_seq=_tile(db, 8))
        m_out = _mem_attention(r3(mq), cache_mem_k[l].reshape(db, n_mem * M_HEADS, M_DH),
                               cache_mem_v[l].reshape(db, n_mem * M_HEADS, M_DH), tm=t, interleaved=True)
        y2 = _merge_ffn(xs2, a_out.reshape(db * t, -1), g_out.reshape(db * t, -1), m_out.reshape(db * t, -1), gates,
                        *ffn_w, tm=_tile(db * t, 256))
        ys = y2.reshape(db, t, d)
        s_states.append((ak.reshape(db, t, A_KV_HEADS, A_DH), av.reshape(db, t, A_KV_HEADS, A_DH),
                         misc[:, MISC_IK:MISC_IK + IDX_DH].reshape(db, t, IDX_DH), s_gdn, s_conv))

    p_k, p_v, p_idx_k, p_gdn, p_conv, p_mem_k, p_mem_v = [jnp.stack(z) for z in zip(*p_states)]
    s_k, s_v, s_idx_k, s_gdn, s_conv = [jnp.stack(z) for z in zip(*s_states)]
    return (yp, ys, p_k, p_v, p_idx_k, p_gdn, p_conv, p_mem_k, p_mem_v, s_k, s_v, s_idx_k, s_gdn, s_conv)
```

```python
import functools

import numpy as np
import jax
import jax.numpy as jnp
from jax import lax
from jax.experimental import pallas as pl
from jax.experimental.pallas import tpu as pltpu

F32 = jnp.float32
BF16 = jnp.bfloat16
I32 = jnp.int32

A_HEADS = 8
A_KV_HEADS = 4
A_DH = 64
IDX_HEADS = 4
IDX_DH = 64
TOPK_MAX = 256
G_HEADS = 4
G_DK = 128
G_DV = 128
CONV_W = 4
G_CHUNK = 64
M_HEADS = 4
M_DH = 128
EPS = 1e-6

LANES = 128
VMEM_LIMIT = 56 * 1024 * 1024
INT_MIN = np.int32(-2 ** 31)

MISC_IK = 0
MISC_IW = IDX_DH
MISC_GB = MISC_IW + IDX_HEADS
MISC_GA = MISC_GB + G_HEADS

NN = (((1,), (0,)), ((), ()))
NT = (((1,), (1,)), ((), ()))
TN = (((0,), (0,)), ((), ()))


def _dg(a, b, dn=NN):
    return lax.dot_general(a, b, dn, preferred_element_type=F32)


def _dot1(a, b, dn=NN):
    return _dg(a.astype(BF16), b.astype(BF16), dn)


def _split2(x):
    hi = x.astype(BF16)
    return hi, (x - hi.astype(F32)).astype(BF16)


def _dot3(a, b, dn=NN):
    ah, al = _split2(a)
    bh, bl = _split2(b)
    return _dg(ah, bh, dn) + (_dg(ah, bl, dn) + _dg(al, bh, dn))


def _sigmoid(x):
    return 1.0 / (1.0 + jnp.exp(-x))


def _silu(x):
    return x * _sigmoid(x)


def _softplus(x):
    return jnp.maximum(x, 0.0) + jnp.log1p(jnp.exp(-jnp.abs(x)))


def _rms_rows(x, gain):
    ms = jnp.mean(x * x, axis=-1, keepdims=True)
    return x * lax.rsqrt(ms + EPS) * gain


def _headnorm_pairs(h, gain128):
    rows, width = h.shape
    lo_mask = lax.broadcasted_iota(I32, (rows, LANES), 1) < (LANES // 2)
    outs = []
    for c in range(width // LANES):
        s = h[:, c * LANES:(c + 1) * LANES]
        sq = s * s
        lo = jnp.sum(jnp.where(lo_mask, sq, 0.0), axis=-1, keepdims=True)
        hi = jnp.sum(jnp.where(lo_mask, 0.0, sq), axis=-1, keepdims=True)
        ms = jnp.where(lo_mask, lo, hi) * (2.0 / LANES)
        outs.append(s * lax.rsqrt(ms + EPS) * gain128)
    return outs


def _headnorm_full(h, gain128):
    outs = []
    for c in range(h.shape[1] // LANES):
        outs.append(_rms_rows(h[:, c * LANES:(c + 1) * LANES], gain128))
    return outs


def _const_spec(shape):
    nd = len(shape)
    return pl.BlockSpec(shape, lambda *_: (0,) * nd, pipeline_mode=pl.Buffered(1))


def _params(sem):
    return pltpu.CompilerParams(dimension_semantics=sem, vmem_limit_bytes=VMEM_LIMIT)


_P_AQ = (0, 512)
_P_AK = (512, 768)
_P_AV = (768, 1024)
_P_IQ = (1024, 1280)
_P_MISC = (1280, 1408)
_P_GQKV = (1408, 2944)
_P_GZ = (2944, 3456)
_P_MQ = (3456, 3968)
_P_GATES = (3968, 7040)
_P_WIDTH = 7040


def _pack_w_in(w_in):
    d = w_in.shape[0]
    sizes = (A_HEADS * A_DH, A_KV_HEADS * A_DH, A_KV_HEADS * A_DH, IDX_HEADS * IDX_DH, IDX_DH, IDX_HEADS,
             2 * G_HEADS * G_DK + G_HEADS * G_DV, G_HEADS * G_DV, G_HEADS, G_HEADS, M_HEADS * M_DH)
    offs = np.cumsum((0,) + sizes)
    aq, ak, av, iq, ik, iw, gqkv, gz, gb, ga, mq = (w_in[:, offs[i]:offs[i + 1]] for i in range(len(sizes)))
    gates = w_in[:, offs[-1]:]
    pad = jnp.zeros((d, LANES - (IDX_DH + IDX_HEADS + 2 * G_HEADS)), w_in.dtype)
    misc = jnp.concatenate([ik, iw, gb, ga, pad], axis=1)
    packed = jnp.concatenate([aq, ak, av, iq, misc, gqkv, gz, mq, gates], axis=1)
    assert packed.shape[1] == _P_WIDTH
    return packed.astype(BF16)


_T_AQ = (0, 512)
_T_IQ = (512, 768)
_T_IW = (768, 776)
_T_AV = (776, 1032)
_T_ROWS = 1032


def _pack_w_in_t(w_in):
    sizes = (A_HEADS * A_DH, A_KV_HEADS * A_DH, A_KV_HEADS * A_DH, IDX_HEADS * IDX_DH, IDX_DH, IDX_HEADS)
    offs = np.cumsum((0,) + sizes)
    aq, _, av, iq, _, iw = (w_in[:, offs[i]:offs[i + 1]] for i in range(len(sizes)))
    pad = jnp.zeros((w_in.shape[0], _T_IW[1] - _T_IW[0] - IDX_HEADS), w_in.dtype)
    packed = jnp.concatenate([aq, iq, iw, pad, av], axis=1).T
    assert packed.shape[0] == _T_ROWS
    return packed.astype(BF16)


def _inproj_common(xn, w_ref, akg_ref, mqg_ref, misc_scale_ref, ak_ref, misc_ref, gqkv_ref, gz_ref, mq_ref, gates_ref):
    def proj(rng):
        return jnp.dot(xn, w_ref[:, rng[0]:rng[1]], preferred_element_type=F32)

    for c, y in enumerate(_headnorm_pairs(proj(_P_AK), akg_ref[...])):
        ak_ref[:, c * LANES:(c + 1) * LANES] = y
    misc_ref[...] = proj(_P_MISC) * misc_scale_ref[...]
    gqkv_ref[...] = proj(_P_GQKV)
    gz_ref[...] = proj(_P_GZ)
    for c, y in enumerate(_headnorm_full(proj(_P_MQ), mqg_ref[...])):
        mq_ref[:, c * LANES:(c + 1) * LANES] = y
    gates_ref[...] = _sigmoid(proj(_P_GATES))
    return proj


def _inproj_kernel(x_ref, gain_ref, w_ref, aqg_ref, akg_ref, mqg_ref, misc_scale_ref,
                   aq_ref, ak_ref, av_ref, iq_ref, misc_ref, gqkv_ref, gz_ref, mq_ref, gates_ref):
    xn = _rms_rows(x_ref[...], gain_ref[...]).astype(BF16)
    proj = _inproj_common(xn, w_ref, akg_ref, mqg_ref, misc_scale_ref, ak_ref, misc_ref, gqkv_ref, gz_ref, mq_ref,
                          gates_ref)
    for c, y in enumerate(_headnorm_pairs(proj(_P_AQ), aqg_ref[...])):
        aq_ref[:, c * LANES:(c + 1) * LANES] = y
    av_ref[...] = proj(_P_AV)
    iq_ref[...] = proj(_P_IQ)


def _inproj_t_kernel(x_ref, gain_ref, w_ref, wt_ref, aqg_ref, akg_ref, mqg_ref, misc_scale_ref,
                     aqt_ref, iqt_ref, iwt_ref, avt_ref, ak_ref, misc_ref, gqkv_ref, gz_ref, mq_ref, gates_ref):
    xn = _rms_rows(x_ref[...], gain_ref[...]).astype(BF16)
    _inproj_common(xn, w_ref, akg_ref, mqg_ref, misc_scale_ref, ak_ref, misc_ref, gqkv_ref, gz_ref, mq_ref, gates_ref)

    def proj_t(rng):
        return _dg(wt_ref[rng[0]:rng[1], :], xn, NT)

    aqt = proj_t(_T_AQ)
    gain_col = aqg_ref[...]
    for h in range(A_HEADS):
        qh = aqt[h * A_DH:(h + 1) * A_DH, :]
        ms = jnp.mean(qh * qh, axis=0, keepdims=True)
        aqt_ref[h * A_DH:(h + 1) * A_DH, :] = qh * lax.rsqrt(ms + EPS) * gain_col
    iqt_ref[...] = proj_t(_T_IQ)
    iwt_ref[...] = proj_t(_T_IW) * (IDX_HEADS ** -0.5)
    avt_ref[...] = proj_t(_T_AV)


def _inproj(x2d, norm_mix, w_packed, w_packed_t, a_q_norm, a_k_norm, m_q_norm, tm, seq=None):
    n, d = x2d.shape
    transposed = seq is not None
    lane = np.arange(LANES)
    misc_scale = np.where((lane >= MISC_IW) & (lane < MISC_GB), IDX_HEADS ** -0.5, 1.0).astype(np.float32)[None]
    akg = jnp.tile(a_k_norm.reshape(1, A_DH), (1, 2))
    mqg = m_q_norm.reshape(1, M_DH)
    common = [r[1] - r[0] for r in (_P_MISC, _P_GQKV, _P_GZ, _P_MQ, _P_GATES)]
    row_spec = lambda w: pl.BlockSpec((tm, w), lambda i: (i, 0))
    row_shape = lambda w: jax.ShapeDtypeStruct((n, w), F32)
    x_specs = [row_spec(d), _const_spec((1, d)), _const_spec((d, _P_WIDTH))]
    g_specs = [_const_spec((1, LANES)), _const_spec((1, LANES)), _const_spec((1, LANES))]
    if not transposed:
        widths = [512, 256, 256, 256] + common
        return pl.pallas_call(
            _inproj_kernel,
            grid=(n // tm,),
            in_specs=x_specs + [_const_spec((1, LANES))] + g_specs,
            out_specs=[row_spec(w) for w in widths],
            out_shape=[row_shape(w) for w in widths],
            compiler_params=_params(("parallel",)),
            name="inproj",
        )(x2d, norm_mix.reshape(1, d), w_packed, jnp.tile(a_q_norm.reshape(1, A_DH), (1, 2)), akg, mqg,
          jnp.asarray(misc_scale))
    t_rows = [r[1] - r[0] for r in (_T_AQ, _T_IQ, _T_IW, _T_AV)]
    widths = [256] + common
    tiles = seq // tm
    col_spec = lambda r: pl.BlockSpec((None, r, tm), lambda i: (i // tiles, 0, i % tiles))
    return pl.pallas_call(
        _inproj_t_kernel,
        grid=(n // tm,),
        in_specs=x_specs + [_const_spec((_T_ROWS, d)), _const_spec((A_DH, 1))] + g_specs,
        out_specs=[col_spec(r) for r in t_rows] + [row_spec(w) for w in widths],
        out_shape=[jax.ShapeDtypeStruct((n // seq, r, seq), F32) for r in t_rows] + [row_shape(w) for w in widths],
        compiler_params=_params(("parallel",)),
        name="inproj_t",
    )(x2d, norm_mix.reshape(1, d), w_packed, w_packed_t, a_q_norm.reshape(A_DH, 1), akg, mqg, jnp.asarray(misc_scale))


KEY_NEG_INF = np.int32(-2 ** 31 + 0x7FFFFF)


def _canon_zero(score):
    return jnp.where(score == 0.0, 0.0, score)


def _key_to_float(key):
    key = jnp.maximum(key, KEY_NEG_INF)
    return lax.bitcast_convert_type(key ^ ((key >> 31) & np.int32(0x7FFFFFFF)), F32)


def _as_i32(v):
    return np.int32(v - (1 << 32) if v >= (1 << 31) else v)


def _radix_select(count_fn, shape, bcast, k_top, idx_bits, bits_per_pass):
    thr_f, need, n_eq = _radix_threshold(count_fn, shape, bcast, k_top, bits_per_pass)
    del n_eq
    return thr_f, _tie_cut(count_fn, shape, bcast, thr_f, need, idx_bits, bits_per_pass)


def _radix_threshold(count_fn, shape, bcast, k_top, bits_per_pass):
    thr = jnp.full(shape, INT_MIN, I32)
    for hi in range(32, 0, -bits_per_pass):
        lo = max(hi - bits_per_pass, 0)
        cands = [bcast(_key_to_float(thr ^ _as_i32(v << lo))) for v in range(1, 2 ** (hi - lo))]
        cnts = count_fn([lambda s, c, cf=cf: s >= cf for cf in cands])
        digit = sum(jnp.where(cnt >= k_top, 1, 0) for cnt in cnts)
        thr = thr ^ (digit << lo)
    thr_f = _key_to_float(thr)
    thr_b = bcast(thr_f)
    n_gt, n_eq = count_fn([lambda s, c: s > thr_b, lambda s, c: s == thr_b])
    return thr_f, k_top - n_gt, n_eq


def _tie_cut(count_fn, shape, bcast, thr_f, need, idx_bits, bits_per_pass):
    thr_b = bcast(thr_f)
    cut = jnp.zeros(shape, I32)
    for hi in range(idx_bits, 0, -bits_per_pass):
        lo = max(hi - bits_per_pass, 0)
        cands = [bcast(cut | np.int32(v << lo)) for v in range(1, 2 ** (hi - lo))]
        cnts = count_fn([lambda s, c, cb=cb: (s == thr_b) & (c < cb) for cb in cands])
        digit = sum(jnp.where(cnt < need, 1, 0) for cnt in cnts)
        cut = cut | (digit << lo)
    return cut


def _prompt_attn_kernel(aqt_ref, iqt_ref, iwt_ref, k_ref, vt_ref, misc_ref, out_ref,
                        score_ref, kb_ref, ki_ref, vth_ref, qblk_ref, acc_ref, cut_ref, *, tq, k_top, idx_bits):
    j = pl.program_id(1)
    nk = j + 1
    ck = tq
    n_chunks = score_ref.shape[0]
    sub = 8
    group = A_HEADS // A_KV_HEADS
    head_rows = [slice(h * A_DH, (h + 1) * A_DH) for h in range(A_HEADS)]
    head_cols = [slice(h * tq, (h + 1) * tq) for h in range(A_HEADS)]
    kv_rows = [slice((h // group) * A_DH, (h // group + 1) * A_DH) for h in range(A_HEADS)]

    @pl.when(j == 0)
    def _():
        for c in range(n_chunks):
            rows = slice(c * ck, (c + 1) * ck)
            kb_ref[rows, :] = k_ref[rows, :].astype(BF16)
            ki_ref[rows, :] = misc_ref[rows, MISC_IK:MISC_IK + IDX_DH].astype(BF16)
            vth_ref[c] = vt_ref[:, rows].astype(BF16)
        qblk_ref[...] = jnp.zeros(qblk_ref.shape, BF16)

    key_in_chunk = lax.broadcasted_iota(I32, (ck, tq), 0)
    q_pos = j * tq + lax.broadcasted_iota(I32, (ck, tq), 1)

    iqb = (iqt_ref[...] * (IDX_DH ** -0.5)).astype(BF16)
    iq_cat = jnp.concatenate([iqb[h * IDX_DH:(h + 1) * IDX_DH, :] for h in range(IDX_HEADS)], axis=1)
    iw = iwt_ref[...]

    def idx_chunk(c, diagonal):
        start = pl.multiple_of(c * ck, ck)
        dots = _dg(ki_ref[pl.ds(start, ck), :], iq_cat)
        acc = jnp.zeros((ck, tq), F32)
        for h in range(IDX_HEADS):
            acc = acc + jnp.maximum(dots[:, h * tq:(h + 1) * tq], 0.0) * iw[h:h + 1, :]
        if diagonal:
            acc = jnp.where(c * ck + key_in_chunk <= q_pos, acc, -jnp.inf)
        score_ref[c] = _canon_zero(acc)

    def idx_body(c, carry):
        idx_chunk(c, False)
        return carry

    lax.fori_loop(0, j, idx_body, 0)
    idx_chunk(j, True)
    step = 1
    n_iter = nk

    key_in_group = lax.broadcasted_iota(I32, (sub, tq), 0)
    n_acc = 4

    def count_fn(preds):
        def body(it, cnts):
            cnts = [list(a) for a in cnts]
            for u in range(step):
                c = it * step + u
                for i in range(ck // sub):
                    sc = score_ref[c, i * sub:(i + 1) * sub, :]
                    key_id = c * ck + i * sub + key_in_group
                    for n, pred in enumerate(preds):
                        cnts[n][i % n_acc] = cnts[n][i % n_acc] + jnp.where(pred(sc, key_id), 1.0, 0.0)
            return tuple(tuple(a) for a in cnts)
        zero = tuple(jnp.zeros((sub, tq), F32) for _ in range(n_acc))
        cnts = lax.fori_loop(0, n_iter, body, tuple(zero for _ in preds))
        return [jnp.sum(sum(a), axis=0, keepdims=True) for a in cnts]

    bcast = lambda v: jnp.broadcast_to(v, (sub, tq))
    thr, need, n_eq = _radix_threshold(count_fn, (1, tq), bcast, k_top, 1)
    cut_ref[...] = jnp.full((1, tq), 2 ** idx_bits - 1, I32)

    @pl.when(jnp.max(n_eq - need) > 0.5)
    def _():
        cut_ref[...] = _tie_cut(count_fn, (1, tq), bcast, thr, need, idx_bits, 1)

    cut = cut_ref[...]

    def mask_body(c, carry):
        sc = score_ref[c]
        key_id = c * ck + key_in_chunk
        sel = ((sc > thr) | ((sc == thr) & (key_id <= cut))) & (key_id <= q_pos)
        score_ref[c] = jnp.where(sel, 0.0, -jnp.inf)
        return carry

    lax.fori_loop(0, nk, mask_body, 0)

    qb = (aqt_ref[...] * (A_DH ** -0.5)).astype(BF16)
    for h in range(A_HEADS):
        qblk_ref[kv_rows[h], head_cols[h]] = qb[head_rows[h], :]
    acc_ref[...] = jnp.zeros(acc_ref.shape, F32)

    def chunk_scores(c):
        start = pl.multiple_of(c * ck, ck)
        return _dg(kb_ref[pl.ds(start, ck), :], qblk_ref[...])

    def fold(x, acc, op):
        for i in range(ck // sub):
            acc = op(acc, x[i * sub:(i + 1) * sub, :])
        return acc

    def att_body(c, carry):
        ms, lparts = carry
        s_all = chunk_scores(c)
        bias = score_ref[c]
        scores = [s_all[:, head_cols[h]] + bias for h in range(A_HEADS)]
        m_new = [jnp.maximum(ms[h], jnp.max(fold(scores[h], jnp.full((sub, tq), -1e30, F32), jnp.maximum),
                                            axis=0, keepdims=True)) for h in range(A_HEADS)]
        alphas = [jnp.exp(ms[h] - m_new[h]) for h in range(A_HEADS)]
        probs, new_l = [], []
        for h in range(A_HEADS):
            p = jnp.exp(scores[h] - m_new[h])
            new_l.append(fold(p, alphas[h] * lparts[h], jnp.add))
            probs.append(p.astype(BF16))
        pv = _dg(vth_ref[c], jnp.concatenate(probs, axis=1))
        for h in range(A_HEADS):
            acc_ref[head_rows[h], :] = alphas[h] * acc_ref[head_rows[h], :] + pv[kv_rows[h], head_cols[h]]
        return tuple(m_new), tuple(new_l)

    init = (tuple(jnp.full((1, tq), -1e30, F32) for _ in range(A_HEADS)),
            tuple(jnp.zeros((sub, tq), F32) for _ in range(A_HEADS)))
    _, lparts = lax.fori_loop(0, nk, att_body, init)
    for h in range(A_HEADS):
        acc_ref[head_rows[h], :] = acc_ref[head_rows[h], :] / jnp.sum(lparts[h], axis=0, keepdims=True)
    out_ref[...] = acc_ref[...].T


def _prompt_attention(aqt, iqt, iwt, ak, avt, misc, tq):
    b, s, kvw = ak.shape
    nq = s // tq
    k_top = min(TOPK_MAX, s // 4)
    idx_bits = max(1, int(np.ceil(np.log2(s))))
    kern = functools.partial(_prompt_attn_kernel, tq=tq, k_top=k_top, idx_bits=idx_bits)
    qspec = lambda r: pl.BlockSpec((None, r, tq), lambda bi, j: (bi, 0, j))
    kspec = lambda w: pl.BlockSpec((None, s, w), lambda bi, j: (bi, 0, 0))
    return pl.pallas_call(
        kern,
        grid=(b, nq),
        in_specs=[qspec(A_HEADS * A_DH), qspec(IDX_HEADS * IDX_DH), qspec(_T_IW[1] - _T_IW[0]),
                  kspec(kvw), pl.BlockSpec((None, kvw, s), lambda bi, j: (bi, 0, 0)), kspec(LANES)],
        out_specs=pl.BlockSpec((None, tq, A_HEADS * A_DH), lambda bi, j: (bi, j, 0)),
        out_shape=jax.ShapeDtypeStruct((b, s, A_HEADS * A_DH), F32),
        scratch_shapes=[pltpu.VMEM((nq, tq, tq), F32),
                        pltpu.VMEM((s, kvw), BF16), pltpu.VMEM((s, IDX_DH), BF16),
                        pltpu.VMEM((nq, kvw, tq), BF16), pltpu.VMEM((kvw, A_HEADS * tq), BF16),
                        pltpu.VMEM((A_HEADS * A_DH, tq), F32), pltpu.VMEM((1, tq), I32)],
        compiler_params=_params(("arbitrary", "arbitrary")),
        name="prompt_attn",
    )(aqt, iqt, iwt, ak, avt, misc)


def _sample_attn_kernel(pt_ref, aq_ref, iq_ref, misc_ref, kn_ref, vn_ref, *refs, n_seq, n_pages, page, k_top, idx_bits):
    npg = n_seq * n_pages
    ki_refs = [refs[u * n_pages:(u + 1) * n_pages] for u in range(n_seq)]
    k_refs = [refs[npg + u * n_pages:npg + (u + 1) * n_pages] for u in range(n_seq)]
    v_refs = [refs[2 * npg + u * n_pages:2 * npg + (u + 1) * n_pages] for u in range(n_seq)]
    out_ref = refs[3 * npg]
    cut_ref = refs[3 * npg + 1]
    t = aq_ref.shape[1]
    rows = n_seq * t
    assert page == LANES and t <= LANES
    tok = jnp.concatenate([lax.broadcasted_iota(I32, (t, LANES), 0)] * n_seq, axis=0)
    lane_ids = lax.broadcasted_iota(I32, (rows, LANES), 1)
    new_ok = lane_ids <= tok
    group = A_HEADS // A_KV_HEADS

    page_scores = [[] for _ in range(n_pages + 1)]
    for u in range(n_seq):
        iqb = (iq_ref[u] * (IDX_DH ** -0.5)).astype(BF16)
        q4 = jnp.concatenate([iqb[:, h * IDX_DH:(h + 1) * IDX_DH] for h in range(IDX_HEADS)], axis=0)
        iw = misc_ref[u][:, MISC_IW:MISC_IW + IDX_HEADS]
        new_rows = jnp.concatenate([misc_ref[u], jnp.zeros((LANES - t, LANES), F32)], axis=0)
        dots = [_dg(q4, ki_refs[u][p][...].astype(BF16)) for p in range(n_pages)]
        dots.append(_dg(q4, new_rows[:, MISC_IK:MISC_IK + IDX_DH].astype(BF16), NT))
        for p, d in enumerate(dots):
            acc = jnp.zeros((t, LANES), F32)
            for h in range(IDX_HEADS):
                acc = acc + jnp.maximum(d[h * t:(h + 1) * t, :], 0.0) * iw[:, h:h + 1]
            page_scores[p].append(acc)
    keys = [_canon_zero(jnp.concatenate(ps, axis=0)) for ps in page_scores]
    keys[n_pages] = jnp.where(new_ok, keys[n_pages], -jnp.inf)

    def count_fn(preds):
        cnts = []
        for pred in preds:
            cnt = jnp.zeros((rows, LANES), F32)
            for p, kc in enumerate(keys):
                cnt = cnt + jnp.where(pred(kc, p * LANES + lane_ids), 1.0, 0.0)
            cnts.append(jnp.sum(cnt, axis=-1, keepdims=True))
        return cnts

    bcast = lambda v: jnp.broadcast_to(v, (rows, LANES))
    thr, need, n_eq = _radix_threshold(count_fn, (rows, 1), bcast, k_top, 4)
    cut_ref[...] = jnp.full((rows, 1), 2 ** idx_bits - 1, I32)

    @pl.when(jnp.max(n_eq - need) > 0.5)
    def _():
        cut_ref[...] = _tie_cut(count_fn, (rows, 1), bcast, thr, need, idx_bits, 4)

    cut = cut_ref[...]
    biases = []
    for p in range(n_pages + 1):
        cols = p * LANES + lane_ids
        sel = (keys[p] > thr) | ((keys[p] == thr) & (cols <= cut))
        if p == n_pages:
            sel = sel & new_ok
        biases.append(jnp.where(sel, 0.0, -jnp.inf))

    chs = [slice(g * A_DH, (g + 1) * A_DH) for g in range(A_KV_HEADS)]
    units = [(u, g) for u in range(n_seq) for g in range(A_KV_HEADS)]
    pad_kv = jnp.zeros((LANES - t, A_KV_HEADS * A_DH), F32)
    k_new = [jnp.concatenate([kn_ref[u], pad_kv], axis=0).astype(BF16) for u in range(n_seq)]
    v_new = [jnp.concatenate([vn_ref[u], pad_kv], axis=0).astype(BF16) for u in range(n_seq)]
    scores = []
    for u, g in units:
        qb = (aq_ref[u] * (A_DH ** -0.5)).astype(BF16)
        q_stack = jnp.concatenate([qb[:, (g * group + r) * A_DH:(g * group + r + 1) * A_DH] for r in range(group)], axis=0)
        sg = [_dg(q_stack, k_refs[u][p][chs[g], :].astype(BF16)) for p in range(n_pages)]
        sg.append(_dg(q_stack, k_new[u][:, chs[g]], NT))
        scores.append(sg)
    probs, dens = [], []
    for i, (u, g) in enumerate(units):
        sg = [s + jnp.concatenate([biases[p][u * t:(u + 1) * t, :]] * group, axis=0) for p, s in enumerate(scores[i])]
        mx = sg[0]
        for s in sg[1:]:
            mx = jnp.maximum(mx, s)
        mx = jnp.max(mx, axis=-1, keepdims=True)
        es = [jnp.exp(s - mx) for s in sg]
        dens.append(jnp.sum(sum(es), axis=-1, keepdims=True))
        probs.append([e.astype(BF16) for e in es])
    for i, (u, g) in enumerate(units):
        pvs = [_dg(probs[i][p], v_refs[u][p][chs[g], :].astype(BF16), NT) for p in range(n_pages)]
        pvs.append(_dg(probs[i][n_pages], v_new[u][:, chs[g]]))
        o = sum(pvs) / dens[i]
        for r in range(group):
            h = g * group + r
            out_ref[u, :, h * A_DH:(h + 1) * A_DH] = o[r * t:(r + 1) * t, :]


def _sample_attention(aq, ak, av, iq, misc, pool_kt, pool_vt, pool_kit, page_table, n_seq):
    db, t, _ = aq.shape
    assert db % n_seq == 0
    n_pages = page_table.shape[1]
    page = pool_kit.shape[2]
    total = n_pages * page + t
    k_top = min(TOPK_MAX, total // 4)
    idx_bits = max(1, int(np.ceil(np.log2((n_pages + 1) * LANES))))
    kern = functools.partial(_sample_attn_kernel, n_seq=n_seq, n_pages=n_pages, page=page, k_top=k_top,
                             idx_bits=idx_bits)
    qspec = lambda w: pl.BlockSpec((n_seq, t, w), lambda i, pt: (i, 0, 0))
    pspec = lambda w, u, p: pl.BlockSpec((None, w, page), lambda i, pt: (pt[i * n_seq + u, p], 0, 0))
    kvw = A_KV_HEADS * A_DH
    pages = [(u, p) for u in range(n_seq) for p in range(n_pages)]
    in_specs = [qspec(A_HEADS * A_DH), qspec(IDX_HEADS * IDX_DH), qspec(LANES), qspec(kvw), qspec(kvw)]
    in_specs += [pspec(IDX_DH, u, p) for u, p in pages]
    in_specs += [pspec(kvw, u, p) for u, p in pages]
    in_specs += [pspec(kvw, u, p) for u, p in pages]
    return pl.pallas_call(
        kern,
        grid_spec=pltpu.PrefetchScalarGridSpec(
            num_scalar_prefetch=1, grid=(db // n_seq,), in_specs=in_specs, out_specs=qspec(A_HEADS * A_DH),
            scratch_shapes=[pltpu.VMEM((n_seq * t, 1), I32)]),
        out_shape=jax.ShapeDtypeStruct((db, t, A_HEADS * A_DH), F32),
        compiler_params=_params(("arbitrary",)),
        name="sample_attn",
    )(page_table, aq, iq, misc, ak, av, *([pool_kit] * len(pages)), *([pool_kt] * len(pages)),
      *([pool_vt] * len(pages)))


def _gdn_kernel(u_ref, gz_ref, misc_ref, conv0_ref, s0_ref, convw_ref, alog_ref, dtb_ref, onorm_ref,
                out_ref, s_out_ref, conv_out_ref, state_ref, ubuf_ref, *, chunk, n_seq):
    c = pl.program_id(1)
    n_c = pl.num_programs(1)
    hist = CONV_W - 1
    base = 8

    @pl.when(c == 0)
    def _():
        state_ref[...] = s0_ref[...]
        ubuf_ref[:, base - hist:base, :] = conv0_ref[...]

    ii = lax.broadcasted_iota(I32, (chunk, chunk), 0)
    jj = lax.broadcasted_iota(I32, (chunk, chunk), 1)
    lower = jnp.where(ii >= jj, 1.0, 0.0).astype(BF16)
    upper = jnp.where(ii <= jj, 1.0, 0.0).astype(BF16)
    lane = lax.broadcasted_iota(I32, (chunk, LANES), 1)
    ga_lanes = (lane >= MISC_GA) & (lane < MISC_GA + G_HEADS)
    qk = G_HEADS * G_DK
    n_dbl = int(np.log2(chunk))
    assert 2 ** n_dbl == chunk

    per_seq = []
    for b in range(n_seq):
        ubuf_ref[b, base:base + chunk, :] = u_ref[b]
        conv = 0.0
        for jt in range(CONV_W):
            conv = conv + ubuf_ref[b, base - hist + jt:base - hist + jt + chunk, :] * convw_ref[jt:jt + 1, :]
        act = _silu(conv)
        tail = ubuf_ref[b, base + chunk - hist:base + chunk, :]
        ubuf_ref[b, base - hist:base, :] = tail
        conv_out_ref[b] = tail

        misc = misc_ref[b]
        beta_all = _sigmoid(misc)
        g_all = jnp.where(ga_lanes, -jnp.exp(alog_ref[...]) * _softplus(misc + dtb_ref[...]), 0.0)
        g1 = g_all.astype(BF16)
        r1 = g_all - g1.astype(F32)
        g2 = r1.astype(BF16)
        g3 = (r1 - g2.astype(F32)).astype(BF16)
        gc_col = _dg(lower, g1) + _dg(lower, g2) + _dg(lower, g3)
        gc_row = _dg(g1, upper, TN) + _dg(g2, upper, TN) + _dg(g3, upper, TN)
        per_seq.append((act, beta_all, gc_col, gc_row))

    chains = [(b, h) for b in range(n_seq) for h in range(G_HEADS)]
    qs, ks, kbs, gammas, egs, gcols, sols = [], [], [], [], [], [], []
    for b, h in chains:
        act, beta_all, gc_col, gc_row = per_seq[b]
        q = act[:, h * G_DK:(h + 1) * G_DK]
        k = act[:, qk + h * G_DK:qk + (h + 1) * G_DK]
        v = act[:, 2 * qk + h * G_DV:2 * qk + (h + 1) * G_DV]
        q = q * lax.rsqrt(jnp.sum(q * q, axis=-1, keepdims=True) + EPS) * (G_DK ** -0.5)
        k = k * lax.rsqrt(jnp.sum(k * k, axis=-1, keepdims=True) + EPS)
        beta = beta_all[:, MISC_GB + h:MISC_GB + h + 1]
        gcol = gc_col[:, MISC_GA + h:MISC_GA + h + 1]
        grow = gc_row[MISC_GA + h:MISC_GA + h + 1, :]
        eg = jnp.exp(gcol)
        kb = k * beta
        qs.append(q)
        ks.append(k)
        kbs.append(kb)
        gammas.append(jnp.exp(jnp.where(ii >= jj, gcol - grow, -jnp.inf)))
        egs.append(eg)
        gcols.append(gcol)
        sols.append(jnp.concatenate([v * beta, kb * eg], axis=-1))
    n_ch = len(chains)
    kk = [_dot1(kbs[i], ks[i], NT) for i in range(n_ch)]
    pws = [jnp.where(ii > jj, -(kk[i] * gammas[i]), 0.0) for i in range(n_ch)]
    for it in range(n_dbl):
        upd = [_dot3(pws[i], sols[i]) for i in range(n_ch)]
        if it + 1 < n_dbl:
            pws = [_dot1(pws[i], pws[i]) for i in range(n_ch)]
        sols = [sols[i] + upd[i] for i in range(n_ch)]
    aqk = [_dot1(qs[i], ks[i], NT) for i in range(n_ch)]
    sts = [state_ref[b, h] for b, h in chains]
    ws = [_dot1(sols[i][:, G_DV:], sts[i]) for i in range(n_ch)]
    o1 = [_dot1(qs[i] * egs[i], sts[i]) for i in range(n_ch)]
    v_new = [sols[i][:, :G_DV] - ws[i] for i in range(n_ch)]
    o2 = [_dot1(aqk[i] * gammas[i], v_new[i]) for i in range(n_ch)]
    g_last = [gcols[i][chunk - 1:chunk, :] for i in range(n_ch)]
    kv = [_dot1(ks[i] * jnp.exp(g_last[i] - gcols[i]), v_new[i], TN) for i in range(n_ch)]
    for i, (b, h) in enumerate(chains):
        state_ref[b, h] = sts[i] * jnp.exp(g_last[i]) + kv[i]
        o = _rms_rows(o1[i] + o2[i], onorm_ref[...]) * _silu(gz_ref[b, :, h * G_DV:(h + 1) * G_DV])
        out_ref[b, :, h * G_DV:(h + 1) * G_DV] = o

    @pl.when(c == n_c - 1)
    def _():
        s_out_ref[...] = state_ref[...]


def _gdn(gqkv, gz, misc, conv0, s0, g_conv, g_a_log, g_dt_bias, g_o_norm, n_seq):
    b, t, cch = gqkv.shape
    chunk = min(G_CHUNK, t)
    assert t % chunk == 0 and chunk % 8 == 0 and b % n_seq == 0
    hist = CONV_W - 1
    alog = jnp.zeros((1, LANES), F32).at[0, MISC_GA:MISC_GA + G_HEADS].set(g_a_log.astype(F32))
    dtb = jnp.zeros((1, LANES), F32).at[0, MISC_GA:MISC_GA + G_HEADS].set(g_dt_bias.astype(F32))
    kern = functools.partial(_gdn_kernel, chunk=chunk, n_seq=n_seq)
    tspec = lambda w: pl.BlockSpec((n_seq, chunk, w), lambda bi, c: (bi, c, 0))
    sspec = pl.BlockSpec((n_seq, G_HEADS, G_DK, G_DV), lambda bi, c: (bi, 0, 0, 0))
    cspec = pl.BlockSpec((n_seq, hist, cch), lambda bi, c: (bi, 0, 0))
    return pl.pallas_call(
        kern,
        grid=(b // n_seq, t // chunk),
        in_specs=[tspec(cch), tspec(G_HEADS * G_DV), tspec(LANES), cspec, sspec,
                  _const_spec((CONV_W, cch)), _const_spec((1, LANES)), _const_spec((1, LANES)),
                  _const_spec((1, G_DV))],
        out_specs=[tspec(G_HEADS * G_DV), sspec, cspec],
        out_shape=[jax.ShapeDtypeStruct((b, t, G_HEADS * G_DV), F32),
                   jax.ShapeDtypeStruct((b, G_HEADS, G_DK, G_DV), F32),
                   jax.ShapeDtypeStruct((b, hist, cch), F32)],
        scratch_shapes=[pltpu.VMEM((n_seq, G_HEADS, G_DK, G_DV), F32), pltpu.VMEM((n_seq, 8 + chunk, cch), F32)],
        compiler_params=_params(("arbitrary", "arbitrary")),
        name="gdn",
    )(gqkv, gz, misc, conv0, s0, g_conv, alog, dtb, g_o_norm.reshape(1, G_DV))


def _mem_kv_kernel(mem_ref, gain_ref, w_ref, kg_ref, mk_ref, mv_ref):
    xn = _rms_rows(mem_ref[...], gain_ref[...]).astype(BF16)
    half = M_HEADS * M_DH
    kk = jnp.dot(xn, w_ref[:, :half], preferred_element_type=F32)
    for c, y in enumerate(_headnorm_full(kk, kg_ref[...])):
        mk_ref[:, c * LANES:(c + 1) * LANES] = y
    mv_ref[...] = jnp.dot(xn, w_ref[:, half:], preferred_element_type=F32)


def _mem_kv(mem2d, mem_norm, w_mem_kv, m_k_norm, tm):
    n, d = mem2d.shape
    half = M_HEADS * M_DH
    return pl.pallas_call(
        _mem_kv_kernel,
        grid=(n // tm,),
        in_specs=[pl.BlockSpec((tm, d), lambda i: (i, 0)), _const_spec((1, d)), _const_spec((d, 2 * half)),
                  _const_spec((1, M_DH))],
        out_specs=[pl.BlockSpec((tm, half), lambda i: (i, 0))] * 2,
        out_shape=[jax.ShapeDtypeStruct((n, half), F32)] * 2,
        compiler_params=_params(("parallel",)),
        name="mem_kv",
    )(mem2d, mem_norm.reshape(1, d), w_mem_kv.astype(BF16), m_k_norm.reshape(1, M_DH))


def _mem_attn_kernel(q_ref, k_ref, v_ref, out_ref):
    n_seq, t, _ = q_ref.shape
    n_rows = k_ref.shape[1]
    q_head = jnp.concatenate([jnp.full((t, n_rows), h, I32) for h in range(M_HEADS)], axis=0)
    row_id = lax.broadcasted_iota(I32, (M_HEADS * t, n_rows), 1)
    row_head = row_id & (M_HEADS - 1) if M_HEADS & (M_HEADS - 1) == 0 else row_id % M_HEADS
    own = q_head == row_head
    q_stacks = [jnp.concatenate([q_ref[u, :, h * M_DH:(h + 1) * M_DH] for h in range(M_HEADS)], axis=0).astype(BF16)
                for u in range(n_seq)]
    scores = [_dg(q_stacks[u], k_ref[u].astype(BF16), NT) for u in range(n_seq)]
    probs = []
    for u in range(n_seq):
        s = jnp.where(own, scores[u] * (M_DH ** -0.5), -jnp.inf)
        e = jnp.exp(s - jnp.max(s, axis=-1, keepdims=True))
        probs.append((e / jnp.sum(e, axis=-1, keepdims=True)).astype(BF16))
    outs = [_dg(probs[u], v_ref[u].astype(BF16)) for u in range(n_seq)]
    for u in range(n_seq):
        for h in range(M_HEADS):
            out_ref[u, :, h * M_DH:(h + 1) * M_DH] = outs[u][h * t:(h + 1) * t, :]


def _mem_attn_wide_kernel(q_ref, k_ref, v_ref, out_ref, kblk_ref, vblk_ref):
    n_mem = k_ref.shape[0]

    @pl.when(pl.program_id(1) == 0)
    def _():
        kblk_ref[...] = jnp.zeros(kblk_ref.shape, BF16)
        vblk_ref[...] = jnp.zeros(vblk_ref.shape, BF16)
        for h in range(M_HEADS):
            sl = slice(h * M_DH, (h + 1) * M_DH)
            kblk_ref[h * n_mem:(h + 1) * n_mem, sl] = k_ref[:, sl].astype(BF16)
            vblk_ref[h * n_mem:(h + 1) * n_mem, sl] = v_ref[:, sl].astype(BF16)

    s_all = _dg(q_ref[...].astype(BF16), kblk_ref[...], NT) * (M_DH ** -0.5)
    probs = []
    for h in range(M_HEADS):
        s = s_all[:, h * n_mem:(h + 1) * n_mem]
        e = jnp.exp(s - jnp.max(s, axis=-1, keepdims=True))
        probs.append((e / jnp.sum(e, axis=-1, keepdims=True)).astype(BF16))
    out_ref[...] = _dg(jnp.concatenate(probs, axis=1), vblk_ref[...])


def _mem_attention_wide(mq, mk, mv, tm):
    b, t, w = mq.shape
    m = mk.shape[1]
    return pl.pallas_call(
        _mem_attn_wide_kernel,
        grid=(b, t // tm),
        in_specs=[pl.BlockSpec((None, tm, w), lambda bi, i: (bi, i, 0)),
                  pl.BlockSpec((None, m, w), lambda bi, i: (bi, 0, 0)),
                  pl.BlockSpec((None, m, w), lambda bi, i: (bi, 0, 0))],
        out_specs=pl.BlockSpec((None, tm, w), lambda bi, i: (bi, i, 0)),
        out_shape=jax.ShapeDtypeStruct((b, t, w), F32),
        scratch_shapes=[pltpu.VMEM((M_HEADS * m, w), BF16), pltpu.VMEM((M_HEADS * m, w), BF16)],
        compiler_params=_params(("arbitrary", "arbitrary")),
        name="mem_attn_wide",
    )(mq, mk, mv)


def _mem_attention(mq, mk, mv, n_seq):
    b, t, w = mq.shape
    m, kw = mk.shape[1], mk.shape[2]
    assert b % n_seq == 0
    return pl.pallas_call(
        _mem_attn_kernel,
        grid=(b // n_seq,),
        in_specs=[pl.BlockSpec((n_seq, t, w), lambda i: (i, 0, 0)),
                  pl.BlockSpec((n_seq, m, kw), lambda i: (i, 0, 0)),
                  pl.BlockSpec((n_seq, m, kw), lambda i: (i, 0, 0))],
        out_specs=pl.BlockSpec((n_seq, t, w), lambda i: (i, 0, 0)),
        out_shape=jax.ShapeDtypeStruct((b, t, w), F32),
        compiler_params=_params(("parallel",)),
        name="mem_attn",
    )(mq, mk, mv)


def _merge_ffn_kernel(x_ref, a_ref, g_ref, m_ref, gates_ref, wa_ref, wg_ref, wm_ref, wo_ref, nf_ref,
                      win_ref, wout_ref, y_ref, *, d_ff, ff_chunk):
    d = x_ref.shape[1]
    gates = gates_ref[...]
    h = (gates[:, :d] * jnp.dot(a_ref[...].astype(BF16), wa_ref[...], preferred_element_type=F32)
         + gates[:, d:2 * d] * jnp.dot(g_ref[...].astype(BF16), wg_ref[...], preferred_element_type=F32)
         + gates[:, 2 * d:] * jnp.dot(m_ref[...].astype(BF16), wm_ref[...], preferred_element_type=F32))
    x1 = x_ref[...] + jnp.dot(h.astype(BF16), wo_ref[...], preferred_element_type=F32)
    xn = _rms_rows(x1, nf_ref[...]).astype(BF16)
    acc = jnp.zeros_like(x1)
    for c in range(d_ff // ff_chunk):
        lo = c * ff_chunk
        gate = jnp.dot(xn, win_ref[:, lo:lo + ff_chunk], preferred_element_type=F32)
        up = jnp.dot(xn, win_ref[:, d_ff + lo:d_ff + lo + ff_chunk], preferred_element_type=F32)
        acc = acc + jnp.dot((_silu(gate) * up).astype(BF16), wout_ref[lo:lo + ff_chunk, :],
                            preferred_element_type=F32)
    y_ref[...] = x1 + acc


def _merge_ffn(x2d, a_out, g_out, m_out, gates, w_a, w_g, w_m, w_o, norm_ffn, w_in, w_out, tm):
    n, d = x2d.shape
    d_ff = w_out.shape[0]
    ff_chunk = 2 * LANES
    assert d_ff % ff_chunk == 0
    kern = functools.partial(_merge_ffn_kernel, d_ff=d_ff, ff_chunk=ff_chunk)
    row = lambda w: pl.BlockSpec((tm, w), lambda i: (i, 0))
    return pl.pallas_call(
        kern,
        grid=(n // tm,),
        in_specs=[row(d), row(a_out.shape[1]), row(g_out.shape[1]), row(m_out.shape[1]), row(3 * d),
                  _const_spec(w_a.shape), _const_spec(w_g.shape), _const_spec(w_m.shape), _const_spec(w_o.shape),
                  _const_spec((1, d)), _const_spec(w_in.shape), _const_spec(w_out.shape)],
        out_specs=row(d),
        out_shape=jax.ShapeDtypeStruct((n, d), F32),
        compiler_params=_params(("parallel",)),
        name="merge_ffn",
    )(x2d, a_out, g_out, m_out, gates, w_a, w_g, w_m, w_o, norm_ffn.reshape(1, d), w_in, w_out)


def _tile(n, pref):
    t = min(n, pref)
    assert n % t == 0
    return t


def kernel(x_prompt, x_sample, mem_prompt, cache_k, cache_v, cache_idx_k, page_table, state_gdn, state_conv,
           cache_mem_k, cache_mem_v, norm_mix, w_in, a_q_norm, a_k_norm, g_conv, g_a_log, g_dt_bias, g_o_norm,
           mem_norm, w_mem_kv, m_q_norm, m_k_norm, w_a_out, w_g_out, w_m_out, w_o, norm_ffn, w_ffn_in, w_ffn_out):
    depth = w_in.shape[0]
    b, s, d = x_prompt.shape
    db, t, _ = x_sample.shape
    n_mem = mem_prompt.shape[1]
    kvw = A_KV_HEADS * A_DH
    cch = g_conv.shape[2]
    yp, ys = x_prompt, x_sample
    p_states, s_states = [], []
    for l in range(depth):
        bf = lambda w: w.astype(BF16)
        proj_w = (norm_mix[l], _pack_w_in(w_in[l]), _pack_w_in_t(w_in[l]), a_q_norm[l], a_k_norm[l], m_q_norm[l])
        ffn_w = (bf(w_a_out[l]), bf(w_g_out[l]), bf(w_m_out[l]), bf(w_o[l]), norm_ffn[l], bf(w_ffn_in[l]),
                 bf(w_ffn_out[l]))
        gdn_w = (g_conv[l], g_a_log[l], g_dt_bias[l], g_o_norm[l])

        x2 = yp.reshape(b * s, d)
        tq = _tile(s, 256)
        aqt, iqt, iwt, avt, ak, misc, gqkv, gz, mq, gates = _inproj(x2, *proj_w, tm=tq, seq=s)
        r3 = lambda a: a.reshape(b, s, a.shape[-1])
        a_out = _prompt_attention(aqt, iqt, iwt, r3(ak), avt, r3(misc), tq=tq)
        av = jnp.transpose(avt.reshape(b, A_KV_HEADS, A_DH, s), (0, 3, 1, 2))
        g_out, p_gdn, p_conv = _gdn(r3(gqkv), r3(gz), r3(misc),
                                    jnp.zeros((b, CONV_W - 1, cch), F32),
                                    jnp.zeros((b, G_HEADS, G_DK, G_DV), F32), *gdn_w, n_seq=b)
        mk, mv = _mem_kv(mem_prompt.reshape(b * n_mem, d), mem_norm[l], w_mem_kv[l], m_k_norm[l],
                         tm=_tile(b * n_mem, 256))
        m_out = _mem_attention_wide(r3(mq), mk.reshape(b, n_mem, -1), mv.reshape(b, n_mem, -1), tm=_tile(s, 512))
        y2 = _merge_ffn(x2, a_out.reshape(b * s, -1), g_out.reshape(b * s, -1), m_out.reshape(b * s, -1), gates,
                        *ffn_w, tm=_tile(b * s, 256))
        yp = y2.reshape(b, s, d)
        p_states.append((ak.reshape(b, s, A_KV_HEADS, A_DH), av,
                         misc[:, MISC_IK:MISC_IK + IDX_DH].reshape(b, s, IDX_DH), p_gdn, p_conv,
                         mk.reshape(b, n_mem, M_HEADS, M_DH), mv.reshape(b, n_mem, M_HEADS, M_DH)))

        xs2 = ys.reshape(db * t, d)
        aq, ak, av, iq, misc, gqkv, gz, mq, gates = _inproj(xs2, *proj_w, tm=_tile(db * t, 256))
        r3 = lambda a: a.reshape(db, t, a.shape[-1])
        n_phys, page = cache_idx_k.shape[1], cache_idx_k.shape[2]
        pool_t = lambda c: jnp.transpose(c, (0, 2, 3, 1)).reshape(n_phys, kvw, page)
        a_out = _sample_attention(r3(aq), r3(ak), r3(av), r3(iq), r3(misc), pool_t(cache_k[l]), pool_t(cache_v[l]),
                                  jnp.transpose(cache_idx_k[l], (0, 2, 1)), page_table, n_seq=_tile(db, 4))
        g_out, s_gdn, s_conv = _gdn(r3(gqkv), r3(gz), r3(misc), state_conv[l], state_gdn[l], *gdn_w,
                                    n_seq=_tile(db, 8))
        m_out = _mem_attention(r3(mq), cache_mem_k[l].reshape(db, n_mem * M_HEADS, M_DH),
                               cache_mem_v[l].reshape(db, n_mem * M_HEADS, M_DH), n_seq=_tile(db, 4))
        y2 = _merge_ffn(xs2, a_out.reshape(db * t, -1), g_out.reshape(db * t, -1), m_out.reshape(db * t, -1), gates,
                        *ffn_w, tm=_tile(db * t, 256))
        ys = y2.reshape(db, t, d)
        s_states.append((ak.reshape(db, t, A_KV_HEADS, A_DH), av.reshape(db, t, A_KV_HEADS, A_DH),
                         misc[:, MISC_IK:MISC_IK + IDX_DH].reshape(db, t, IDX_DH), s_gdn, s_conv))

    p_k, p_v, p_idx_k, p_gdn, p_conv, p_mem_k, p_mem_v = [jnp.stack(z) for z in zip(*p_states)]
    s_k, s_v, s_idx_k, s_gdn, s_conv = [jnp.stack(z) for z in zip(*s_states)]
    return (yp, ys, p_k, p_v, p_idx_k, p_gdn, p_conv, p_mem_k, p_mem_v, s_k, s_v, s_idx_k, s_gdn, s_conv)
```

```python
import functools

import numpy as np
import jax
import jax.numpy as jnp
from jax import lax
from jax.experimental import pallas as pl
from jax.experimental.pallas import tpu as pltpu

F32 = jnp.float32
BF16 = jnp.bfloat16
I32 = jnp.int32

A_HEADS = 8
A_KV_HEADS = 4
A_DH = 64
IDX_HEADS = 4
IDX_DH = 64
TOPK_MAX = 256
G_HEADS = 4
G_DK = 128
G_DV = 128
CONV_W = 4
G_CHUNK = 64
M_HEADS = 4
M_DH = 128
EPS = 1e-6

LANES = 128
VMEM_LIMIT = 56 * 1024 * 1024
INT_MIN = np.int32(-2 ** 31)

TOKEN_TILE = 256
ATTN_Q_TILE = 256
MEM_Q_TILE = 512
DECODE_ATTN_SEQS = 4
DECODE_MEM_SEQS = 8
DECODE_GDN_SEQS = 8

MISC_IK = 0
MISC_IW = IDX_DH
MISC_GB = MISC_IW + IDX_HEADS
MISC_GA = MISC_GB + G_HEADS

NN = (((1,), (0,)), ((), ()))
NT = (((1,), (1,)), ((), ()))
TN = (((0,), (0,)), ((), ()))


def _dg(a, b, dn=NN):
    return lax.dot_general(a, b, dn, preferred_element_type=F32)


def _dot1(a, b, dn=NN):
    return _dg(a.astype(BF16), b.astype(BF16), dn)


def _split2(x):
    hi = x.astype(BF16)
    return hi, (x - hi.astype(F32)).astype(BF16)


def _dot3(a, b, dn=NN):
    ah, al = _split2(a)
    bh, bl = _split2(b)
    return _dg(ah, bh, dn) + (_dg(ah, bl, dn) + _dg(al, bh, dn))


def _sigmoid(x):
    return 1.0 / (1.0 + jnp.exp(-x))


def _silu(x):
    return x * _sigmoid(x)


def _softplus(x):
    return jnp.maximum(x, 0.0) + jnp.log1p(jnp.exp(-jnp.abs(x)))


def _rms_rows(x, gain):
    ms = jnp.mean(x * x, axis=-1, keepdims=True)
    return x * lax.rsqrt(ms + EPS) * gain


def _headnorm_pairs(h, gain128):
    rows, width = h.shape
    lo_mask = lax.broadcasted_iota(I32, (rows, LANES), 1) < (LANES // 2)
    outs = []
    for c in range(width // LANES):
        s = h[:, c * LANES:(c + 1) * LANES]
        sq = s * s
        lo = jnp.sum(jnp.where(lo_mask, sq, 0.0), axis=-1, keepdims=True)
        hi = jnp.sum(jnp.where(lo_mask, 0.0, sq), axis=-1, keepdims=True)
        ms = jnp.where(lo_mask, lo, hi) * (2.0 / LANES)
        outs.append(s * lax.rsqrt(ms + EPS) * gain128)
    return outs


def _headnorm_full(h, gain128):
    outs = []
    for c in range(h.shape[1] // LANES):
        outs.append(_rms_rows(h[:, c * LANES:(c + 1) * LANES], gain128))
    return outs


def _const_spec(shape):
    nd = len(shape)
    return pl.BlockSpec(shape, lambda *_: (0,) * nd, pipeline_mode=pl.Buffered(1))


def _params(sem):
    return pltpu.CompilerParams(dimension_semantics=sem, vmem_limit_bytes=VMEM_LIMIT)


_P_AQ = (0, 512)
_P_AK = (512, 768)
_P_AV = (768, 1024)
_P_IQ = (1024, 1280)
_P_MISC = (1280, 1408)
_P_GQKV = (1408, 2944)
_P_GZ = (2944, 3456)
_P_MQ = (3456, 3968)
_P_GATES = (3968, 7040)
_P_WIDTH = 7040


def _pack_w_in(w_in):
    d = w_in.shape[0]
    sizes = (A_HEADS * A_DH, A_KV_HEADS * A_DH, A_KV_HEADS * A_DH, IDX_HEADS * IDX_DH, IDX_DH, IDX_HEADS,
             2 * G_HEADS * G_DK + G_HEADS * G_DV, G_HEADS * G_DV, G_HEADS, G_HEADS, M_HEADS * M_DH)
    offs = np.cumsum((0,) + sizes)
    aq, ak, av, iq, ik, iw, gqkv, gz, gb, ga, mq = (w_in[:, offs[i]:offs[i + 1]] for i in range(len(sizes)))
    gates = w_in[:, offs[-1]:]
    pad = jnp.zeros((d, LANES - (IDX_DH + IDX_HEADS + 2 * G_HEADS)), w_in.dtype)
    misc = jnp.concatenate([ik, iw, gb, ga, pad], axis=1)
    packed = jnp.concatenate([aq, ak, av, iq, misc, gqkv, gz, mq, gates], axis=1)
    assert packed.shape[1] == _P_WIDTH
    return packed.astype(BF16)


_T_AQ = (0, 512)
_T_IQ = (512, 768)
_T_IW = (768, 776)
_T_AV = (776, 1032)
_T_ROWS = 1032


def _pack_w_in_t(w_in):
    sizes = (A_HEADS * A_DH, A_KV_HEADS * A_DH, A_KV_HEADS * A_DH, IDX_HEADS * IDX_DH, IDX_DH, IDX_HEADS)
    offs = np.cumsum((0,) + sizes)
    aq, _, av, iq, _, iw = (w_in[:, offs[i]:offs[i + 1]] for i in range(len(sizes)))
    pad = jnp.zeros((w_in.shape[0], _T_IW[1] - _T_IW[0] - IDX_HEADS), w_in.dtype)
    packed = jnp.concatenate([aq, iq, iw, pad, av], axis=1).T
    assert packed.shape[0] == _T_ROWS
    return packed.astype(BF16)


def _inproj_common(xn, w_ref, akg_ref, mqg_ref, misc_scale_ref, ak_ref, misc_ref, gqkv_ref, gz_ref, mq_ref, gates_ref):
    def proj(rng):
        return jnp.dot(xn, w_ref[:, rng[0]:rng[1]], preferred_element_type=F32)

    for c, y in enumerate(_headnorm_pairs(proj(_P_AK), akg_ref[...])):
        ak_ref[:, c * LANES:(c + 1) * LANES] = y
    misc_ref[...] = proj(_P_MISC) * misc_scale_ref[...]
    gqkv_ref[...] = proj(_P_GQKV)
    gz_ref[...] = proj(_P_GZ)
    for c, y in enumerate(_headnorm_full(proj(_P_MQ), mqg_ref[...])):
        mq_ref[:, c * LANES:(c + 1) * LANES] = y
    gates_ref[...] = _sigmoid(proj(_P_GATES))
    return proj


def _inproj_kernel(x_ref, gain_ref, w_ref, aqg_ref, akg_ref, mqg_ref, misc_scale_ref,
                   aq_ref, ak_ref, av_ref, iq_ref, misc_ref, gqkv_ref, gz_ref, mq_ref, gates_ref):
    xn = _rms_rows(x_ref[...], gain_ref[...]).astype(BF16)
    proj = _inproj_common(xn, w_ref, akg_ref, mqg_ref, misc_scale_ref, ak_ref, misc_ref, gqkv_ref, gz_ref, mq_ref,
                          gates_ref)
    for c, y in enumerate(_headnorm_pairs(proj(_P_AQ), aqg_ref[...])):
        aq_ref[:, c * LANES:(c + 1) * LANES] = y
    av_ref[...] = proj(_P_AV)
    iq_ref[...] = proj(_P_IQ)


def _inproj_t_kernel(x_ref, gain_ref, w_ref, wt_ref, aqg_ref, akg_ref, mqg_ref, misc_scale_ref,
                     aqt_ref, iqt_ref, iwt_ref, avt_ref, ak_ref, misc_ref, gqkv_ref, gz_ref, mq_ref, gates_ref):
    xn = _rms_rows(x_ref[...], gain_ref[...]).astype(BF16)
    _inproj_common(xn, w_ref, akg_ref, mqg_ref, misc_scale_ref, ak_ref, misc_ref, gqkv_ref, gz_ref, mq_ref, gates_ref)

    def proj_t(rng):
        return _dg(wt_ref[rng[0]:rng[1], :], xn, NT)

    aqt = proj_t(_T_AQ)
    gain_col = aqg_ref[...]
    for h in range(A_HEADS):
        qh = aqt[h * A_DH:(h + 1) * A_DH, :]
        ms = jnp.mean(qh * qh, axis=0, keepdims=True)
        aqt_ref[h * A_DH:(h + 1) * A_DH, :] = qh * lax.rsqrt(ms + EPS) * gain_col
    iqt_ref[...] = proj_t(_T_IQ)
    iwt_ref[...] = proj_t(_T_IW) * (IDX_HEADS ** -0.5)
    avt_ref[...] = proj_t(_T_AV)


def _inproj(x2d, norm_mix, w_packed, w_packed_t, a_q_norm, a_k_norm, m_q_norm, tm, seq=None):
    n, d = x2d.shape
    transposed = seq is not None
    lane = np.arange(LANES)
    misc_scale = np.where((lane >= MISC_IW) & (lane < MISC_GB), IDX_HEADS ** -0.5, 1.0).astype(np.float32)[None]
    akg = jnp.tile(a_k_norm.reshape(1, A_DH), (1, 2))
    mqg = m_q_norm.reshape(1, M_DH)
    common = [r[1] - r[0] for r in (_P_MISC, _P_GQKV, _P_GZ, _P_MQ, _P_GATES)]
    row_spec = lambda w: pl.BlockSpec((tm, w), lambda i: (i, 0))
    row_shape = lambda w: jax.ShapeDtypeStruct((n, w), F32)
    x_specs = [row_spec(d), _const_spec((1, d)), _const_spec((d, _P_WIDTH))]
    g_specs = [_const_spec((1, LANES)), _const_spec((1, LANES)), _const_spec((1, LANES))]
    if not transposed:
        widths = [512, 256, 256, 256] + common
        return pl.pallas_call(
            _inproj_kernel,
            grid=(n // tm,),
            in_specs=x_specs + [_const_spec((1, LANES))] + g_specs,
            out_specs=[row_spec(w) for w in widths],
            out_shape=[row_shape(w) for w in widths],
            compiler_params=_params(("parallel",)),
            name="inproj",
        )(x2d, norm_mix.reshape(1, d), w_packed, jnp.tile(a_q_norm.reshape(1, A_DH), (1, 2)), akg, mqg,
          jnp.asarray(misc_scale))
    t_rows = [r[1] - r[0] for r in (_T_AQ, _T_IQ, _T_IW, _T_AV)]
    widths = [256] + common
    tiles = seq // tm
    col_spec = lambda r: pl.BlockSpec((None, r, tm), lambda i: (i // tiles, 0, i % tiles))
    return pl.pallas_call(
        _inproj_t_kernel,
        grid=(n // tm,),
        in_specs=x_specs + [_const_spec((_T_ROWS, d)), _const_spec((A_DH, 1))] + g_specs,
        out_specs=[col_spec(r) for r in t_rows] + [row_spec(w) for w in widths],
        out_shape=[jax.ShapeDtypeStruct((n // seq, r, seq), F32) for r in t_rows] + [row_shape(w) for w in widths],
        compiler_params=_params(("parallel",)),
        name="inproj_t",
    )(x2d, norm_mix.reshape(1, d), w_packed, w_packed_t, a_q_norm.reshape(A_DH, 1), akg, mqg, jnp.asarray(misc_scale))


KEY_NEG_INF = np.int32(-2 ** 31 + 0x7FFFFF)


def _canon_zero(score):
    return jnp.where(score == 0.0, 0.0, score)


def _key_to_float(key):
    key = jnp.maximum(key, KEY_NEG_INF)
    return lax.bitcast_convert_type(key ^ ((key >> 31) & np.int32(0x7FFFFFFF)), F32)


def _as_i32(v):
    return np.int32(v - (1 << 32) if v >= (1 << 31) else v)


def _radix_select(count_fn, shape, bcast, k_top, idx_bits, bits_per_pass):
    thr_f, need, n_eq = _radix_threshold(count_fn, shape, bcast, k_top, bits_per_pass)
    del n_eq
    return thr_f, _tie_cut(count_fn, shape, bcast, thr_f, need, idx_bits, bits_per_pass)


def _radix_threshold(count_fn, shape, bcast, k_top, bits_per_pass):
    thr = jnp.full(shape, INT_MIN, I32)
    for hi in range(32, 0, -bits_per_pass):
        lo = max(hi - bits_per_pass, 0)
        cands = [bcast(_key_to_float(thr ^ _as_i32(v << lo))) for v in range(1, 2 ** (hi - lo))]
        cnts = count_fn([lambda s, c, cf=cf: s >= cf for cf in cands])
        digit = sum(jnp.where(cnt >= k_top, 1, 0) for cnt in cnts)
        thr = thr ^ (digit << lo)
    thr_f = _key_to_float(thr)
    thr_b = bcast(thr_f)
    n_gt, n_eq = count_fn([lambda s, c: s > thr_b, lambda s, c: s == thr_b])
    return thr_f, k_top - n_gt, n_eq


def _tie_cut(count_fn, shape, bcast, thr_f, need, idx_bits, bits_per_pass):
    thr_b = bcast(thr_f)
    cut = jnp.zeros(shape, I32)
    for hi in range(idx_bits, 0, -bits_per_pass):
        lo = max(hi - bits_per_pass, 0)
        cands = [bcast(cut | np.int32(v << lo)) for v in range(1, 2 ** (hi - lo))]
        cnts = count_fn([lambda s, c, cb=cb: (s == thr_b) & (c < cb) for cb in cands])
        digit = sum(jnp.where(cnt < need, 1, 0) for cnt in cnts)
        cut = cut | (digit << lo)
    return cut


def _prompt_attn_kernel(aqt_ref, iqt_ref, iwt_ref, k_ref, vt_ref, misc_ref, out_ref,
                        score_ref, kb_ref, ki_ref, vth_ref, qblk_ref, acc_ref, cut_ref, *, tq, k_top, idx_bits):
    j = pl.program_id(1)
    nk = j + 1
    ck = tq
    n_chunks = score_ref.shape[0]
    sub = 8
    group = A_HEADS // A_KV_HEADS
    head_rows = [slice(h * A_DH, (h + 1) * A_DH) for h in range(A_HEADS)]
    head_cols = [slice(h * tq, (h + 1) * tq) for h in range(A_HEADS)]
    kv_rows = [slice((h // group) * A_DH, (h // group + 1) * A_DH) for h in range(A_HEADS)]

    @pl.when(j == 0)
    def _():
        for c in range(n_chunks):
            rows = slice(c * ck, (c + 1) * ck)
            kb_ref[rows, :] = k_ref[rows, :].astype(BF16)
            ki_ref[rows, :] = misc_ref[rows, MISC_IK:MISC_IK + IDX_DH].astype(BF16)
            vth_ref[c] = vt_ref[:, rows].astype(BF16)
        qblk_ref[...] = jnp.zeros(qblk_ref.shape, BF16)

    key_in_chunk = lax.broadcasted_iota(I32, (ck, tq), 0)
    q_pos = j * tq + lax.broadcasted_iota(I32, (ck, tq), 1)

    iqb = (iqt_ref[...] * (IDX_DH ** -0.5)).astype(BF16)
    iq_cat = jnp.concatenate([iqb[h * IDX_DH:(h + 1) * IDX_DH, :] for h in range(IDX_HEADS)], axis=1)
    iw = iwt_ref[...]

    def idx_chunk(c, diagonal):
        start = pl.multiple_of(c * ck, ck)
        dots = _dg(ki_ref[pl.ds(start, ck), :], iq_cat)
        acc = jnp.zeros((ck, tq), F32)
        for h in range(IDX_HEADS):
            acc = acc + jnp.maximum(dots[:, h * tq:(h + 1) * tq], 0.0) * iw[h:h + 1, :]
        if diagonal:
            acc = jnp.where(c * ck + key_in_chunk <= q_pos, acc, -jnp.inf)
        score_ref[c] = _canon_zero(acc)

    def idx_body(c, carry):
        idx_chunk(c, False)
        return carry

    lax.fori_loop(0, j, idx_body, 0)
    idx_chunk(j, True)
    step = 1
    n_iter = nk

    key_in_group = lax.broadcasted_iota(I32, (sub, tq), 0)
    n_acc = 4

    def count_fn(preds):
        def body(it, cnts):
            cnts = [list(a) for a in cnts]
            for u in range(step):
                c = it * step + u
                for i in range(ck // sub):
                    sc = score_ref[c, i * sub:(i + 1) * sub, :]
                    key_id = c * ck + i * sub + key_in_group
                    for n, pred in enumerate(preds):
                        cnts[n][i % n_acc] = cnts[n][i % n_acc] + jnp.where(pred(sc, key_id), 1.0, 0.0)
            return tuple(tuple(a) for a in cnts)
        zero = tuple(jnp.zeros((sub, tq), F32) for _ in range(n_acc))
        cnts = lax.fori_loop(0, n_iter, body, tuple(zero for _ in preds))
        return [jnp.sum(sum(a), axis=0, keepdims=True) for a in cnts]

    bcast = lambda v: jnp.broadcast_to(v, (sub, tq))
    thr, need, n_eq = _radix_threshold(count_fn, (1, tq), bcast, k_top, 1)
    cut_ref[...] = jnp.full((1, tq), 2 ** idx_bits - 1, I32)

    @pl.when(jnp.max(n_eq - need) > 0.5)
    def _():
        cut_ref[...] = _tie_cut(count_fn, (1, tq), bcast, thr, need, idx_bits, 1)

    cut = cut_ref[...]

    def mask_body(c, carry):
        sc = score_ref[c]
        key_id = c * ck + key_in_chunk
        sel = ((sc > thr) | ((sc == thr) & (key_id <= cut))) & (key_id <= q_pos)
        score_ref[c] = jnp.where(sel, 0.0, -jnp.inf)
        return carry

    lax.fori_loop(0, nk, mask_body, 0)

    qb = (aqt_ref[...] * (A_DH ** -0.5)).astype(BF16)
    for h in range(A_HEADS):
        qblk_ref[kv_rows[h], head_cols[h]] = qb[head_rows[h], :]
    acc_ref[...] = jnp.zeros(acc_ref.shape, F32)

    def chunk_scores(c):
        start = pl.multiple_of(c * ck, ck)
        return _dg(kb_ref[pl.ds(start, ck), :], qblk_ref[...])

    def fold(x, acc, op):
        for i in range(ck // sub):
            acc = op(acc, x[i * sub:(i + 1) * sub, :])
        return acc

    def att_body(c, carry):
        ms, lparts = carry
        s_all = chunk_scores(c)
        bias = score_ref[c]
        scores = [s_all[:, head_cols[h]] + bias for h in range(A_HEADS)]
        m_new = [jnp.maximum(ms[h], jnp.max(fold(scores[h], jnp.full((sub, tq), -1e30, F32), jnp.maximum),
                                            axis=0, keepdims=True)) for h in range(A_HEADS)]
        alphas = [jnp.exp(ms[h] - m_new[h]) for h in range(A_HEADS)]
        probs, new_l = [], []
        for h in range(A_HEADS):
            p = jnp.exp(scores[h] - m_new[h])
            new_l.append(fold(p, alphas[h] * lparts[h], jnp.add))
            probs.append(p.astype(BF16))
        pv = _dg(vth_ref[c], jnp.concatenate(probs, axis=1))
        for h in range(A_HEADS):
            acc_ref[head_rows[h], :] = alphas[h] * acc_ref[head_rows[h], :] + pv[kv_rows[h], head_cols[h]]
        return tuple(m_new), tuple(new_l)

    init = (tuple(jnp.full((1, tq), -1e30, F32) for _ in range(A_HEADS)),
            tuple(jnp.zeros((sub, tq), F32) for _ in range(A_HEADS)))
    _, lparts = lax.fori_loop(0, nk, att_body, init)
    for h in range(A_HEADS):
        acc_ref[head_rows[h], :] = acc_ref[head_rows[h], :] / jnp.sum(lparts[h], axis=0, keepdims=True)
    out_ref[...] = acc_ref[...].T


def _prompt_attention(aqt, iqt, iwt, ak, avt, misc, tq):
    b, s, kvw = ak.shape
    nq = s // tq
    k_top = min(TOPK_MAX, s // 4)
    idx_bits = max(1, int(np.ceil(np.log2(s))))
    kern = functools.partial(_prompt_attn_kernel, tq=tq, k_top=k_top, idx_bits=idx_bits)
    qspec = lambda r: pl.BlockSpec((None, r, tq), lambda bi, j: (bi, 0, j))
    kspec = lambda w: pl.BlockSpec((None, s, w), lambda bi, j: (bi, 0, 0))
    return pl.pallas_call(
        kern,
        grid=(b, nq),
        in_specs=[qspec(A_HEADS * A_DH), qspec(IDX_HEADS * IDX_DH), qspec(_T_IW[1] - _T_IW[0]),
                  kspec(kvw), pl.BlockSpec((None, kvw, s), lambda bi, j: (bi, 0, 0)), kspec(LANES)],
        out_specs=pl.BlockSpec((None, tq, A_HEADS * A_DH), lambda bi, j: (bi, j, 0)),
        out_shape=jax.ShapeDtypeStruct((b, s, A_HEADS * A_DH), F32),
        scratch_shapes=[pltpu.VMEM((nq, tq, tq), F32),
                        pltpu.VMEM((s, kvw), BF16), pltpu.VMEM((s, IDX_DH), BF16),
                        pltpu.VMEM((nq, kvw, tq), BF16), pltpu.VMEM((kvw, A_HEADS * tq), BF16),
                        pltpu.VMEM((A_HEADS * A_DH, tq), F32), pltpu.VMEM((1, tq), I32)],
        compiler_params=_params(("arbitrary", "arbitrary")),
        name="prompt_attn",
    )(aqt, iqt, iwt, ak, avt, misc)


def _sample_attn_kernel(pt_ref, aq_ref, iq_ref, misc_ref, kn_ref, vn_ref, *refs, n_seq, n_pages, page, k_top, idx_bits):
    npg = n_seq * n_pages
    ki_refs = [refs[u * n_pages:(u + 1) * n_pages] for u in range(n_seq)]
    k_refs = [refs[npg + u * n_pages:npg + (u + 1) * n_pages] for u in range(n_seq)]
    v_refs = [refs[2 * npg + u * n_pages:2 * npg + (u + 1) * n_pages] for u in range(n_seq)]
    out_ref = refs[3 * npg]
    cut_ref = refs[3 * npg + 1]
    t = aq_ref.shape[1]
    rows = n_seq * t
    assert page == LANES and t <= LANES
    tok = jnp.concatenate([lax.broadcasted_iota(I32, (t, LANES), 0)] * n_seq, axis=0)
    lane_ids = lax.broadcasted_iota(I32, (rows, LANES), 1)
    new_ok = lane_ids <= tok
    group = A_HEADS // A_KV_HEADS

    page_scores = [[] for _ in range(n_pages + 1)]
    for u in range(n_seq):
        iqb = (iq_ref[u] * (IDX_DH ** -0.5)).astype(BF16)
        q4 = jnp.concatenate([iqb[:, h * IDX_DH:(h + 1) * IDX_DH] for h in range(IDX_HEADS)], axis=0)
        iw = misc_ref[u][:, MISC_IW:MISC_IW + IDX_HEADS]
        new_rows = jnp.concatenate([misc_ref[u], jnp.zeros((LANES - t, LANES), F32)], axis=0)
        dots = [_dg(q4, ki_refs[u][p][...].astype(BF16)) for p in range(n_pages)]
        dots.append(_dg(q4, new_rows[:, MISC_IK:MISC_IK + IDX_DH].astype(BF16), NT))
        for p, d in enumerate(dots):
            acc = jnp.zeros((t, LANES), F32)
            for h in range(IDX_HEADS):
                acc = acc + jnp.maximum(d[h * t:(h + 1) * t, :], 0.0) * iw[:, h:h + 1]
            page_scores[p].append(acc)
    keys = [_canon_zero(jnp.concatenate(ps, axis=0)) for ps in page_scores]
    keys[n_pages] = jnp.where(new_ok, keys[n_pages], -jnp.inf)

    def count_fn(preds):
        cnts = []
        for pred in preds:
            cnt = jnp.zeros((rows, LANES), F32)
            for p, kc in enumerate(keys):
                cnt = cnt + jnp.where(pred(kc, p * LANES + lane_ids), 1.0, 0.0)
            cnts.append(jnp.sum(cnt, axis=-1, keepdims=True))
        return cnts

    bcast = lambda v: jnp.broadcast_to(v, (rows, LANES))
    thr, need, n_eq = _radix_threshold(count_fn, (rows, 1), bcast, k_top, 4)
    cut_ref[...] = jnp.full((rows, 1), 2 ** idx_bits - 1, I32)

    @pl.when(jnp.max(n_eq - need) > 0.5)
    def _():
        cut_ref[...] = _tie_cut(count_fn, (rows, 1), bcast, thr, need, idx_bits, 4)

    cut = cut_ref[...]
    biases = []
    for p in range(n_pages + 1):
        cols = p * LANES + lane_ids
        sel = (keys[p] > thr) | ((keys[p] == thr) & (cols <= cut))
        if p == n_pages:
            sel = sel & new_ok
        biases.append(jnp.where(sel, 0.0, -jnp.inf))

    chs = [slice(g * A_DH, (g + 1) * A_DH) for g in range(A_KV_HEADS)]
    units = [(u, g) for u in range(n_seq) for g in range(A_KV_HEADS)]
    pad_kv = jnp.zeros((LANES - t, A_KV_HEADS * A_DH), F32)
    k_new = [jnp.concatenate([kn_ref[u], pad_kv], axis=0).astype(BF16) for u in range(n_seq)]
    v_new = [jnp.concatenate([vn_ref[u], pad_kv], axis=0).astype(BF16) for u in range(n_seq)]
    scores = []
    for u, g in units:
        qb = (aq_ref[u] * (A_DH ** -0.5)).astype(BF16)
        q_stack = jnp.concatenate([qb[:, (g * group + r) * A_DH:(g * group + r + 1) * A_DH] for r in range(group)], axis=0)
        sg = [_dg(q_stack, k_refs[u][p][chs[g], :].astype(BF16)) for p in range(n_pages)]
        sg.append(_dg(q_stack, k_new[u][:, chs[g]], NT))
        scores.append(sg)
    probs, dens = [], []
    for i, (u, g) in enumerate(units):
        sg = [s + jnp.concatenate([biases[p][u * t:(u + 1) * t, :]] * group, axis=0) for p, s in enumerate(scores[i])]
        mx = sg[0]
        for s in sg[1:]:
            mx = jnp.maximum(mx, s)
        mx = jnp.max(mx, axis=-1, keepdims=True)
        es = [jnp.exp(s - mx) for s in sg]
        dens.append(jnp.sum(sum(es), axis=-1, keepdims=True))
        probs.append([e.astype(BF16) for e in es])
    for i, (u, g) in enumerate(units):
        pvs = [_dg(probs[i][p], v_refs[u][p][chs[g], :].astype(BF16), NT) for p in range(n_pages)]
        pvs.append(_dg(probs[i][n_pages], v_new[u][:, chs[g]]))
        o = sum(pvs) / dens[i]
        for r in range(group):
            h = g * group + r
            out_ref[u, :, h * A_DH:(h + 1) * A_DH] = o[r * t:(r + 1) * t, :]


def _sample_attention(aq, ak, av, iq, misc, pool_kt, pool_vt, pool_kit, page_table, n_seq):
    db, t, _ = aq.shape
    assert db % n_seq == 0
    n_pages = page_table.shape[1]
    page = pool_kit.shape[2]
    total = n_pages * page + t
    k_top = min(TOPK_MAX, total // 4)
    idx_bits = max(1, int(np.ceil(np.log2((n_pages + 1) * LANES))))
    kern = functools.partial(_sample_attn_kernel, n_seq=n_seq, n_pages=n_pages, page=page, k_top=k_top,
                             idx_bits=idx_bits)
    qspec = lambda w: pl.BlockSpec((n_seq, t, w), lambda i, pt: (i, 0, 0))
    pspec = lambda w, u, p: pl.BlockSpec((None, w, page), lambda i, pt: (pt[i * n_seq + u, p], 0, 0))
    kvw = A_KV_HEADS * A_DH
    pages = [(u, p) for u in range(n_seq) for p in range(n_pages)]
    in_specs = [qspec(A_HEADS * A_DH), qspec(IDX_HEADS * IDX_DH), qspec(LANES), qspec(kvw), qspec(kvw)]
    in_specs += [pspec(IDX_DH, u, p) for u, p in pages]
    in_specs += [pspec(kvw, u, p) for u, p in pages]
    in_specs += [pspec(kvw, u, p) for u, p in pages]
    return pl.pallas_call(
        kern,
        grid_spec=pltpu.PrefetchScalarGridSpec(
            num_scalar_prefetch=1, grid=(db // n_seq,), in_specs=in_specs, out_specs=qspec(A_HEADS * A_DH),
            scratch_shapes=[pltpu.VMEM((n_seq * t, 1), I32)]),
        out_shape=jax.ShapeDtypeStruct((db, t, A_HEADS * A_DH), F32),
        compiler_params=_params(("arbitrary",)),
        name="sample_attn",
    )(page_table, aq, iq, misc, ak, av, *([pool_kit] * len(pages)), *([pool_kt] * len(pages)),
      *([pool_vt] * len(pages)))


def _gdn_kernel(u_ref, gz_ref, misc_ref, conv0_ref, s0_ref, convw_ref, alog_ref, dtb_ref, onorm_ref,
                out_ref, s_out_ref, conv_out_ref, state_ref, ubuf_ref, *, chunk, n_seq):
    c = pl.program_id(1)
    n_c = pl.num_programs(1)
    hist = CONV_W - 1
    base = 8

    @pl.when(c == 0)
    def _():
        state_ref[...] = s0_ref[...]
        ubuf_ref[:, base - hist:base, :] = conv0_ref[...]

    ii = lax.broadcasted_iota(I32, (chunk, chunk), 0)
    jj = lax.broadcasted_iota(I32, (chunk, chunk), 1)
    lower = jnp.where(ii >= jj, 1.0, 0.0).astype(BF16)
    upper = jnp.where(ii <= jj, 1.0, 0.0).astype(BF16)
    lane = lax.broadcasted_iota(I32, (chunk, LANES), 1)
    ga_lanes = (lane >= MISC_GA) & (lane < MISC_GA + G_HEADS)
    qk = G_HEADS * G_DK
    n_dbl = int(np.log2(chunk))
    assert 2 ** n_dbl == chunk

    per_seq = []
    for b in range(n_seq):
        ubuf_ref[b, base:base + chunk, :] = u_ref[b]
        conv = 0.0
        for jt in range(CONV_W):
            conv = conv + ubuf_ref[b, base - hist + jt:base - hist + jt + chunk, :] * convw_ref[jt:jt + 1, :]
        act = _silu(conv)
        tail = ubuf_ref[b, base + chunk - hist:base + chunk, :]
        ubuf_ref[b, base - hist:base, :] = tail
        conv_out_ref[b] = tail

        misc = misc_ref[b]
        beta_all = _sigmoid(misc)
        g_all = jnp.where(ga_lanes, -jnp.exp(alog_ref[...]) * _softplus(misc + dtb_ref[...]), 0.0)
        g1 = g_all.astype(BF16)
        r1 = g_all - g1.astype(F32)
        g2 = r1.astype(BF16)
        g3 = (r1 - g2.astype(F32)).astype(BF16)
        gc_col = _dg(lower, g1) + _dg(lower, g2) + _dg(lower, g3)
        gc_row = _dg(g1, upper, TN) + _dg(g2, upper, TN) + _dg(g3, upper, TN)
        per_seq.append((act, beta_all, gc_col, gc_row))

    chains = [(b, h) for b in range(n_seq) for h in range(G_HEADS)]
    qs, ks, kbs, gammas, egs, gcols, sols = [], [], [], [], [], [], []
    for b, h in chains:
        act, beta_all, gc_col, gc_row = per_seq[b]
        q = act[:, h * G_DK:(h + 1) * G_DK]
        k = act[:, qk + h * G_DK:qk + (h + 1) * G_DK]
        v = act[:, 2 * qk + h * G_DV:2 * qk + (h + 1) * G_DV]
        q = q * lax.rsqrt(jnp.sum(q * q, axis=-1, keepdims=True) + EPS) * (G_DK ** -0.5)
        k = k * lax.rsqrt(jnp.sum(k * k, axis=-1, keepdims=True) + EPS)
        beta = beta_all[:, MISC_GB + h:MISC_GB + h + 1]
        gcol = gc_col[:, MISC_GA + h:MISC_GA + h + 1]
        grow = gc_row[MISC_GA + h:MISC_GA + h + 1, :]
        eg = jnp.exp(gcol)
        kb = k * beta
        qs.append(q)
        ks.append(k)
        kbs.append(kb)
        gammas.append(jnp.exp(jnp.where(ii >= jj, gcol - grow, -jnp.inf)))
        egs.append(eg)
        gcols.append(gcol)
        sols.append(jnp.concatenate([v * beta, kb * eg], axis=-1))
    n_ch = len(chains)
    kk = [_dot1(kbs[i], ks[i], NT) for i in range(n_ch)]
    pws = [jnp.where(ii > jj, -(kk[i] * gammas[i]), 0.0) for i in range(n_ch)]
    for it in range(n_dbl):
        upd = [_dot3(pws[i], sols[i]) for i in range(n_ch)]
        if it + 1 < n_dbl:
            pws = [_dot1(pws[i], pws[i]) for i in range(n_ch)]
        sols = [sols[i] + upd[i] for i in range(n_ch)]
    aqk = [_dot1(qs[i], ks[i], NT) for i in range(n_ch)]
    sts = [state_ref[b, h] for b, h in chains]
    ws = [_dot1(sols[i][:, G_DV:], sts[i]) for i in range(n_ch)]
    o1 = [_dot1(qs[i] * egs[i], sts[i]) for i in range(n_ch)]
    v_new = [sols[i][:, :G_DV] - ws[i] for i in range(n_ch)]
    o2 = [_dot1(aqk[i] * gammas[i], v_new[i]) for i in range(n_ch)]
    g_last = [gcols[i][chunk - 1:chunk, :] for i in range(n_ch)]
    kv = [_dot1(ks[i] * jnp.exp(g_last[i] - gcols[i]), v_new[i], TN) for i in range(n_ch)]
    for i, (b, h) in enumerate(chains):
        state_ref[b, h] = sts[i] * jnp.exp(g_last[i]) + kv[i]
        o = _rms_rows(o1[i] + o2[i], onorm_ref[...]) * _silu(gz_ref[b, :, h * G_DV:(h + 1) * G_DV])
        out_ref[b, :, h * G_DV:(h + 1) * G_DV] = o

    @pl.when(c == n_c - 1)
    def _():
        s_out_ref[...] = state_ref[...]


def _gdn(gqkv, gz, misc, conv0, s0, g_conv, g_a_log, g_dt_bias, g_o_norm, n_seq):
    b, t, cch = gqkv.shape
    chunk = min(G_CHUNK, t)
    assert t % chunk == 0 and chunk % 8 == 0 and b % n_seq == 0
    hist = CONV_W - 1
    alog = jnp.zeros((1, LANES), F32).at[0, MISC_GA:MISC_GA + G_HEADS].set(g_a_log.astype(F32))
    dtb = jnp.zeros((1, LANES), F32).at[0, MISC_GA:MISC_GA + G_HEADS].set(g_dt_bias.astype(F32))
    kern = functools.partial(_gdn_kernel, chunk=chunk, n_seq=n_seq)
    tspec = lambda w: pl.BlockSpec((n_seq, chunk, w), lambda bi, c: (bi, c, 0))
    sspec = pl.BlockSpec((n_seq, G_HEADS, G_DK, G_DV), lambda bi, c: (bi, 0, 0, 0))
    cspec = pl.BlockSpec((n_seq, hist, cch), lambda bi, c: (bi, 0, 0))
    return pl.pallas_call(
        kern,
        grid=(b // n_seq, t // chunk),
        in_specs=[tspec(cch), tspec(G_HEADS * G_DV), tspec(LANES), cspec, sspec,
                  _const_spec((CONV_W, cch)), _const_spec((1, LANES)), _const_spec((1, LANES)),
                  _const_spec((1, G_DV))],
        out_specs=[tspec(G_HEADS * G_DV), sspec, cspec],
        out_shape=[jax.ShapeDtypeStruct((b, t, G_HEADS * G_DV), F32),
                   jax.ShapeDtypeStruct((b, G_HEADS, G_DK, G_DV), F32),
                   jax.ShapeDtypeStruct((b, hist, cch), F32)],
        scratch_shapes=[pltpu.VMEM((n_seq, G_HEADS, G_DK, G_DV), F32), pltpu.VMEM((n_seq, 8 + chunk, cch), F32)],
        compiler_params=_params(("arbitrary", "arbitrary")),
        name="gdn",
    )(gqkv, gz, misc, conv0, s0, g_conv, alog, dtb, g_o_norm.reshape(1, G_DV))


def _mem_kv_kernel(mem_ref, gain_ref, w_ref, kg_ref, mk_ref, mv_ref):
    xn = _rms_rows(mem_ref[...], gain_ref[...]).astype(BF16)
    half = M_HEADS * M_DH
    kk = jnp.dot(xn, w_ref[:, :half], preferred_element_type=F32)
    for c, y in enumerate(_headnorm_full(kk, kg_ref[...])):
        mk_ref[:, c * LANES:(c + 1) * LANES] = y
    mv_ref[...] = jnp.dot(xn, w_ref[:, half:], preferred_element_type=F32)


def _mem_kv(mem2d, mem_norm, w_mem_kv, m_k_norm, tm):
    n, d = mem2d.shape
    half = M_HEADS * M_DH
    return pl.pallas_call(
        _mem_kv_kernel,
        grid=(n // tm,),
        in_specs=[pl.BlockSpec((tm, d), lambda i: (i, 0)), _const_spec((1, d)), _const_spec((d, 2 * half)),
                  _const_spec((1, M_DH))],
        out_specs=[pl.BlockSpec((tm, half), lambda i: (i, 0))] * 2,
        out_shape=[jax.ShapeDtypeStruct((n, half), F32)] * 2,
        compiler_params=_params(("parallel",)),
        name="mem_kv",
    )(mem2d, mem_norm.reshape(1, d), w_mem_kv.astype(BF16), m_k_norm.reshape(1, M_DH))


def _mem_attn_kernel(q_ref, k_ref, v_ref, out_ref):
    n_seq, t, _ = q_ref.shape
    n_rows = k_ref.shape[1]
    q_head = jnp.concatenate([jnp.full((t, n_rows), h, I32) for h in range(M_HEADS)], axis=0)
    row_id = lax.broadcasted_iota(I32, (M_HEADS * t, n_rows), 1)
    row_head = row_id & (M_HEADS - 1) if M_HEADS & (M_HEADS - 1) == 0 else row_id % M_HEADS
    own = q_head == row_head
    q_stacks = [jnp.concatenate([q_ref[u, :, h * M_DH:(h + 1) * M_DH] for h in range(M_HEADS)], axis=0).astype(BF16)
                for u in range(n_seq)]
    scores = [_dg(q_stacks[u], k_ref[u].astype(BF16), NT) for u in range(n_seq)]
    probs = []
    for u in range(n_seq):
        s = jnp.where(own, scores[u] * (M_DH ** -0.5), -jnp.inf)
        e = jnp.exp(s - jnp.max(s, axis=-1, keepdims=True))
        probs.append((e / jnp.sum(e, axis=-1, keepdims=True)).astype(BF16))
    outs = [_dg(probs[u], v_ref[u].astype(BF16)) for u in range(n_seq)]
    for u in range(n_seq):
        for h in range(M_HEADS):
            out_ref[u, :, h * M_DH:(h + 1) * M_DH] = outs[u][h * t:(h + 1) * t, :]


def _mem_attn_wide_kernel(q_ref, k_ref, v_ref, out_ref, kblk_ref, vblk_ref):
    n_mem = k_ref.shape[0]

    @pl.when(pl.program_id(1) == 0)
    def _():
        kblk_ref[...] = jnp.zeros(kblk_ref.shape, BF16)
        vblk_ref[...] = jnp.zeros(vblk_ref.shape, BF16)
        for h in range(M_HEADS):
            sl = slice(h * M_DH, (h + 1) * M_DH)
            kblk_ref[h * n_mem:(h + 1) * n_mem, sl] = k_ref[:, sl].astype(BF16)
            vblk_ref[h * n_mem:(h + 1) * n_mem, sl] = v_ref[:, sl].astype(BF16)

    s_all = _dg(q_ref[...].astype(BF16), kblk_ref[...], NT) * (M_DH ** -0.5)
    probs = []
    for h in range(M_HEADS):
        s = s_all[:, h * n_mem:(h + 1) * n_mem]
        e = jnp.exp(s - jnp.max(s, axis=-1, keepdims=True))
        probs.append((e / jnp.sum(e, axis=-1, keepdims=True)).astype(BF16))
    out_ref[...] = _dg(jnp.concatenate(probs, axis=1), vblk_ref[...])


def _mem_attention_wide(mq, mk, mv, tm):
    b, t, w = mq.shape
    m = mk.shape[1]
    return pl.pallas_call(
        _mem_attn_wide_kernel,
        grid=(b, t // tm),
        in_specs=[pl.BlockSpec((None, tm, w), lambda bi, i: (bi, i, 0)),
                  pl.BlockSpec((None, m, w), lambda bi, i: (bi, 0, 0)),
                  pl.BlockSpec((None, m, w), lambda bi, i: (bi, 0, 0))],
        out_specs=pl.BlockSpec((None, tm, w), lambda bi, i: (bi, i, 0)),
        out_shape=jax.ShapeDtypeStruct((b, t, w), F32),
        scratch_shapes=[pltpu.VMEM((M_HEADS * m, w), BF16), pltpu.VMEM((M_HEADS * m, w), BF16)],
        compiler_params=_params(("arbitrary", "arbitrary")),
        name="mem_attn_wide",
    )(mq, mk, mv)


def _mem_attention(mq, mk, mv, n_seq):
    b, t, w = mq.shape
    m, kw = mk.shape[1], mk.shape[2]
    assert b % n_seq == 0
    return pl.pallas_call(
        _mem_attn_kernel,
        grid=(b // n_seq,),
        in_specs=[pl.BlockSpec((n_seq, t, w), lambda i: (i, 0, 0)),
                  pl.BlockSpec((n_seq, m, kw), lambda i: (i, 0, 0)),
                  pl.BlockSpec((n_seq, m, kw), lambda i: (i, 0, 0))],
        out_specs=pl.BlockSpec((n_seq, t, w), lambda i: (i, 0, 0)),
        out_shape=jax.ShapeDtypeStruct((b, t, w), F32),
        compiler_params=_params(("parallel",)),
        name="mem_attn",
    )(mq, mk, mv)


def _merge_ffn_kernel(x_ref, a_ref, g_ref, m_ref, gates_ref, wa_ref, wg_ref, wm_ref, wo_ref, nf_ref,
                      win_ref, wout_ref, y_ref, *, d_ff, ff_chunk):
    d = x_ref.shape[1]
    gates = gates_ref[...]
    h = (gates[:, :d] * jnp.dot(a_ref[...].astype(BF16), wa_ref[...], preferred_element_type=F32)
         + gates[:, d:2 * d] * jnp.dot(g_ref[...].astype(BF16), wg_ref[...], preferred_element_type=F32)
         + gates[:, 2 * d:] * jnp.dot(m_ref[...].astype(BF16), wm_ref[...], preferred_element_type=F32))
    x1 = x_ref[...] + jnp.dot(h.astype(BF16), wo_ref[...], preferred_element_type=F32)
    xn = _rms_rows(x1, nf_ref[...]).astype(BF16)
    acc = jnp.zeros_like(x1)
    for c in range(d_ff // ff_chunk):
        lo = c * ff_chunk
        gate = jnp.dot(xn, win_ref[:, lo:lo + ff_chunk], preferred_element_type=F32)
        up = jnp.dot(xn, win_ref[:, d_ff + lo:d_ff + lo + ff_chunk], preferred_element_type=F32)
        acc = acc + jnp.dot((_silu(gate) * up).astype(BF16), wout_ref[lo:lo + ff_chunk, :],
                            preferred_element_type=F32)
    y_ref[...] = x1 + acc


def _merge_ffn(x2d, a_out, g_out, m_out, gates, w_a, w_g, w_m, w_o, norm_ffn, w_in, w_out, tm):
    n, d = x2d.shape
    d_ff = w_out.shape[0]
    ff_chunk = 2 * LANES
    assert d_ff % ff_chunk == 0
    kern = functools.partial(_merge_ffn_kernel, d_ff=d_ff, ff_chunk=ff_chunk)
    row = lambda w: pl.BlockSpec((tm, w), lambda i: (i, 0))
    return pl.pallas_call(
        kern,
        grid=(n // tm,),
        in_specs=[row(d), row(a_out.shape[1]), row(g_out.shape[1]), row(m_out.shape[1]), row(3 * d),
                  _const_spec(w_a.shape), _const_spec(w_g.shape), _const_spec(w_m.shape), _const_spec(w_o.shape),
                  _const_spec((1, d)), _const_spec(w_in.shape), _const_spec(w_out.shape)],
        out_specs=row(d),
        out_shape=jax.ShapeDtypeStruct((n, d), F32),
        compiler_params=_params(("parallel",)),
        name="merge_ffn",
    )(x2d, a_out, g_out, m_out, gates, w_a, w_g, w_m, w_o, norm_ffn.reshape(1, d), w_in, w_out)


def _tile(n, pref):
    t = min(n, pref)
    assert n % t == 0
    return t


def kernel(x_prompt, x_sample, mem_prompt, cache_k, cache_v, cache_idx_k, page_table, state_gdn, state_conv,
           cache_mem_k, cache_mem_v, norm_mix, w_in, a_q_norm, a_k_norm, g_conv, g_a_log, g_dt_bias, g_o_norm,
           mem_norm, w_mem_kv, m_q_norm, m_k_norm, w_a_out, w_g_out, w_m_out, w_o, norm_ffn, w_ffn_in, w_ffn_out):
    depth = w_in.shape[0]
    b, s, d = x_prompt.shape
    db, t, _ = x_sample.shape
    n_mem = mem_prompt.shape[1]
    kvw = A_KV_HEADS * A_DH
    cch = g_conv.shape[2]
    yp, ys = x_prompt, x_sample
    p_states, s_states = [], []
    for l in range(depth):
        bf = lambda w: w.astype(BF16)
        proj_w = (norm_mix[l], _pack_w_in(w_in[l]), _pack_w_in_t(w_in[l]), a_q_norm[l], a_k_norm[l], m_q_norm[l])
        ffn_w = (bf(w_a_out[l]), bf(w_g_out[l]), bf(w_m_out[l]), bf(w_o[l]), norm_ffn[l], bf(w_ffn_in[l]),
                 bf(w_ffn_out[l]))
        gdn_w = (g_conv[l], g_a_log[l], g_dt_bias[l], g_o_norm[l])

        x2 = yp.reshape(b * s, d)
        tq = _tile(s, ATTN_Q_TILE)
        aqt, iqt, iwt, avt, ak, misc, gqkv, gz, mq, gates = _inproj(x2, *proj_w, tm=tq, seq=s)
        r3 = lambda a: a.reshape(b, s, a.shape[-1])
        a_out = _prompt_attention(aqt, iqt, iwt, r3(ak), avt, r3(misc), tq=tq)
        av = jnp.transpose(avt.reshape(b, A_KV_HEADS, A_DH, s), (0, 3, 1, 2))
        g_out, p_gdn, p_conv = _gdn(r3(gqkv), r3(gz), r3(misc),
                                    jnp.zeros((b, CONV_W - 1, cch), F32),
                                    jnp.zeros((b, G_HEADS, G_DK, G_DV), F32), *gdn_w, n_seq=b)
        mk, mv = _mem_kv(mem_prompt.reshape(b * n_mem, d), mem_norm[l], w_mem_kv[l], m_k_norm[l],
                         tm=_tile(b * n_mem, TOKEN_TILE))
        m_out = _mem_attention_wide(r3(mq), mk.reshape(b, n_mem, -1), mv.reshape(b, n_mem, -1),
                                    tm=_tile(s, MEM_Q_TILE))
        y2 = _merge_ffn(x2, a_out.reshape(b * s, -1), g_out.reshape(b * s, -1), m_out.reshape(b * s, -1), gates,
                        *ffn_w, tm=_tile(b * s, TOKEN_TILE))
        yp = y2.reshape(b, s, d)
        p_states.append((ak.reshape(b, s, A_KV_HEADS, A_DH), av,
                         misc[:, MISC_IK:MISC_IK + IDX_DH].reshape(b, s, IDX_DH), p_gdn, p_conv,
                         mk.reshape(b, n_mem, M_HEADS, M_DH), mv.reshape(b, n_mem, M_HEADS, M_DH)))

        xs2 = ys.reshape(db * t, d)
        aq, ak, av, iq, misc, gqkv, gz, mq, gates = _inproj(xs2, *proj_w, tm=_tile(db * t, TOKEN_TILE))
        r3 = lambda a: a.reshape(db, t, a.shape[-1])
        n_phys, page = cache_idx_k.shape[1], cache_idx_k.shape[2]
        pool_t = lambda c: jnp.transpose(c, (0, 2, 3, 1)).reshape(n_phys, kvw, page)
        a_out = _sample_attention(r3(aq), r3(ak), r3(av), r3(iq), r3(misc), pool_t(cache_k[l]), pool_t(cache_v[l]),
                                  jnp.transpose(cache_idx_k[l], (0, 2, 1)), page_table,
                                  n_seq=_tile(db, DECODE_ATTN_SEQS))
        g_out, s_gdn, s_conv = _gdn(r3(gqkv), r3(gz), r3(misc), state_conv[l], state_gdn[l], *gdn_w,
                                    n_seq=_tile(db, DECODE_GDN_SEQS))
        m_out = _mem_attention(r3(mq), cache_mem_k[l].reshape(db, n_mem * M_HEADS, M_DH),
                               cache_mem_v[l].reshape(db, n_mem * M_HEADS, M_DH), n_seq=_tile(db, DECODE_MEM_SEQS))
        y2 = _merge_ffn(xs2, a_out.reshape(db * t, -1), g_out.reshape(db * t, -1), m_out.reshape(db * t, -1), gates,
                        *ffn_w, tm=_tile(db * t, TOKEN_TILE))
        ys = y2.reshape(db, t, d)
        s_states.append((ak.reshape(db, t, A_KV_HEADS, A_DH), av.reshape(db, t, A_KV_HEADS, A_DH),
                         misc[:, MISC_IK:MISC_IK + IDX_DH].reshape(db, t, IDX_DH), s_gdn, s_conv))

    p_k, p_v, p_idx_k, p_gdn, p_conv, p_mem_k, p_mem_v = [jnp.stack(z) for z in zip(*p_states)]
    s_k, s_v, s_idx_k, s_gdn, s_conv = [jnp.stack(z) for z in zip(*s_states)]
    return (yp, ys, p_k, p_v, p_idx_k, p_gdn, p_conv, p_mem_k, p_mem_v, s_k, s_v, s_idx_k, s_gdn, s_conv)
```

```python
import functools

import numpy as np
import jax
import jax.numpy as jnp
from jax import lax
from jax.experimental import pallas as pl
from jax.experimental.pallas import tpu as pltpu

F32 = jnp.float32
BF16 = jnp.bfloat16
I32 = jnp.int32

A_HEADS = 8
A_KV_HEADS = 4
A_DH = 64
IDX_HEADS = 4
IDX_DH = 64
TOPK_MAX = 256
G_HEADS = 4
G_DK = 128
G_DV = 128
CONV_W = 4
G_CHUNK = 64
M_HEADS = 4
M_DH = 128
EPS = 1e-6

LANES = 128
VMEM_LIMIT = 56 * 1024 * 1024
INT_MIN = np.int32(-2 ** 31)

TOKEN_TILE = 256
ATTN_Q_TILE = 512
MEM_Q_TILE = 512
DECODE_ATTN_SEQS = 4
DECODE_MEM_SEQS = 8
DECODE_GDN_SEQS = 8

MISC_IK = 0
MISC_IW = IDX_DH
MISC_GB = MISC_IW + IDX_HEADS
MISC_GA = MISC_GB + G_HEADS

NN = (((1,), (0,)), ((), ()))
NT = (((1,), (1,)), ((), ()))
TN = (((0,), (0,)), ((), ()))


def _dg(a, b, dn=NN):
    return lax.dot_general(a, b, dn, preferred_element_type=F32)


def _dot1(a, b, dn=NN):
    return _dg(a.astype(BF16), b.astype(BF16), dn)


def _split2(x):
    hi = x.astype(BF16)
    return hi, (x - hi.astype(F32)).astype(BF16)


def _dot3(a, b, dn=NN):
    ah, al = _split2(a)
    bh, bl = _split2(b)
    return _dg(ah, bh, dn) + (_dg(ah, bl, dn) + _dg(al, bh, dn))


def _sigmoid(x):
    return 1.0 / (1.0 + jnp.exp(-x))


def _silu(x):
    return x * _sigmoid(x)


def _softplus(x):
    return jnp.maximum(x, 0.0) + jnp.log1p(jnp.exp(-jnp.abs(x)))


def _rms_rows(x, gain):
    ms = jnp.mean(x * x, axis=-1, keepdims=True)
    return x * lax.rsqrt(ms + EPS) * gain


def _headnorm_pairs(h, gain128):
    rows, width = h.shape
    lo_mask = lax.broadcasted_iota(I32, (rows, LANES), 1) < (LANES // 2)
    outs = []
    for c in range(width // LANES):
        s = h[:, c * LANES:(c + 1) * LANES]
        sq = s * s
        lo = jnp.sum(jnp.where(lo_mask, sq, 0.0), axis=-1, keepdims=True)
        hi = jnp.sum(jnp.where(lo_mask, 0.0, sq), axis=-1, keepdims=True)
        ms = jnp.where(lo_mask, lo, hi) * (2.0 / LANES)
        outs.append(s * lax.rsqrt(ms + EPS) * gain128)
    return outs


def _headnorm_full(h, gain128):
    outs = []
    for c in range(h.shape[1] // LANES):
        outs.append(_rms_rows(h[:, c * LANES:(c + 1) * LANES], gain128))
    return outs


def _const_spec(shape):
    nd = len(shape)
    return pl.BlockSpec(shape, lambda *_: (0,) * nd, pipeline_mode=pl.Buffered(1))


def _params(sem):
    return pltpu.CompilerParams(dimension_semantics=sem, vmem_limit_bytes=VMEM_LIMIT)


_P_AQ = (0, 512)
_P_AK = (512, 768)
_P_AV = (768, 1024)
_P_IQ = (1024, 1280)
_P_MISC = (1280, 1408)
_P_GQKV = (1408, 2944)
_P_GZ = (2944, 3456)
_P_MQ = (3456, 3968)
_P_GATES = (3968, 7040)
_P_WIDTH = 7040


def _pack_w_in(w_in):
    d = w_in.shape[0]
    sizes = (A_HEADS * A_DH, A_KV_HEADS * A_DH, A_KV_HEADS * A_DH, IDX_HEADS * IDX_DH, IDX_DH, IDX_HEADS,
             2 * G_HEADS * G_DK + G_HEADS * G_DV, G_HEADS * G_DV, G_HEADS, G_HEADS, M_HEADS * M_DH)
    offs = np.cumsum((0,) + sizes)
    aq, ak, av, iq, ik, iw, gqkv, gz, gb, ga, mq = (w_in[:, offs[i]:offs[i + 1]] for i in range(len(sizes)))
    gates = w_in[:, offs[-1]:]
    pad = jnp.zeros((d, LANES - (IDX_DH + IDX_HEADS + 2 * G_HEADS)), w_in.dtype)
    misc = jnp.concatenate([ik, iw, gb, ga, pad], axis=1)
    packed = jnp.concatenate([aq, ak, av, iq, misc, gqkv, gz, mq, gates], axis=1)
    assert packed.shape[1] == _P_WIDTH
    return packed.astype(BF16)


_T_AQ = (0, 512)
_T_IQ = (512, 768)
_T_IW = (768, 776)
_T_AV = (776, 1032)
_T_ROWS = 1032


def _pack_w_in_t(w_in):
    sizes = (A_HEADS * A_DH, A_KV_HEADS * A_DH, A_KV_HEADS * A_DH, IDX_HEADS * IDX_DH, IDX_DH, IDX_HEADS)
    offs = np.cumsum((0,) + sizes)
    aq, _, av, iq, _, iw = (w_in[:, offs[i]:offs[i + 1]] for i in range(len(sizes)))
    pad = jnp.zeros((w_in.shape[0], _T_IW[1] - _T_IW[0] - IDX_HEADS), w_in.dtype)
    packed = jnp.concatenate([aq, iq, iw, pad, av], axis=1).T
    assert packed.shape[0] == _T_ROWS
    return packed.astype(BF16)


def _inproj_common(xn, w_ref, akg_ref, mqg_ref, misc_scale_ref, ak_ref, misc_ref, gqkv_ref, gz_ref, mq_ref, gates_ref):
    def proj(rng):
        return jnp.dot(xn, w_ref[:, rng[0]:rng[1]], preferred_element_type=F32)

    for c, y in enumerate(_headnorm_pairs(proj(_P_AK), akg_ref[...])):
        ak_ref[:, c * LANES:(c + 1) * LANES] = y
    misc_ref[...] = proj(_P_MISC) * misc_scale_ref[...]
    gqkv_ref[...] = proj(_P_GQKV)
    gz_ref[...] = proj(_P_GZ)
    for c, y in enumerate(_headnorm_full(proj(_P_MQ), mqg_ref[...])):
        mq_ref[:, c * LANES:(c + 1) * LANES] = y
    gates_ref[...] = _sigmoid(proj(_P_GATES))
    return proj


def _inproj_kernel(x_ref, gain_ref, w_ref, aqg_ref, akg_ref, mqg_ref, misc_scale_ref,
                   aq_ref, ak_ref, av_ref, iq_ref, misc_ref, gqkv_ref, gz_ref, mq_ref, gates_ref):
    xn = _rms_rows(x_ref[...], gain_ref[...]).astype(BF16)
    proj = _inproj_common(xn, w_ref, akg_ref, mqg_ref, misc_scale_ref, ak_ref, misc_ref, gqkv_ref, gz_ref, mq_ref,
                          gates_ref)
    for c, y in enumerate(_headnorm_pairs(proj(_P_AQ), aqg_ref[...])):
        aq_ref[:, c * LANES:(c + 1) * LANES] = y
    av_ref[...] = proj(_P_AV)
    iq_ref[...] = proj(_P_IQ)


def _inproj_t_kernel(x_ref, gain_ref, w_ref, wt_ref, aqg_ref, akg_ref, mqg_ref, misc_scale_ref,
                     aqt_ref, iqt_ref, iwt_ref, avt_ref, ak_ref, misc_ref, gqkv_ref, gz_ref, mq_ref, gates_ref):
    xn = _rms_rows(x_ref[...], gain_ref[...]).astype(BF16)
    _inproj_common(xn, w_ref, akg_ref, mqg_ref, misc_scale_ref, ak_ref, misc_ref, gqkv_ref, gz_ref, mq_ref, gates_ref)

    def proj_t(rng):
        return _dg(wt_ref[rng[0]:rng[1], :], xn, NT)

    aqt = proj_t(_T_AQ)
    gain_col = aqg_ref[...]
    for h in range(A_HEADS):
        qh = aqt[h * A_DH:(h + 1) * A_DH, :]
        ms = jnp.mean(qh * qh, axis=0, keepdims=True)
        aqt_ref[h * A_DH:(h + 1) * A_DH, :] = qh * lax.rsqrt(ms + EPS) * gain_col
    iqt_ref[...] = proj_t(_T_IQ)
    iwt_ref[...] = proj_t(_T_IW) * (IDX_HEADS ** -0.5)
    avt_ref[...] = proj_t(_T_AV)


def _inproj(x2d, norm_mix, w_packed, w_packed_t, a_q_norm, a_k_norm, m_q_norm, tm, seq=None):
    n, d = x2d.shape
    transposed = seq is not None
    lane = np.arange(LANES)
    misc_scale = np.where((lane >= MISC_IW) & (lane < MISC_GB), IDX_HEADS ** -0.5, 1.0).astype(np.float32)[None]
    akg = jnp.tile(a_k_norm.reshape(1, A_DH), (1, 2))
    mqg = m_q_norm.reshape(1, M_DH)
    common = [r[1] - r[0] for r in (_P_MISC, _P_GQKV, _P_GZ, _P_MQ, _P_GATES)]
    row_spec = lambda w: pl.BlockSpec((tm, w), lambda i: (i, 0))
    row_shape = lambda w: jax.ShapeDtypeStruct((n, w), F32)
    x_specs = [row_spec(d), _const_spec((1, d)), _const_spec((d, _P_WIDTH))]
    g_specs = [_const_spec((1, LANES)), _const_spec((1, LANES)), _const_spec((1, LANES))]
    if not transposed:
        widths = [512, 256, 256, 256] + common
        return pl.pallas_call(
            _inproj_kernel,
            grid=(n // tm,),
            in_specs=x_specs + [_const_spec((1, LANES))] + g_specs,
            out_specs=[row_spec(w) for w in widths],
            out_shape=[row_shape(w) for w in widths],
            compiler_params=_params(("parallel",)),
            name="inproj",
        )(x2d, norm_mix.reshape(1, d), w_packed, jnp.tile(a_q_norm.reshape(1, A_DH), (1, 2)), akg, mqg,
          jnp.asarray(misc_scale))
    t_rows = [r[1] - r[0] for r in (_T_AQ, _T_IQ, _T_IW, _T_AV)]
    widths = [256] + common
    tiles = seq // tm
    col_spec = lambda r: pl.BlockSpec((None, r, tm), lambda i: (i // tiles, 0, i % tiles))
    return pl.pallas_call(
        _inproj_t_kernel,
        grid=(n // tm,),
        in_specs=x_specs + [_const_spec((_T_ROWS, d)), _const_spec((A_DH, 1))] + g_specs,
        out_specs=[col_spec(r) for r in t_rows] + [row_spec(w) for w in widths],
        out_shape=[jax.ShapeDtypeStruct((n // seq, r, seq), F32) for r in t_rows] + [row_shape(w) for w in widths],
        compiler_params=_params(("parallel",)),
        name="inproj_t",
    )(x2d, norm_mix.reshape(1, d), w_packed, w_packed_t, a_q_norm.reshape(A_DH, 1), akg, mqg, jnp.asarray(misc_scale))


KEY_NEG_INF = np.int32(-2 ** 31 + 0x7FFFFF)


def _canon_zero(score):
    return jnp.where(score == 0.0, 0.0, score)


def _key_to_float(key):
    key = jnp.maximum(key, KEY_NEG_INF)
    return lax.bitcast_convert_type(key ^ ((key >> 31) & np.int32(0x7FFFFFFF)), F32)


def _as_i32(v):
    return np.int32(v - (1 << 32) if v >= (1 << 31) else v)


def _radix_select(count_fn, shape, bcast, k_top, idx_bits, bits_per_pass):
    thr_f, need, n_eq = _radix_threshold(count_fn, shape, bcast, k_top, bits_per_pass)
    del n_eq
    return thr_f, _tie_cut(count_fn, shape, bcast, thr_f, need, idx_bits, bits_per_pass)


def _radix_threshold(count_fn, shape, bcast, k_top, bits_per_pass):
    thr = jnp.full(shape, INT_MIN, I32)
    for hi in range(32, 0, -bits_per_pass):
        lo = max(hi - bits_per_pass, 0)
        cands = [bcast(_key_to_float(thr ^ _as_i32(v << lo))) for v in range(1, 2 ** (hi - lo))]
        cnts = count_fn([lambda s, c, cf=cf: s >= cf for cf in cands])
        digit = sum(jnp.where(cnt >= k_top, 1, 0) for cnt in cnts)
        thr = thr ^ (digit << lo)
    thr_f = _key_to_float(thr)
    thr_b = bcast(thr_f)
    n_gt, n_eq = count_fn([lambda s, c: s > thr_b, lambda s, c: s == thr_b])
    return thr_f, k_top - n_gt, n_eq


def _tie_cut(count_fn, shape, bcast, thr_f, need, idx_bits, bits_per_pass):
    thr_b = bcast(thr_f)
    cut = jnp.zeros(shape, I32)
    for hi in range(idx_bits, 0, -bits_per_pass):
        lo = max(hi - bits_per_pass, 0)
        cands = [bcast(cut | np.int32(v << lo)) for v in range(1, 2 ** (hi - lo))]
        cnts = count_fn([lambda s, c, cb=cb: (s == thr_b) & (c < cb) for cb in cands])
        digit = sum(jnp.where(cnt < need, 1, 0) for cnt in cnts)
        cut = cut | (digit << lo)
    return cut


def _prompt_attn_kernel(aqt_ref, iqt_ref, iwt_ref, k_ref, vt_ref, misc_ref, out_ref,
                        score_ref, kb_ref, ki_ref, vth_ref, qblk_ref, acc_ref, cut_ref, *, tq, k_top, idx_bits):
    j = pl.program_id(1)
    nk = j + 1
    ck = tq
    n_chunks = score_ref.shape[0]
    sub = 8
    group = A_HEADS // A_KV_HEADS
    head_rows = [slice(h * A_DH, (h + 1) * A_DH) for h in range(A_HEADS)]
    head_cols = [slice(h * tq, (h + 1) * tq) for h in range(A_HEADS)]
    kv_rows = [slice((h // group) * A_DH, (h // group + 1) * A_DH) for h in range(A_HEADS)]

    @pl.when(j == 0)
    def _():
        for c in range(n_chunks):
            rows = slice(c * ck, (c + 1) * ck)
            kb_ref[rows, :] = k_ref[rows, :].astype(BF16)
            ki_ref[rows, :] = misc_ref[rows, MISC_IK:MISC_IK + IDX_DH].astype(BF16)
            vth_ref[c] = vt_ref[:, rows].astype(BF16)
        qblk_ref[...] = jnp.zeros(qblk_ref.shape, BF16)

    key_in_chunk = lax.broadcasted_iota(I32, (ck, tq), 0)
    q_pos = j * tq + lax.broadcasted_iota(I32, (ck, tq), 1)

    iqb = (iqt_ref[...] * (IDX_DH ** -0.5)).astype(BF16)
    iq_cat = jnp.concatenate([iqb[h * IDX_DH:(h + 1) * IDX_DH, :] for h in range(IDX_HEADS)], axis=1)
    iw = iwt_ref[...]

    def idx_chunk(c, diagonal):
        start = pl.multiple_of(c * ck, ck)
        dots = _dg(ki_ref[pl.ds(start, ck), :], iq_cat)
        acc = jnp.zeros((ck, tq), F32)
        for h in range(IDX_HEADS):
            acc = acc + jnp.maximum(dots[:, h * tq:(h + 1) * tq], 0.0) * iw[h:h + 1, :]
        if diagonal:
            acc = jnp.where(c * ck + key_in_chunk <= q_pos, acc, -jnp.inf)
        score_ref[c] = _canon_zero(acc)

    def idx_body(c, carry):
        idx_chunk(c, False)
        return carry

    lax.fori_loop(0, j, idx_body, 0)
    idx_chunk(j, True)
    step = 1
    n_iter = nk

    key_in_group = lax.broadcasted_iota(I32, (sub, tq), 0)
    n_acc = 4

    def count_fn(preds):
        def body(it, cnts):
            cnts = [list(a) for a in cnts]
            for u in range(step):
                c = it * step + u
                for i in range(ck // sub):
                    sc = score_ref[c, i * sub:(i + 1) * sub, :]
                    key_id = c * ck + i * sub + key_in_group
                    for n, pred in enumerate(preds):
                        cnts[n][i % n_acc] = cnts[n][i % n_acc] + jnp.where(pred(sc, key_id), 1.0, 0.0)
            return tuple(tuple(a) for a in cnts)
        zero = tuple(jnp.zeros((sub, tq), F32) for _ in range(n_acc))
        cnts = lax.fori_loop(0, n_iter, body, tuple(zero for _ in preds))
        return [jnp.sum(sum(a), axis=0, keepdims=True) for a in cnts]

    bcast = lambda v: jnp.broadcast_to(v, (sub, tq))
    thr, need, n_eq = _radix_threshold(count_fn, (1, tq), bcast, k_top, 1)
    cut_ref[...] = jnp.full((1, tq), 2 ** idx_bits - 1, I32)

    @pl.when(jnp.max(n_eq - need) > 0.5)
    def _():
        cut_ref[...] = _tie_cut(count_fn, (1, tq), bcast, thr, need, idx_bits, 1)

    cut = cut_ref[...]

    def mask_body(c, carry):
        sc = score_ref[c]
        key_id = c * ck + key_in_chunk
        sel = ((sc > thr) | ((sc == thr) & (key_id <= cut))) & (key_id <= q_pos)
        score_ref[c] = jnp.where(sel, 0.0, -jnp.inf)
        return carry

    lax.fori_loop(0, nk, mask_body, 0)

    qb = (aqt_ref[...] * (A_DH ** -0.5)).astype(BF16)
    for h in range(A_HEADS):
        qblk_ref[kv_rows[h], head_cols[h]] = qb[head_rows[h], :]
    acc_ref[...] = jnp.zeros(acc_ref.shape, F32)

    def chunk_scores(c):
        start = pl.multiple_of(c * ck, ck)
        return _dg(kb_ref[pl.ds(start, ck), :], qblk_ref[...])

    def fold(x, acc, op):
        for i in range(ck // sub):
            acc = op(acc, x[i * sub:(i + 1) * sub, :])
        return acc

    def att_body(c, carry):
        ms, lparts = carry
        s_all = chunk_scores(c)
        bias = score_ref[c]
        scores = [s_all[:, head_cols[h]] + bias for h in range(A_HEADS)]
        m_new = [jnp.maximum(ms[h], jnp.max(fold(scores[h], jnp.full((sub, tq), -1e30, F32), jnp.maximum),
                                            axis=0, keepdims=True)) for h in range(A_HEADS)]
        alphas = [jnp.exp(ms[h] - m_new[h]) for h in range(A_HEADS)]
        probs, new_l = [], []
        for h in range(A_HEADS):
            p = jnp.exp(scores[h] - m_new[h])
            new_l.append(fold(p, alphas[h] * lparts[h], jnp.add))
            probs.append(p.astype(BF16))
        pv = _dg(vth_ref[c], jnp.concatenate(probs, axis=1))
        for h in range(A_HEADS):
            acc_ref[head_rows[h], :] = alphas[h] * acc_ref[head_rows[h], :] + pv[kv_rows[h], head_cols[h]]
        return tuple(m_new), tuple(new_l)

    init = (tuple(jnp.full((1, tq), -1e30, F32) for _ in range(A_HEADS)),
            tuple(jnp.zeros((sub, tq), F32) for _ in range(A_HEADS)))
    _, lparts = lax.fori_loop(0, nk, att_body, init)
    for h in range(A_HEADS):
        acc_ref[head_rows[h], :] = acc_ref[head_rows[h], :] / jnp.sum(lparts[h], axis=0, keepdims=True)
    out_ref[...] = acc_ref[...].T


def _prompt_attention(aqt, iqt, iwt, ak, avt, misc, tq):
    b, s, kvw = ak.shape
    nq = s // tq
    k_top = min(TOPK_MAX, s // 4)
    idx_bits = max(1, int(np.ceil(np.log2(s))))
    kern = functools.partial(_prompt_attn_kernel, tq=tq, k_top=k_top, idx_bits=idx_bits)
    qspec = lambda r: pl.BlockSpec((None, r, tq), lambda bi, j: (bi, 0, j))
    kspec = lambda w: pl.BlockSpec((None, s, w), lambda bi, j: (bi, 0, 0))
    return pl.pallas_call(
        kern,
        grid=(b, nq),
        in_specs=[qspec(A_HEADS * A_DH), qspec(IDX_HEADS * IDX_DH), qspec(_T_IW[1] - _T_IW[0]),
                  kspec(kvw), pl.BlockSpec((None, kvw, s), lambda bi, j: (bi, 0, 0)), kspec(LANES)],
        out_specs=pl.BlockSpec((None, tq, A_HEADS * A_DH), lambda bi, j: (bi, j, 0)),
        out_shape=jax.ShapeDtypeStruct((b, s, A_HEADS * A_DH), F32),
        scratch_shapes=[pltpu.VMEM((nq, tq, tq), F32),
                        pltpu.VMEM((s, kvw), BF16), pltpu.VMEM((s, IDX_DH), BF16),
                        pltpu.VMEM((nq, kvw, tq), BF16), pltpu.VMEM((kvw, A_HEADS * tq), BF16),
                        pltpu.VMEM((A_HEADS * A_DH, tq), F32), pltpu.VMEM((1, tq), I32)],
        compiler_params=_params(("arbitrary", "arbitrary")),
        name="prompt_attn",
    )(aqt, iqt, iwt, ak, avt, misc)


def _sample_attn_kernel(pt_ref, aq_ref, iq_ref, misc_ref, kn_ref, vn_ref, *refs, n_seq, n_pages, page, k_top, idx_bits):
    npg = n_seq * n_pages
    ki_refs = [refs[u * n_pages:(u + 1) * n_pages] for u in range(n_seq)]
    k_refs = [refs[npg + u * n_pages:npg + (u + 1) * n_pages] for u in range(n_seq)]
    v_refs = [refs[2 * npg + u * n_pages:2 * npg + (u + 1) * n_pages] for u in range(n_seq)]
    out_ref = refs[3 * npg]
    cut_ref = refs[3 * npg + 1]
    t = aq_ref.shape[1]
    rows = n_seq * t
    assert page == LANES and t <= LANES
    tok = jnp.concatenate([lax.broadcasted_iota(I32, (t, LANES), 0)] * n_seq, axis=0)
    lane_ids = lax.broadcasted_iota(I32, (rows, LANES), 1)
    new_ok = lane_ids <= tok
    group = A_HEADS // A_KV_HEADS

    page_scores = [[] for _ in range(n_pages + 1)]
    for u in range(n_seq):
        iqb = (iq_ref[u] * (IDX_DH ** -0.5)).astype(BF16)
        q4 = jnp.concatenate([iqb[:, h * IDX_DH:(h + 1) * IDX_DH] for h in range(IDX_HEADS)], axis=0)
        iw = misc_ref[u][:, MISC_IW:MISC_IW + IDX_HEADS]
        new_rows = jnp.concatenate([misc_ref[u], jnp.zeros((LANES - t, LANES), F32)], axis=0)
        dots = [_dg(q4, ki_refs[u][p][...].astype(BF16)) for p in range(n_pages)]
        dots.append(_dg(q4, new_rows[:, MISC_IK:MISC_IK + IDX_DH].astype(BF16), NT))
        for p, d in enumerate(dots):
            acc = jnp.zeros((t, LANES), F32)
            for h in range(IDX_HEADS):
                acc = acc + jnp.maximum(d[h * t:(h + 1) * t, :], 0.0) * iw[:, h:h + 1]
            page_scores[p].append(acc)
    keys = [_canon_zero(jnp.concatenate(ps, axis=0)) for ps in page_scores]
    keys[n_pages] = jnp.where(new_ok, keys[n_pages], -jnp.inf)

    def count_fn(preds):
        cnts = []
        for pred in preds:
            cnt = jnp.zeros((rows, LANES), F32)
            for p, kc in enumerate(keys):
                cnt = cnt + jnp.where(pred(kc, p * LANES + lane_ids), 1.0, 0.0)
            cnts.append(jnp.sum(cnt, axis=-1, keepdims=True))
        return cnts

    bcast = lambda v: jnp.broadcast_to(v, (rows, LANES))
    thr, need, n_eq = _radix_threshold(count_fn, (rows, 1), bcast, k_top, 4)
    cut_ref[...] = jnp.full((rows, 1), 2 ** idx_bits - 1, I32)

    @pl.when(jnp.max(n_eq - need) > 0.5)
    def _():
        cut_ref[...] = _tie_cut(count_fn, (rows, 1), bcast, thr, need, idx_bits, 4)

    cut = cut_ref[...]
    biases = []
    for p in range(n_pages + 1):
        cols = p * LANES + lane_ids
        sel = (keys[p] > thr) | ((keys[p] == thr) & (cols <= cut))
        if p == n_pages:
            sel = sel & new_ok
        biases.append(jnp.where(sel, 0.0, -jnp.inf))

    chs = [slice(g * A_DH, (g + 1) * A_DH) for g in range(A_KV_HEADS)]
    units = [(u, g) for u in range(n_seq) for g in range(A_KV_HEADS)]
    pad_kv = jnp.zeros((LANES - t, A_KV_HEADS * A_DH), F32)
    k_new = [jnp.concatenate([kn_ref[u], pad_kv], axis=0).astype(BF16) for u in range(n_seq)]
    v_new = [jnp.concatenate([vn_ref[u], pad_kv], axis=0).astype(BF16) for u in range(n_seq)]
    scores = []
    for u, g in units:
        qb = (aq_ref[u] * (A_DH ** -0.5)).astype(BF16)
        q_stack = jnp.concatenate([qb[:, (g * group + r) * A_DH:(g * group + r + 1) * A_DH] for r in range(group)], axis=0)
        sg = [_dg(q_stack, k_refs[u][p][chs[g], :].astype(BF16)) for p in range(n_pages)]
        sg.append(_dg(q_stack, k_new[u][:, chs[g]], NT))
        scores.append(sg)
    probs, dens = [], []
    for i, (u, g) in enumerate(units):
        sg = [s + jnp.concatenate([biases[p][u * t:(u + 1) * t, :]] * group, axis=0) for p, s in enumerate(scores[i])]
        mx = sg[0]
        for s in sg[1:]:
            mx = jnp.maximum(mx, s)
        mx = jnp.max(mx, axis=-1, keepdims=True)
        es = [jnp.exp(s - mx) for s in sg]
        dens.append(jnp.sum(sum(es), axis=-1, keepdims=True))
        probs.append([e.astype(BF16) for e in es])
    for i, (u, g) in enumerate(units):
        pvs = [_dg(probs[i][p], v_refs[u][p][chs[g], :].astype(BF16), NT) for p in range(n_pages)]
        pvs.append(_dg(probs[i][n_pages], v_new[u][:, chs[g]]))
        o = sum(pvs) / dens[i]
        for r in range(group):
            h = g * group + r
            out_ref[u, :, h * A_DH:(h + 1) * A_DH] = o[r * t:(r + 1) * t, :]


def _sample_attention(aq, ak, av, iq, misc, pool_kt, pool_vt, pool_kit, page_table, n_seq):
    db, t, _ = aq.shape
    assert db % n_seq == 0
    n_pages = page_table.shape[1]
    page = pool_kit.shape[2]
    total = n_pages * page + t
    k_top = min(TOPK_MAX, total // 4)
    idx_bits = max(1, int(np.ceil(np.log2((n_pages + 1) * LANES))))
    kern = functools.partial(_sample_attn_kernel, n_seq=n_seq, n_pages=n_pages, page=page, k_top=k_top,
                             idx_bits=idx_bits)
    qspec = lambda w: pl.BlockSpec((n_seq, t, w), lambda i, pt: (i, 0, 0))
    pspec = lambda w, u, p: pl.BlockSpec((None, w, page), lambda i, pt: (pt[i * n_seq + u, p], 0, 0))
    kvw = A_KV_HEADS * A_DH
    pages = [(u, p) for u in range(n_seq) for p in range(n_pages)]
    in_specs = [qspec(A_HEADS * A_DH), qspec(IDX_HEADS * IDX_DH), qspec(LANES), qspec(kvw), qspec(kvw)]
    in_specs += [pspec(IDX_DH, u, p) for u, p in pages]
    in_specs += [pspec(kvw, u, p) for u, p in pages]
    in_specs += [pspec(kvw, u, p) for u, p in pages]
    return pl.pallas_call(
        kern,
        grid_spec=pltpu.PrefetchScalarGridSpec(
            num_scalar_prefetch=1, grid=(db // n_seq,), in_specs=in_specs, out_specs=qspec(A_HEADS * A_DH),
            scratch_shapes=[pltpu.VMEM((n_seq * t, 1), I32)]),
        out_shape=jax.ShapeDtypeStruct((db, t, A_HEADS * A_DH), F32),
        compiler_params=_params(("arbitrary",)),
        name="sample_attn",
    )(page_table, aq, iq, misc, ak, av, *([pool_kit] * len(pages)), *([pool_kt] * len(pages)),
      *([pool_vt] * len(pages)))


def _gdn_kernel(u_ref, gz_ref, misc_ref, conv0_ref, s0_ref, convw_ref, alog_ref, dtb_ref, onorm_ref,
                out_ref, s_out_ref, conv_out_ref, state_ref, ubuf_ref, *, chunk, n_seq):
    c = pl.program_id(1)
    n_c = pl.num_programs(1)
    hist = CONV_W - 1
    base = 8

    @pl.when(c == 0)
    def _():
        state_ref[...] = s0_ref[...]
        ubuf_ref[:, base - hist:base, :] = conv0_ref[...]

    ii = lax.broadcasted_iota(I32, (chunk, chunk), 0)
    jj = lax.broadcasted_iota(I32, (chunk, chunk), 1)
    lower = jnp.where(ii >= jj, 1.0, 0.0).astype(BF16)
    upper = jnp.where(ii <= jj, 1.0, 0.0).astype(BF16)
    lane = lax.broadcasted_iota(I32, (chunk, LANES), 1)
    ga_lanes = (lane >= MISC_GA) & (lane < MISC_GA + G_HEADS)
    qk = G_HEADS * G_DK
    n_dbl = int(np.log2(chunk))
    assert 2 ** n_dbl == chunk

    per_seq = []
    for b in range(n_seq):
        ubuf_ref[b, base:base + chunk, :] = u_ref[b]
        conv = 0.0
        for jt in range(CONV_W):
            conv = conv + ubuf_ref[b, base - hist + jt:base - hist + jt + chunk, :] * convw_ref[jt:jt + 1, :]
        act = _silu(conv)
        tail = ubuf_ref[b, base + chunk - hist:base + chunk, :]
        ubuf_ref[b, base - hist:base, :] = tail
        conv_out_ref[b] = tail

        misc = misc_ref[b]
        beta_all = _sigmoid(misc)
        g_all = jnp.where(ga_lanes, -jnp.exp(alog_ref[...]) * _softplus(misc + dtb_ref[...]), 0.0)
        g1 = g_all.astype(BF16)
        r1 = g_all - g1.astype(F32)
        g2 = r1.astype(BF16)
        g3 = (r1 - g2.astype(F32)).astype(BF16)
        gc_col = _dg(lower, g1) + _dg(lower, g2) + _dg(lower, g3)
        gc_row = _dg(g1, upper, TN) + _dg(g2, upper, TN) + _dg(g3, upper, TN)
        per_seq.append((act, beta_all, gc_col, gc_row))

    chains = [(b, h) for b in range(n_seq) for h in range(G_HEADS)]
    qs, ks, kbs, gammas, egs, gcols, sols = [], [], [], [], [], [], []
    for b, h in chains:
        act, beta_all, gc_col, gc_row = per_seq[b]
        q = act[:, h * G_DK:(h + 1) * G_DK]
        k = act[:, qk + h * G_DK:qk + (h + 1) * G_DK]
        v = act[:, 2 * qk + h * G_DV:2 * qk + (h + 1) * G_DV]
        q = q * lax.rsqrt(jnp.sum(q * q, axis=-1, keepdims=True) + EPS) * (G_DK ** -0.5)
        k = k * lax.rsqrt(jnp.sum(k * k, axis=-1, keepdims=True) + EPS)
        beta = beta_all[:, MISC_GB + h:MISC_GB + h + 1]
        gcol = gc_col[:, MISC_GA + h:MISC_GA + h + 1]
        grow = gc_row[MISC_GA + h:MISC_GA + h + 1, :]
        eg = jnp.exp(gcol)
        kb = k * beta
        qs.append(q)
        ks.append(k)
        kbs.append(kb)
        gammas.append(jnp.exp(jnp.where(ii >= jj, gcol - grow, -jnp.inf)))
        egs.append(eg)
        gcols.append(gcol)
        sols.append(jnp.concatenate([v * beta, kb * eg], axis=-1))
    n_ch = len(chains)
    kk = [_dot1(kbs[i], ks[i], NT) for i in range(n_ch)]
    pws = [jnp.where(ii > jj, -(kk[i] * gammas[i]), 0.0) for i in range(n_ch)]
    for it in range(n_dbl):
        upd = [_dot3(pws[i], sols[i]) for i in range(n_ch)]
        if it + 1 < n_dbl:
            pws = [_dot1(pws[i], pws[i]) for i in range(n_ch)]
        sols = [sols[i] + upd[i] for i in range(n_ch)]
    aqk = [_dot1(qs[i], ks[i], NT) for i in range(n_ch)]
    sts = [state_ref[b, h] for b, h in chains]
    ws = [_dot1(sols[i][:, G_DV:], sts[i]) for i in range(n_ch)]
    o1 = [_dot1(qs[i] * egs[i], sts[i]) for i in range(n_ch)]
    v_new = [sols[i][:, :G_DV] - ws[i] for i in range(n_ch)]
    o2 = [_dot1(aqk[i] * gammas[i], v_new[i]) for i in range(n_ch)]
    g_last = [gcols[i][chunk - 1:chunk, :] for i in range(n_ch)]
    kv = [_dot1(ks[i] * jnp.exp(g_last[i] - gcols[i]), v_new[i], TN) for i in range(n_ch)]
    for i, (b, h) in enumerate(chains):
        state_ref[b, h] = sts[i] * jnp.exp(g_last[i]) + kv[i]
        o = _rms_rows(o1[i] + o2[i], onorm_ref[...]) * _silu(gz_ref[b, :, h * G_DV:(h + 1) * G_DV])
        out_ref[b, :, h * G_DV:(h + 1) * G_DV] = o

    @pl.when(c == n_c - 1)
    def _():
        s_out_ref[...] = state_ref[...]


def _gdn(gqkv, gz, misc, conv0, s0, g_conv, g_a_log, g_dt_bias, g_o_norm, n_seq):
    b, t, cch = gqkv.shape
    chunk = min(G_CHUNK, t)
    assert t % chunk == 0 and chunk % 8 == 0 and b % n_seq == 0
    hist = CONV_W - 1
    alog = jnp.zeros((1, LANES), F32).at[0, MISC_GA:MISC_GA + G_HEADS].set(g_a_log.astype(F32))
    dtb = jnp.zeros((1, LANES), F32).at[0, MISC_GA:MISC_GA + G_HEADS].set(g_dt_bias.astype(F32))
    kern = functools.partial(_gdn_kernel, chunk=chunk, n_seq=n_seq)
    tspec = lambda w: pl.BlockSpec((n_seq, chunk, w), lambda bi, c: (bi, c, 0))
    sspec = pl.BlockSpec((n_seq, G_HEADS, G_DK, G_DV), lambda bi, c: (bi, 0, 0, 0))
    cspec = pl.BlockSpec((n_seq, hist, cch), lambda bi, c: (bi, 0, 0))
    return pl.pallas_call(
        kern,
        grid=(b // n_seq, t // chunk),
        in_specs=[tspec(cch), tspec(G_HEADS * G_DV), tspec(LANES), cspec, sspec,
                  _const_spec((CONV_W, cch)), _const_spec((1, LANES)), _const_spec((1, LANES)),
                  _const_spec((1, G_DV))],
        out_specs=[tspec(G_HEADS * G_DV), sspec, cspec],
        out_shape=[jax.ShapeDtypeStruct((b, t, G_HEADS * G_DV), F32),
                   jax.ShapeDtypeStruct((b, G_HEADS, G_DK, G_DV), F32),
                   jax.ShapeDtypeStruct((b, hist, cch), F32)],
        scratch_shapes=[pltpu.VMEM((n_seq, G_HEADS, G_DK, G_DV), F32), pltpu.VMEM((n_seq, 8 + chunk, cch), F32)],
        compiler_params=_params(("arbitrary", "arbitrary")),
        name="gdn",
    )(gqkv, gz, misc, conv0, s0, g_conv, alog, dtb, g_o_norm.reshape(1, G_DV))


def _mem_kv_kernel(mem_ref, gain_ref, w_ref, kg_ref, mk_ref, mv_ref):
    xn = _rms_rows(mem_ref[...], gain_ref[...]).astype(BF16)
    half = M_HEADS * M_DH
    kk = jnp.dot(xn, w_ref[:, :half], preferred_element_type=F32)
    for c, y in enumerate(_headnorm_full(kk, kg_ref[...])):
        mk_ref[:, c * LANES:(c + 1) * LANES] = y
    mv_ref[...] = jnp.dot(xn, w_ref[:, half:], preferred_element_type=F32)


def _mem_kv(mem2d, mem_norm, w_mem_kv, m_k_norm, tm):
    n, d = mem2d.shape
    half = M_HEADS * M_DH
    return pl.pallas_call(
        _mem_kv_kernel,
        grid=(n // tm,),
        in_specs=[pl.BlockSpec((tm, d), lambda i: (i, 0)), _const_spec((1, d)), _const_spec((d, 2 * half)),
                  _const_spec((1, M_DH))],
        out_specs=[pl.BlockSpec((tm, half), lambda i: (i, 0))] * 2,
        out_shape=[jax.ShapeDtypeStruct((n, half), F32)] * 2,
        compiler_params=_params(("parallel",)),
        name="mem_kv",
    )(mem2d, mem_norm.reshape(1, d), w_mem_kv.astype(BF16), m_k_norm.reshape(1, M_DH))


def _mem_attn_kernel(q_ref, k_ref, v_ref, out_ref):
    n_seq, t, _ = q_ref.shape
    n_rows = k_ref.shape[1]
    q_head = jnp.concatenate([jnp.full((t, n_rows), h, I32) for h in range(M_HEADS)], axis=0)
    row_id = lax.broadcasted_iota(I32, (M_HEADS * t, n_rows), 1)
    row_head = row_id & (M_HEADS - 1) if M_HEADS & (M_HEADS - 1) == 0 else row_id % M_HEADS
    own = q_head == row_head
    q_stacks = [jnp.concatenate([q_ref[u, :, h * M_DH:(h + 1) * M_DH] for h in range(M_HEADS)], axis=0).astype(BF16)
                for u in range(n_seq)]
    scores = [_dg(q_stacks[u], k_ref[u].astype(BF16), NT) for u in range(n_seq)]
    probs = []
    for u in range(n_seq):
        s = jnp.where(own, scores[u] * (M_DH ** -0.5), -jnp.inf)
        e = jnp.exp(s - jnp.max(s, axis=-1, keepdims=True))
        probs.append((e / jnp.sum(e, axis=-1, keepdims=True)).astype(BF16))
    outs = [_dg(probs[u], v_ref[u].astype(BF16)) for u in range(n_seq)]
    for u in range(n_seq):
        for h in range(M_HEADS):
            out_ref[u, :, h * M_DH:(h + 1) * M_DH] = outs[u][h * t:(h + 1) * t, :]


def _mem_attn_wide_kernel(q_ref, k_ref, v_ref, out_ref, kblk_ref, vblk_ref):
    n_mem = k_ref.shape[0]

    @pl.when(pl.program_id(1) == 0)
    def _():
        kblk_ref[...] = jnp.zeros(kblk_ref.shape, BF16)
        vblk_ref[...] = jnp.zeros(vblk_ref.shape, BF16)
        for h in range(M_HEADS):
            sl = slice(h * M_DH, (h + 1) * M_DH)
            kblk_ref[h * n_mem:(h + 1) * n_mem, sl] = k_ref[:, sl].astype(BF16)
            vblk_ref[h * n_mem:(h + 1) * n_mem, sl] = v_ref[:, sl].astype(BF16)

    s_all = _dg(q_ref[...].astype(BF16), kblk_ref[...], NT) * (M_DH ** -0.5)
    probs = []
    for h in range(M_HEADS):
        s = s_all[:, h * n_mem:(h + 1) * n_mem]
        e = jnp.exp(s - jnp.max(s, axis=-1, keepdims=True))
        probs.append((e / jnp.sum(e, axis=-1, keepdims=True)).astype(BF16))
    out_ref[...] = _dg(jnp.concatenate(probs, axis=1), vblk_ref[...])


def _mem_attention_wide(mq, mk, mv, tm):
    b, t, w = mq.shape
    m = mk.shape[1]
    return pl.pallas_call(
        _mem_attn_wide_kernel,
        grid=(b, t // tm),
        in_specs=[pl.BlockSpec((None, tm, w), lambda bi, i: (bi, i, 0)),
                  pl.BlockSpec((None, m, w), lambda bi, i: (bi, 0, 0)),
                  pl.BlockSpec((None, m, w), lambda bi, i: (bi, 0, 0))],
        out_specs=pl.BlockSpec((None, tm, w), lambda bi, i: (bi, i, 0)),
        out_shape=jax.ShapeDtypeStruct((b, t, w), F32),
        scratch_shapes=[pltpu.VMEM((M_HEADS * m, w), BF16), pltpu.VMEM((M_HEADS * m, w), BF16)],
        compiler_params=_params(("arbitrary", "arbitrary")),
        name="mem_attn_wide",
    )(mq, mk, mv)


def _mem_attention(mq, mk, mv, n_seq):
    b, t, w = mq.shape
    m, kw = mk.shape[1], mk.shape[2]
    assert b % n_seq == 0
    return pl.pallas_call(
        _mem_attn_kernel,
        grid=(b // n_seq,),
        in_specs=[pl.BlockSpec((n_seq, t, w), lambda i: (i, 0, 0)),
                  pl.BlockSpec((n_seq, m, kw), lambda i: (i, 0, 0)),
                  pl.BlockSpec((n_seq, m, kw), lambda i: (i, 0, 0))],
        out_specs=pl.BlockSpec((n_seq, t, w), lambda i: (i, 0, 0)),
        out_shape=jax.ShapeDtypeStruct((b, t, w), F32),
        compiler_params=_params(("parallel",)),
        name="mem_attn",
    )(mq, mk, mv)


def _merge_ffn_kernel(x_ref, a_ref, g_ref, m_ref, gates_ref, wa_ref, wg_ref, wm_ref, wo_ref, nf_ref,
                      win_ref, wout_ref, y_ref, *, d_ff, ff_chunk):
    d = x_ref.shape[1]
    gates = gates_ref[...]
    h = (gates[:, :d] * jnp.dot(a_ref[...].astype(BF16), wa_ref[...], preferred_element_type=F32)
         + gates[:, d:2 * d] * jnp.dot(g_ref[...].astype(BF16), wg_ref[...], preferred_element_type=F32)
         + gates[:, 2 * d:] * jnp.dot(m_ref[...].astype(BF16), wm_ref[...], preferred_element_type=F32))
    x1 = x_ref[...] + jnp.dot(h.astype(BF16), wo_ref[...], preferred_element_type=F32)
    xn = _rms_rows(x1, nf_ref[...]).astype(BF16)
    acc = jnp.zeros_like(x1)
    for c in range(d_ff // ff_chunk):
        lo = c * ff_chunk
        gate = jnp.dot(xn, win_ref[:, lo:lo + ff_chunk], preferred_element_type=F32)
        up = jnp.dot(xn, win_ref[:, d_ff + lo:d_ff + lo + ff_chunk], preferred_element_type=F32)
        acc = acc + jnp.dot((_silu(gate) * up).astype(BF16), wout_ref[lo:lo + ff_chunk, :],
                            preferred_element_type=F32)
    y_ref[...] = x1 + acc


def _merge_ffn(x2d, a_out, g_out, m_out, gates, w_a, w_g, w_m, w_o, norm_ffn, w_in, w_out, tm):
    n, d = x2d.shape
    d_ff = w_out.shape[0]
    ff_chunk = 2 * LANES
    assert d_ff % ff_chunk == 0
    kern = functools.partial(_merge_ffn_kernel, d_ff=d_ff, ff_chunk=ff_chunk)
    row = lambda w: pl.BlockSpec((tm, w), lambda i: (i, 0))
    return pl.pallas_call(
        kern,
        grid=(n // tm,),
        in_specs=[row(d), row(a_out.shape[1]), row(g_out.shape[1]), row(m_out.shape[1]), row(3 * d),
                  _const_spec(w_a.shape), _const_spec(w_g.shape), _const_spec(w_m.shape), _const_spec(w_o.shape),
                  _const_spec((1, d)), _const_spec(w_in.shape), _const_spec(w_out.shape)],
        out_specs=row(d),
        out_shape=jax.ShapeDtypeStruct((n, d), F32),
        compiler_params=_params(("parallel",)),
        name="merge_ffn",
    )(x2d, a_out, g_out, m_out, gates, w_a, w_g, w_m, w_o, norm_ffn.reshape(1, d), w_in, w_out)


def _tile(n, pref):
    t = min(n, pref)
    assert n % t == 0
    return t


def kernel(x_prompt, x_sample, mem_prompt, cache_k, cache_v, cache_idx_k, page_table, state_gdn, state_conv,
           cache_mem_k, cache_mem_v, norm_mix, w_in, a_q_norm, a_k_norm, g_conv, g_a_log, g_dt_bias, g_o_norm,
           mem_norm, w_mem_kv, m_q_norm, m_k_norm, w_a_out, w_g_out, w_m_out, w_o, norm_ffn, w_ffn_in, w_ffn_out):
    depth = w_in.shape[0]
    b, s, d = x_prompt.shape
    db, t, _ = x_sample.shape
    n_mem = mem_prompt.shape[1]
    kvw = A_KV_HEADS * A_DH
    cch = g_conv.shape[2]
    yp, ys = x_prompt, x_sample
    p_states, s_states = [], []
    for l in range(depth):
        bf = lambda w: w.astype(BF16)
        proj_w = (norm_mix[l], _pack_w_in(w_in[l]), _pack_w_in_t(w_in[l]), a_q_norm[l], a_k_norm[l], m_q_norm[l])
        ffn_w = (bf(w_a_out[l]), bf(w_g_out[l]), bf(w_m_out[l]), bf(w_o[l]), norm_ffn[l], bf(w_ffn_in[l]),
                 bf(w_ffn_out[l]))
        gdn_w = (g_conv[l], g_a_log[l], g_dt_bias[l], g_o_norm[l])

        x2 = yp.reshape(b * s, d)
        tq = _tile(s, ATTN_Q_TILE)
        aqt, iqt, iwt, avt, ak, misc, gqkv, gz, mq, gates = _inproj(x2, *proj_w, tm=_tile(s, TOKEN_TILE), seq=s)
        r3 = lambda a: a.reshape(b, s, a.shape[-1])
        a_out = _prompt_attention(aqt, iqt, iwt, r3(ak), avt, r3(misc), tq=tq)
        av = jnp.transpose(avt.reshape(b, A_KV_HEADS, A_DH, s), (0, 3, 1, 2))
        g_out, p_gdn, p_conv = _gdn(r3(gqkv), r3(gz), r3(misc),
                                    jnp.zeros((b, CONV_W - 1, cch), F32),
                                    jnp.zeros((b, G_HEADS, G_DK, G_DV), F32), *gdn_w, n_seq=b)
        mk, mv = _mem_kv(mem_prompt.reshape(b * n_mem, d), mem_norm[l], w_mem_kv[l], m_k_norm[l],
                         tm=_tile(b * n_mem, TOKEN_TILE))
        m_out = _mem_attention_wide(r3(mq), mk.reshape(b, n_mem, -1), mv.reshape(b, n_mem, -1),
                                    tm=_tile(s, MEM_Q_TILE))
        y2 = _merge_ffn(x2, a_out.reshape(b * s, -1), g_out.reshape(b * s, -1), m_out.reshape(b * s, -1), gates,
                        *ffn_w, tm=_tile(b * s, TOKEN_TILE))
        yp = y2.reshape(b, s, d)
        p_states.append((ak.reshape(b, s, A_KV_HEADS, A_DH), av,
                         misc[:, MISC_IK:MISC_IK + IDX_DH].reshape(b, s, IDX_DH), p_gdn, p_conv,
                         mk.reshape(b, n_mem, M_HEADS, M_DH), mv.reshape(b, n_mem, M_HEADS, M_DH)))

        xs2 = ys.reshape(db * t, d)
        aq, ak, av, iq, misc, gqkv, gz, mq, gates = _inproj(xs2, *proj_w, tm=_tile(db * t, TOKEN_TILE))
        r3 = lambda a: a.reshape(db, t, a.shape[-1])
        n_phys, page = cache_idx_k.shape[1], cache_idx_k.shape[2]
        pool_t = lambda c: jnp.transpose(c, (0, 2, 3, 1)).reshape(n_phys, kvw, page)
        a_out = _sample_attention(r3(aq), r3(ak), r3(av), r3(iq), r3(misc), pool_t(cache_k[l]), pool_t(cache_v[l]),
                                  jnp.transpose(cache_idx_k[l], (0, 2, 1)), page_table,
                                  n_seq=_tile(db, DECODE_ATTN_SEQS))
        g_out, s_gdn, s_conv = _gdn(r3(gqkv), r3(gz), r3(misc), state_conv[l], state_gdn[l], *gdn_w,
                                    n_seq=_tile(db, DECODE_GDN_SEQS))
        m_out = _mem_attention(r3(mq), cache_mem_k[l].reshape(db, n_mem * M_HEADS, M_DH),
                               cache_mem_v[l].reshape(db, n_mem * M_HEADS, M_DH), n_seq=_tile(db, DECODE_MEM_SEQS))
        y2 = _merge_ffn(xs2, a_out.reshape(db * t, -1), g_out.reshape(db * t, -1), m_out.reshape(db * t, -1), gates,
                        *ffn_w, tm=_tile(db * t, TOKEN_TILE))
        ys = y2.reshape(db, t, d)
        s_states.append((ak.reshape(db, t, A_KV_HEADS, A_DH), av.reshape(db, t, A_KV_HEADS, A_DH),
                         misc[:, MISC_IK:MISC_IK + IDX_DH].reshape(db, t, IDX_DH), s_gdn, s_conv))

    p_k, p_v, p_idx_k, p_gdn, p_conv, p_mem_k, p_mem_v = [jnp.stack(z) for z in zip(*p_states)]
    s_k, s_v, s_idx_k, s_gdn, s_conv = [jnp.stack(z) for z in zip(*s_states)]
    return (yp, ys, p_k, p_v, p_idx_k, p_gdn, p_conv, p_mem_k, p_mem_v, s_k, s_v, s_idx_k, s_gdn, s_conv)
```
